```python
import jax, jax.numpy as jnp
from jax import lax
import numpy as np

D_MODEL = 2048
BATCH = 4
SEQ = 2048
DEPTH = 1
DEC_BATCH = 16
DEC_SEQ = 2048
PAST_LEN = 128

ML_WIDTH = D_MODEL // 2
ML_HEADS = 4
ML_DH = ML_WIDTH // ML_HEADS
ML_CHUNK = 128
CONV_WIDTH = 3
HG_WIDTH = D_MODEL - ML_WIDTH
HG_HEADS = 8
HG_DK = HG_WIDTH // HG_HEADS
HG_DV = HG_WIDTH // HG_HEADS
HG_CHUNK = 32
N_EXPERTS = 16
EC_FACTOR = 2
EXPERT_FF = 2048
ALPHA = (2.0 * DEPTH) ** 0.25
OUT_INIT_SCALE = (8.0 * DEPTH) ** -0.25
EPS = 1e-5

COL_SIZES = [ML_WIDTH] * 4 + [ML_HEADS] * 4 + [HG_WIDTH] * 5
IN_COLS = sum(COL_SIZES)
COL_SPLITS = [int(c) for c in np.cumsum(COL_SIZES)[:-1]]

kernel_name = "hymba_mlstm_hgrn2_ec_encoder"

F32 = jnp.float32


def layer_norm(x, g, b):
    xf = x.astype(F32)
    mu = xf.mean(-1, keepdims=True)
    var = jnp.mean(jnp.square(xf - mu), -1, keepdims=True)
    y = (xf - mu) * lax.rsqrt(var + EPS) * g.astype(F32) + b.astype(F32)
    return y.astype(x.dtype)


def to_heads(a, n_heads):
    B, L, W = a.shape
    return a.reshape(B, L, n_heads, W // n_heads).transpose(0, 2, 1, 3)


def merge_heads(a):
    B, H, L, d = a.shape
    return a.transpose(0, 2, 1, 3).reshape(B, L, H * d)


def flip_seq(a):
    return jnp.flip(a, axis=2)


def to_chunks(a, c):
    B, H, L = a.shape[:3]
    a = a.reshape(B, H, L // c, c, *a.shape[3:])
    return jnp.moveaxis(a, 2, 0)


def from_chunks(a):
    a = jnp.moveaxis(a, 0, 2)
    B, H, nc, c, d = a.shape
    return a.reshape(B, H, nc * c, d)


def depthwise_conv_centred(x, w):
    C = x.shape[-1]
    return lax.conv_general_dilated(
        x, w.astype(x.dtype)[:, None, :], window_strides=(1,), padding="SAME",
        dimension_numbers=("NWC", "WIO", "NWC"), feature_group_count=C)


def mlstm_direction(q, k, v, log_i, log_f):
    B, H, L, d = q.shape
    c = ML_CHUNK
    xs = (to_chunks(q, c), to_chunks(k, c), to_chunks(v, c), to_chunks(log_i, c), to_chunks(log_f, c))
    causal = jnp.tril(jnp.ones((c, c), bool))

    def step(carry, xs_c):
        C, n, m = carry
        q_, k_, v_, i_, f_ = xs_c
        b = jnp.cumsum(f_, axis=-1)
        a_inter = b + m[..., None]
        D = jnp.where(causal, b[..., :, None] - b[..., None, :] + i_[..., None, :], -jnp.inf)
        m_t = jnp.maximum(a_inter, D.max(-1))
        w_inter = jnp.exp(a_inter - m_t)
        s = jnp.einsum('bhtd,bhsd->bhts', q_, k_) * jnp.exp(D - m_t[..., None])
        num = (w_inter[..., None] * jnp.einsum('bhtd,bhde->bhte', q_, C)
               + jnp.einsum('bhts,bhse->bhte', s, v_))
        den = w_inter * jnp.einsum('bhtd,bhd->bht', q_, n) + s.sum(-1)
        h = num / jnp.maximum(jnp.abs(den), jnp.exp(-m_t))[..., None]
        bL = b[..., -1]
        g = bL[..., None] - b + i_
        m_new = jnp.maximum(bL + m, g.max(-1))
        decay = jnp.exp(bL + m - m_new)
        wg = jnp.exp(g - m_new[..., None])
        C_new = decay[..., None, None] * C + jnp.einsum('bhs,bhsd,bhse->bhde', wg, k_, v_)
        n_new = decay[..., None] * n + jnp.einsum('bhs,bhsd->bhd', wg, k_)
        return (C_new, n_new, m_new), h

    init = (jnp.zeros((B, H, d, d), F32), jnp.zeros((B, H, d), F32), jnp.full((B, H), -1e30, F32))
    _, hs = lax.scan(step, init, xs)
    return from_chunks(hs)


def hgrn2_direction(q, log_f, key, v):
    B, H, L, dk = q.shape
    dv = v.shape[-1]
    c = HG_CHUNK
    xs = (to_chunks(q, c), to_chunks(log_f, c), to_chunks(key, c), to_chunks(v, c))
    causal = jnp.tril(jnp.ones((c, c), bool))[:, :, None]

    def step(S, xs_c):
        q_, lf_, k_, v_ = xs_c
        b = jnp.cumsum(lf_, axis=2)
        inter = jnp.einsum('bhtk,bhkv->bhtv', q_ * jnp.exp(b), S)
        rel = jnp.where(causal, b[:, :, :, None, :] - b[:, :, None, :, :], -jnp.inf)
        A = jnp.einsum('bhtk,bhsk,bhtsk->bhts', q_, k_, jnp.exp(rel))
        intra = jnp.einsum('bhts,bhsv->bhtv', A, v_)
        bL = b[:, :, -1]
        S_new = (jnp.exp(bL)[..., None] * S
                 + jnp.einsum('bhsk,bhsv->bhkv', k_ * jnp.exp(bL[:, :, None] - b), v_))
        return S_new, inter + intra

    _, os_ = lax.scan(step, jnp.zeros((B, H, dk, dv), F32), xs)
    return from_chunks(os_)


def token_mixer(h, w_in, conv_w, ml_ib, ml_fb, ml_norm_w, lb, hg_norm_w, w_out):
    proj = h @ w_in
    (mq, mk, mv, mo, mi_f, mi_b, mf_f, mf_b, hq, hf_f, hf_b, hi, hg) = jnp.split(proj, COL_SPLITS, axis=-1)

    qk = jax.nn.silu(depthwise_conv_centred(jnp.concatenate([mq, mk], -1), conv_w))
    cq, ck = jnp.split(qk, 2, axis=-1)
    q = to_heads(cq.astype(F32), ML_HEADS)
    k = to_heads(ck.astype(F32), ML_HEADS) * (ML_DH ** -0.5)
    v = to_heads(mv.astype(F32), ML_HEADS)

    def gate(pre, bias):
        return (pre.astype(F32) + bias.astype(F32)).transpose(0, 2, 1)

    li_f, li_b = gate(mi_f, ml_ib[0]), gate(mi_b, ml_ib[1])
    lf_f, lf_b = jax.nn.log_sigmoid(gate(mf_f, ml_fb[0])), jax.nn.log_sigmoid(gate(mf_b, ml_fb[1]))
    h_fwd = mlstm_direction(q, k, v, li_f, lf_f)
    h_bwd = flip_seq(mlstm_direction(flip_seq(q), flip_seq(k), flip_seq(v), flip_seq(li_b), flip_seq(lf_b)))
    hm = h_fwd + h_bwd
    mu = hm.mean(-1, keepdims=True)
    var = jnp.mean(jnp.square(hm - mu), -1, keepdims=True)
    hm = merge_heads((hm - mu) * lax.rsqrt(var + EPS))
    ml_out = hm * ml_norm_w.astype(F32) * jax.nn.sigmoid(mo.astype(F32))

    lb = lb.astype(F32)

    def forget(pre):
        pre = pre.astype(F32)
        log_f = jnp.log(lb + (1.0 - lb) * jax.nn.sigmoid(pre))
        key = (1.0 - lb) * jax.nn.sigmoid(-pre)
        return to_heads(log_f, HG_HEADS), to_heads(key, HG_HEADS)

    glf_f, gk_f = forget(hf_f)
    glf_b, gk_b = forget(hf_b)
    gq = to_heads(jax.nn.silu(hq.astype(F32)), HG_HEADS)
    gv = to_heads(hi.astype(F32), HG_HEADS)
    o = (hgrn2_direction(gq, glf_f, gk_f, gv)
         + flip_seq(hgrn2_direction(flip_seq(gq), flip_seq(glf_b), flip_seq(gk_b), flip_seq(gv))))
    o = o * lax.rsqrt(jnp.mean(jnp.square(o), -1, keepdims=True) + EPS)
    hg_out = merge_heads(o) * hg_norm_w.astype(F32) * jax.nn.silu(hg.astype(F32))

    merged = jnp.concatenate([ml_out, hg_out], axis=-1).astype(h.dtype)
    return merged @ w_out


def expert_choice_ffn(x, w_router, w_gate, w_up, w_down):
    B, L, D = x.shape
    n = B * L
    cap = EC_FACTOR * n // N_EXPERTS
    xt = x.reshape(n, D)
    aff = jax.nn.softmax((xt @ w_router).astype(F32), axis=-1)
    g, idx = lax.top_k(aff.T, cap)
    xe = xt[idx]
    hid = jax.nn.silu(jnp.einsum('ecd,edf->ecf', xe, w_gate)) * jnp.einsum('ecd,edf->ecf', xe, w_up)
    ye = jnp.einsum('ecf,efd->ecd', hid, w_down) * g[..., None].astype(x.dtype)
    y = jnp.zeros_like(xt).at[idx.reshape(-1)].add(ye.reshape(-1, D))
    return y.reshape(B, L, D)


def encoder_trunk(x, emb_ln_g, emb_ln_b, w_in, conv_w, ml_igate_b, ml_fgate_b, ml_norm_w,
                  hg_lb, hg_norm_w, w_out, ln1_g, ln1_b, w_router, w_gate, w_up, w_down, ln2_g, ln2_b):
    x = layer_norm(x, emb_ln_g, emb_ln_b)
    lower_bounds = jnp.cumsum(jax.nn.softmax(hg_lb.astype(F32), axis=0), axis=0)
    for l in range(DEPTH):
        mix = token_mixer(x, w_in[l], conv_w[l], ml_igate_b[l], ml_fgate_b[l], ml_norm_w[l],
                          lower_bounds[l], hg_norm_w[l], w_out[l])
        x = layer_norm(ALPHA * x + mix, ln1_g[l], ln1_b[l])
        ffn = expert_choice_ffn(x, w_router[l], w_gate[l], w_up[l], w_down[l])
        x = layer_norm(ALPHA * x + ffn, ln2_g[l], ln2_b[l])
    return x


def setup_inputs(seed: int = 0) -> dict:
    key = jax.random.key(seed)
    ks = jax.random.split(key, 24)
    nrm = jax.random.normal
    D, E, F = D_MODEL, N_EXPERTS, EXPERT_FF
    fgate_b = (jnp.linspace(3.0, 6.0, ML_HEADS, dtype=F32)[None, None, :]
               + 0.1 * nrm(ks[6], (DEPTH, 2, ML_HEADS), F32))
    return {
        "x_prompt": nrm(ks[0], (BATCH, SEQ, D), F32),
        "x_sample": nrm(ks[1], (DEC_BATCH, DEC_SEQ, D), F32),
        "emb_ln_g": 1.0 + 0.02 * nrm(ks[2], (D,), F32),
        "emb_ln_b": 0.02 * nrm(ks[3], (D,), F32),
        "w_in": nrm(ks[4], (DEPTH, D, IN_COLS), F32) * D ** -0.5,
        "conv_w": nrm(ks[5], (DEPTH, CONV_WIDTH, 2 * ML_WIDTH), F32) * CONV_WIDTH ** -0.5,
        "ml_igate_b": 0.1 * nrm(ks[7], (DEPTH, 2, ML_HEADS), F32),
        "ml_fgate_b": fgate_b,
        "ml_norm_w": 1.0 + 0.02 * nrm(ks[8], (DEPTH, ML_WIDTH), F32),
        "hg_lb": 0.5 * nrm(ks[9], (DEPTH + 1, HG_WIDTH), F32),
        "hg_norm_w": 1.0 + 0.02 * nrm(ks[10], (DEPTH, HG_WIDTH), F32),
        "w_out": nrm(ks[11], (DEPTH, D, D), F32) * (D ** -0.5 * OUT_INIT_SCALE),
        "ln1_g": 1.0 + 0.02 * nrm(ks[12], (DEPTH, D), F32),
        "ln1_b": 0.02 * nrm(ks[13], (DEPTH, D), F32),
        "w_router": nrm(ks[14], (DEPTH, D, E), F32) * D ** -0.5,
        "w_gate": nrm(ks[15], (DEPTH, E, D, F), F32) * D ** -0.5,
        "w_up": nrm(ks[16], (DEPTH, E, D, F), F32) * D ** -0.5,
        "w_down": nrm(ks[17], (DEPTH, E, F, D), F32) * (F ** -0.5 * OUT_INIT_SCALE),
        "ln2_g": 1.0 + 0.02 * nrm(ks[18], (DEPTH, D), F32),
        "ln2_b": 0.02 * nrm(ks[19], (DEPTH, D), F32),
    }


def reference(x_prompt, x_sample, emb_ln_g, emb_ln_b, w_in, conv_w, ml_igate_b, ml_fgate_b, ml_norm_w,
              hg_lb, hg_norm_w, w_out, ln1_g, ln1_b, w_router, w_gate, w_up, w_down, ln2_g, ln2_b):
    y_prompt = encoder_trunk(x_prompt, emb_ln_g, emb_ln_b, w_in, conv_w, ml_igate_b, ml_fgate_b, ml_norm_w,
                             hg_lb, hg_norm_w, w_out, ln1_g, ln1_b, w_router, w_gate, w_up, w_down, ln2_g, ln2_b)
    y_sample = encoder_trunk(x_sample, emb_ln_g, emb_ln_b, w_in, conv_w, ml_igate_b, ml_fgate_b, ml_norm_w,
                             hg_lb, hg_norm_w, w_out, ln1_g, ln1_b, w_router, w_gate, w_up, w_down, ln2_g, ln2_b)
    return (y_prompt, y_sample)
```

```python
import functools

import jax
import jax.numpy as jnp
from jax import lax
from jax.experimental import pallas as pl
from jax.experimental.pallas import tpu as pltpu

F32 = jnp.float32
BF16 = jnp.bfloat16
I32 = jnp.int32

D_MODEL = 2048
ML_WIDTH = 1024
ML_HEADS = 4
ML_DH = 256
HG_WIDTH = 1024
HG_HEADS = 8
HG_D = 128
N_EXPERTS = 16
EC_FACTOR = 2
EXPERT_FF = 2048
DEPTH = 1
ALPHA = (2.0 * DEPTH) ** 0.25
EPS = 1e-5

LANES = 128
SUBLANES = 8
MAIN_COLS = 9 * 1024
VMEM_LIMIT = 56 * 1024 * 1024

ML_CHUNK = 256
HG_CHUNK = 128
HG_BASE = 8


def _cparams(sem):
    return pltpu.CompilerParams(dimension_semantics=sem, vmem_limit_bytes=VMEM_LIMIT)


def _split_bf16(x):
    hi = x.astype(BF16)
    lo = (x - hi.astype(F32)).astype(BF16)
    return hi, lo


def _layer_norm(x, g, b):
    mu = jnp.mean(x, axis=-1, keepdims=True)
    xc = x - mu
    var = jnp.mean(xc * xc, axis=-1, keepdims=True)
    return xc * lax.rsqrt(var + EPS) * g + b


def _sigmoid(x):
    return 1.0 / (1.0 + jnp.exp(-x))


def _silu(x):
    return x * _sigmoid(x)


def _log_sigmoid(x):
    return jnp.minimum(x, 0.0) - jnp.log1p(jnp.exp(-jnp.abs(x)))


def _dot(a, b):
    return jnp.dot(a, b, preferred_element_type=F32)


def _dot_nt(a, b):
    return lax.dot_general(a, b, (((1,), (1,)), ((), ())), preferred_element_type=F32)


def _dot_tn(a, b):
    return lax.dot_general(a, b, (((0,), (0,)), ((), ())), preferred_element_type=F32)


def _ln_proj_kernel(x_ref, g_ref, b_ref, w_ref, wgh_ref, wgl_ref, proj_ref, gate_ref, hh_ref, hl_ref):
    @pl.when(pl.program_id(1) == 0)
    def _():
        h = _layer_norm(x_ref[...], g_ref[...], b_ref[...])
        hh, hl = _split_bf16(h)
        hh_ref[...] = hh
        hl_ref[...] = hl
        gate_ref[...] = _dot(hh, wgh_ref[...]) + _dot(hl, wgh_ref[...]) + _dot(hh, wgl_ref[...])

    proj_ref[...] = _dot(hh_ref[...], w_ref[...]).astype(proj_ref.dtype)


def _ln_proj(x, g, b, w_main, wg_hi, wg_lo, tm=512, tn=1024):
    n, d = x.shape
    ncol = w_main.shape[1]
    return pl.pallas_call(
        _ln_proj_kernel,
        grid=(n // tm, ncol // tn),
        in_specs=[
            pl.BlockSpec((tm, d), lambda i, j: (i, 0)),
            pl.BlockSpec((1, d), lambda i, j: (0, 0)),
            pl.BlockSpec((1, d), lambda i, j: (0, 0)),
            pl.BlockSpec((d, tn), lambda i, j: (0, j)),
            pl.BlockSpec((d, LANES), lambda i, j: (0, 0)),
            pl.BlockSpec((d, LANES), lambda i, j: (0, 0)),
        ],
        out_specs=[
            pl.BlockSpec((tm, tn), lambda i, j: (i, j)),
            pl.BlockSpec((tm, LANES), lambda i, j: (i, 0)),
        ],
        out_shape=[
            jax.ShapeDtypeStruct((n, ncol), F32),
            jax.ShapeDtypeStruct((n, LANES), F32),
        ],
        scratch_shapes=[pltpu.VMEM((tm, d), BF16), pltpu.VMEM((tm, d), BF16)],
        compiler_params=_cparams(("arbitrary", "arbitrary")),
        name="ln_proj",
    )(x, g, b, w_main, wg_hi, wg_lo)


def _mlstm_chunk(qb, kb, vb, f_col, f_row, i_col, i_row, c_state, n_state, m_state, rev):
    c = qb.shape[0]
    tt = lax.broadcasted_iota(I32, (c, c), 0)
    ss = lax.broadcasted_iota(I32, (c, c), 1)
    causal = (ss >= tt) if rev else (ss <= tt)
    b_col = jnp.sum(jnp.where(causal, f_row, 0.0), axis=1, keepdims=True)
    anti = (tt >= ss) if rev else (tt <= ss)
    b_row = jnp.sum(jnp.where(anti, f_col, 0.0), axis=0, keepdims=True)
    total = jnp.sum(f_row, axis=1, keepdims=True)

    dmat = jnp.where(causal, b_col - b_row + i_row, -jnp.inf)
    a_inter = b_col + m_state
    m_t = jnp.maximum(a_inter, jnp.max(dmat, axis=1, keepdims=True))
    w_inter = jnp.exp(a_inter - m_t)
    s = _dot_nt(qb, kb) * jnp.exp(dmat - m_t)
    num = w_inter * _dot(qb, c_state.astype(BF16)) + _dot(s.astype(BF16), vb)
    qn = jnp.sum(qb.astype(F32) * n_state, axis=1, keepdims=True)
    den = w_inter * qn + jnp.sum(s, axis=1, keepdims=True)
    h = num / jnp.maximum(jnp.abs(den), jnp.exp(-m_t))

    g_col = total - b_col + i_col
    g_row = total - b_row + i_row
    m_new = jnp.maximum(total + m_state, jnp.max(g_row, axis=1, keepdims=True))
    decay = jnp.exp(total + m_state - m_new)
    ks = kb.astype(F32) * jnp.exp(g_col - m_new)
    c_new = decay * c_state + _dot_tn(ks.astype(BF16), vb)
    n_new = decay * n_state + jnp.sum(ks, axis=0, keepdims=True)
    return h, c_new, n_new, m_new


def _mlstm_kernel(bias_ref, q_ref, k_ref, v_ref, o_ref, cwq_ref, cwk_ref, gc_ref, gr_ref, nw_ref,
                  out_ref, qc_ref, kc_ref, vc_ref, hf_ref, hb_ref, cs_ref):
    L, d = q_ref.shape
    c = ML_CHUNK
    nc = L // c
    hd = pl.program_id(1)

    row = lax.broadcasted_iota(I32, (L, 1), 0)

    def conv_silu(x, w):
        xp = jnp.where(row == 0, 0.0, pltpu.roll(x, 1, 0))
        xn = jnp.where(row == L - 1, 0.0, pltpu.roll(x, L - 1, 0))
        return _silu(w[0:1, :] * xp + w[1:2, :] * x + w[2:3, :] * xn)

    qc_ref[...] = conv_silu(q_ref[...], cwq_ref[...]).astype(BF16)
    kc_ref[...] = (conv_silu(k_ref[...], cwk_ref[...]) * (ML_DH ** -0.5)).astype(BF16)
    vc_ref[...] = v_ref[...].astype(BF16)

    bi_f = bias_ref[hd]
    bi_b = bias_ref[ML_HEADS + hd]
    bf_f = bias_ref[2 * ML_HEADS + hd]
    bf_b = bias_ref[3 * ML_HEADS + hd]

    cs_ref[...] = jnp.zeros_like(cs_ref)

    def gates(t0):
        gc = gc_ref[0, 0, pl.ds(t0, c), :]
        gr = gr_ref[0, 0, :, pl.ds(t0, c)]
        return gc, gr

    def body(j, carry):
        n_f, m_f, n_b, m_b = carry
        t0 = pl.multiple_of(j * c, c)
        gc, gr = gates(t0)
        h, c_new, n_f, m_f = _mlstm_chunk(
            qc_ref[pl.ds(t0, c), :], kc_ref[pl.ds(t0, c), :], vc_ref[pl.ds(t0, c), :],
            _log_sigmoid(gc[:, 2:3] + bf_f), _log_sigmoid(gr[2:3, :] + bf_f),
            gc[:, 0:1] + bi_f, gr[0:1, :] + bi_f,
            cs_ref[0], n_f, m_f, rev=False)
        cs_ref[0] = c_new
        hf_ref[pl.ds(t0, c), :] = h
        t1 = pl.multiple_of((nc - 1 - j) * c, c)
        gc, gr = gates(t1)
        h, c_new, n_b, m_b = _mlstm_chunk(
            qc_ref[pl.ds(t1, c), :], kc_ref[pl.ds(t1, c), :], vc_ref[pl.ds(t1, c), :],
            _log_sigmoid(gc[:, 3:4] + bf_b), _log_sigmoid(gr[3:4, :] + bf_b),
            gc[:, 1:2] + bi_b, gr[1:2, :] + bi_b,
            cs_ref[1], n_b, m_b, rev=True)
        cs_ref[1] = c_new
        hb_ref[pl.ds(t1, c), :] = h
        return n_f, m_f, n_b, m_b

    zn = jnp.zeros((1, d), F32)
    m0 = jnp.full((1, 1), -1e30, F32)
    lax.fori_loop(0, nc, body, (zn, m0, zn, m0))

    hm = hf_ref[...] + hb_ref[...]
    mu = jnp.mean(hm, axis=1, keepdims=True)
    hc = hm - mu
    var = jnp.mean(hc * hc, axis=1, keepdims=True)
    y = hc * lax.rsqrt(var + EPS) * nw_ref[...] * _sigmoid(o_ref[...])
    out_ref[...] = y.astype(out_ref.dtype)


def _mlstm(proj, conv_w, gate_col, gate_row, bias, norm_w, B, L):
    d = ML_DH
    H = ML_HEADS
    grid_spec = pltpu.PrefetchScalarGridSpec(
        num_scalar_prefetch=1,
        grid=(B, H),
        in_specs=[
            pl.BlockSpec((L, d), lambda b, h, s: (b, h)),
            pl.BlockSpec((L, d), lambda b, h, s: (b, H + h)),
            pl.BlockSpec((L, d), lambda b, h, s: (b, 2 * H + h)),
            pl.BlockSpec((L, d), lambda b, h, s: (b, 3 * H + h)),
            pl.BlockSpec((3, d), lambda b, h, s: (0, h)),
            pl.BlockSpec((3, d), lambda b, h, s: (0, H + h)),
            pl.BlockSpec((1, 1, L, 4), lambda b, h, s: (b, h, 0, 0)),
            pl.BlockSpec((1, 1, 4, L), lambda b, h, s: (b, h, 0, 0)),
            pl.BlockSpec((1, d), lambda b, h, s: (0, h)),
        ],
        out_specs=pl.BlockSpec((L, d), lambda b, h, s: (b, h)),
        scratch_shapes=[
            pltpu.VMEM((L, d), BF16), pltpu.VMEM((L, d), BF16), pltpu.VMEM((L, d), BF16),
            pltpu.VMEM((L, d), F32), pltpu.VMEM((L, d), F32),
            pltpu.VMEM((2, d, d), F32),
        ],
    )
    return pl.pallas_call(
        _mlstm_kernel,
        grid_spec=grid_spec,
        out_shape=jax.ShapeDtypeStruct((B * L, ML_WIDTH), BF16),
        compiler_params=_cparams(("arbitrary", "arbitrary")),
        name="mlstm",
    )(bias, proj, proj, proj, proj, conv_w, conv_w, gate_col, gate_row, norm_w)


def _chunk_cumsum(x, rev):
    c = x.shape[0]
    row = lax.broadcasted_iota(I32, x.shape, 0)
    shift = 1
    while shift < c:
        if rev:
            moved = jnp.where(row < c - shift, pltpu.roll(x, c - shift, 0), 0.0)
        else:
            moved = jnp.where(row >= shift, pltpu.roll(x, shift, 0), 0.0)
        x = x + moved
        shift *= 2
    return x


def _hgrn2_chunk(q, k, v, logf, st, rev):
    c, dk = q.shape
    b = _chunk_cumsum(logf, rev)
    vb = v.astype(BF16)

    nb = c // HG_BASE
    b3 = b.reshape(nb, HG_BASE, dk)
    q3 = q.reshape(nb, HG_BASE, dk)
    k3 = k.reshape(nb, HG_BASE, dk)
    v3 = v.reshape(nb, HG_BASE, dk)
    u3 = lax.broadcasted_iota(I32, (nb, HG_BASE, dk), 1)
    o3 = jnp.zeros((nb, HG_BASE, dk), F32)
    for j in range(HG_BASE):
        keep = (u3 <= j) if rev else (u3 >= j)
        e = jnp.exp(jnp.where(keep, b3 - b3[:, j:j + 1, :], -jnp.inf))
        a = jnp.sum(q3 * e * k3[:, j:j + 1, :], axis=2, keepdims=True)
        o3 = o3 + a * v3[:, j:j + 1, :]
    o = o3.reshape(c, dk)

    tt = lax.broadcasted_iota(I32, (c, c), 0)
    ss = lax.broadcasted_iota(I32, (c, c), 1)
    amat = jnp.zeros((c, c), F32)
    m = HG_BASE
    while m < c:
        blk = 2 * m
        nblk = c // blk
        bb = b.reshape(nblk, blk, dk)
        ub = lax.broadcasted_iota(I32, (nblk, blk, dk), 1)
        if rev:
            ref = bb[:, m:m + 1, :]
            tgt = ub < m
        else:
            ref = bb[:, m - 1:m, :]
            tgt = ub >= m
        e = jnp.exp(-jnp.abs(bb - ref)).reshape(c, dk)
        tgt = tgt.reshape(c, dk)
        qe = jnp.where(tgt, q * e, 0.0).astype(BF16)
        ke = jnp.where(tgt, 0.0, k * e).astype(BF16)
        same = (tt // blk) == (ss // blk)
        amat = amat + jnp.where(same, _dot_nt(qe, ke), 0.0)
        m = blk
    o = o + _dot(amat.astype(BF16), vb)

    edge = b[0:1, :] if rev else b[c - 1:c, :]
    o = o + _dot_nt((q * jnp.exp(b)).astype(BF16), st.astype(BF16))
    kl = (k * jnp.exp(edge - b)).astype(BF16)
    st_new = jnp.exp(edge) * st + _dot_tn(vb, kl)
    return o, st_new


def _hgrn2_kernel(q_ref, ff_ref, fb_ref, v_ref, g_ref, lb_ref, nw_ref, out_ref, of_ref, ob_ref, st_ref):
    L, dk = q_ref.shape
    c = HG_CHUNK
    nc = L // c

    lbp = lb_ref[...]
    mx = jnp.max(lbp, axis=0, keepdims=True)
    ex = jnp.exp(lbp - mx)
    lb = ex[0:1, :] / jnp.sum(ex, axis=0, keepdims=True)

    st_ref[...] = jnp.zeros_like(st_ref)

    def one(t0, f_ref, slot, rev, dst_ref):
        pre = f_ref[pl.ds(t0, c), :]
        logf = jnp.log(lb + (1.0 - lb) * _sigmoid(pre))
        key = (1.0 - lb) * _sigmoid(-pre)
        q = _silu(q_ref[pl.ds(t0, c), :])
        o, st_new = _hgrn2_chunk(q, key, v_ref[pl.ds(t0, c), :], logf, st_ref[slot], rev)
        st_ref[slot] = st_new
        dst_ref[pl.ds(t0, c), :] = o

    def body(j, carry):
        one(pl.multiple_of(j * c, c), ff_ref, 0, False, of_ref)
        one(pl.multiple_of((nc - 1 - j) * c, c), fb_ref, 1, True, ob_ref)
        return carry

    lax.fori_loop(0, nc, body, 0)

    o = of_ref[...] + ob_ref[...]
    o = o * lax.rsqrt(jnp.mean(o * o, axis=1, keepdims=True) + EPS)
    y = o * nw_ref[...] * _silu(g_ref[...])
    out_ref[...] = y.astype(out_ref.dtype)


def _hgrn2(proj, hg_lb, norm_w, B, L):
    dk = HG_D
    H = HG_HEADS
    base = 4 * ML_WIDTH // dk
    return pl.pallas_call(
        _hgrn2_kernel,
        grid=(B, H),
        in_specs=[
            pl.BlockSpec((L, dk), lambda b, h: (b, base + h)),
            pl.BlockSpec((L, dk), lambda b, h: (b, base + H + h)),
            pl.BlockSpec((L, dk), lambda b, h: (b, base + 2 * H + h)),
            pl.BlockSpec((L, dk), lambda b, h: (b, base + 3 * H + h)),
            pl.BlockSpec((L, dk), lambda b, h: (b, base + 4 * H + h)),
            pl.BlockSpec((2, dk), lambda b, h: (0, h)),
            pl.BlockSpec((1, dk), lambda b, h: (0, h)),
        ],
        out_specs=pl.BlockSpec((L, dk), lambda b, h: (b, h)),
        out_shape=jax.ShapeDtypeStruct((B * L, HG_WIDTH), BF16),
        scratch_shapes=[
            pltpu.VMEM((L, dk), F32), pltpu.VMEM((L, dk), F32),
            pltpu.VMEM((2, dk, dk), F32),
        ],
        compiler_params=_cparams(("arbitrary", "arbitrary")),
        name="hgrn2",
    )(proj, proj, proj, proj, proj, hg_lb, norm_w)


def _outproj_kernel(x_ref, ml_ref, hg_ref, eg_ref, eb_ref, wo_ref, g1_ref, b1_ref, wrh_ref, wrl_ref,
                    x1_ref, lg_ref):
    half = ml_ref.shape[1]
    mix = _dot(ml_ref[...], wo_ref[0:half, :]) + _dot(hg_ref[...], wo_ref[half:, :])
    h = _layer_norm(x_ref[...], eg_ref[...], eb_ref[...])
    x1 = _layer_norm(ALPHA * h + mix, g1_ref[...], b1_ref[...])
    x1_ref[...] = x1
    xh, xl = _split_bf16(x1)
    lg_ref[...] = _dot(xh, wrh_ref[...]) + _dot(xl, wrh_ref[...]) + _dot(xh, wrl_ref[...])


def _outproj(x, ml, hg, eg, eb, wo, g1, b1, wr_hi, wr_lo, tm=256):
    n, d = x.shape
    row = lambda i: (i, 0)
    fixed = lambda i: (0, 0)
    return pl.pallas_call(
        _outproj_kernel,
        grid=(n // tm,),
        in_specs=[
            pl.BlockSpec((tm, d), row),
            pl.BlockSpec((tm, ML_WIDTH), row),
            pl.BlockSpec((tm, HG_WIDTH), row),
            pl.BlockSpec((1, d), fixed),
            pl.BlockSpec((1, d), fixed),
            pl.BlockSpec((d, d), fixed),
            pl.BlockSpec((1, d), fixed),
            pl.BlockSpec((1, d), fixed),
            pl.BlockSpec((d, LANES), fixed),
            pl.BlockSpec((d, LANES), fixed),
        ],
        out_specs=[pl.BlockSpec((tm, d), row), pl.BlockSpec((tm, LANES), row)],
        out_shape=[jax.ShapeDtypeStruct((n, d), F32), jax.ShapeDtypeStruct((n, LANES), F32)],
        compiler_params=_cparams(("arbitrary",)),
        name="outproj",
    )(x, ml, hg, eg, eb, wo, g1, b1, wr_hi, wr_lo)


def _pad_cols(w, width=LANES):
    return jnp.pad(w, ((0, 0), (0, width - w.shape[1])))


def _token_mixer_stage(x, emb_ln_g, emb_ln_b, w_in, conv_w, ml_igate_b, ml_fgate_b, ml_norm_w,
                       hg_lb, hg_norm_w, w_out, ln1_g, ln1_b, w_router, B, L):
    d = D_MODEL
    g0 = 4 * ML_WIDTH
    w_main = jnp.concatenate([w_in[:, :g0], w_in[:, g0 + 16:]], axis=1).astype(BF16)
    wg_hi, wg_lo = _split_bf16(_pad_cols(w_in[:, g0:g0 + 16]))
    eg = emb_ln_g.reshape(1, d)
    eb = emb_ln_b.reshape(1, d)
    proj, gates = _ln_proj(x, eg, eb, w_main, wg_hi, wg_lo)

    g4 = gates[:, :16].reshape(B, L, 4, ML_HEADS)
    gate_col = g4.transpose(0, 3, 1, 2)
    gate_row = g4.transpose(0, 3, 2, 1)
    bias = jnp.concatenate([ml_igate_b[0], ml_igate_b[1], ml_fgate_b[0], ml_fgate_b[1]]).astype(F32)
    ml = _mlstm(proj, conv_w, gate_col, gate_row, bias, ml_norm_w.reshape(1, ML_WIDTH), B, L)
    hg = _hgrn2(proj, hg_lb, hg_norm_w.reshape(1, HG_WIDTH), B, L)

    wr_hi, wr_lo = _split_bf16(_pad_cols(w_router))
    return _outproj(x, ml, hg, eg, eb, w_out.astype(BF16), ln1_g.reshape(1, d), ln1_b.reshape(1, d),
                    wr_hi, wr_lo)


def _excl_token_cumsum(mask):
    E, R, ln = mask.shape
    mf = jnp.where(mask, 1.0, 0.0)
    mb = mf.astype(BF16)
    upper = jnp.where(lax.broadcasted_iota(I32, (ln, ln), 0) <= lax.broadcasted_iota(I32, (ln, ln), 1),
                      1.0, 0.0).astype(BF16)
    lower = jnp.where(lax.broadcasted_iota(I32, (R, R), 1) < lax.broadcasted_iota(I32, (R, R), 0),
                      1.0, 0.0).astype(BF16)
    ones = jnp.ones((ln, ln), BF16)
    within = _dot(mb.reshape(E * R, ln), upper).reshape(E, R, ln)
    rows = jnp.stack([_dot(_dot(lower, mb[e]).astype(BF16), ones) for e in range(E)], axis=0)
    return within - mf + rows


def _select_kernel(lg_ref, pos_ref, posm_ref, wts_ref, *, cap):
    E = lg_ref.shape[0]
    lg = lg_ref[...]
    mx = jnp.max(lg, axis=0, keepdims=True)
    ex = jnp.exp(lg - mx)
    aff = ex / jnp.sum(ex, axis=0, keepdims=True)

    def count(mask):
        ones = jnp.where(mask, 1.0, 0.0)
        return jnp.sum(jnp.sum(ones, axis=2, keepdims=True), axis=1, keepdims=True)

    def body(i, tbits):
        cand = tbits | lax.shift_left(jnp.int32(1), 30 - i)
        cnt = count(aff >= lax.bitcast_convert_type(cand, F32))
        return jnp.where(cnt >= cap, cand, tbits)

    tbits = lax.fori_loop(0, 31, body, jnp.zeros((E, 1, 1), I32))
    thr = lax.bitcast_convert_type(tbits, F32)
    nxt = lax.bitcast_convert_type(tbits + 1, F32)
    above = aff >= nxt
    band = jnp.logical_and(aff >= thr, jnp.logical_not(above))
    need = cap - count(above)
    sel = jnp.logical_or(above, jnp.logical_and(band, _excl_token_cumsum(band) < need))
    pos = _excl_token_cumsum(sel).astype(I32)
    pos_ref[...] = pos
    posm_ref[...] = jnp.where(sel, pos, -1)
    wts_ref[...] = jnp.where(sel, aff, 0.0)


def _select(lg_t, cap):
    E, R, ln = lg_t.shape
    full = pl.BlockSpec((E, R, ln), lambda i: (0, 0, 0))
    return pl.pallas_call(
        functools.partial(_select_kernel, cap=cap),
        grid=(1,),
        in_specs=[full],
        out_specs=[full, full, full],
        out_shape=[jax.ShapeDtypeStruct((E, R, ln), I32), jax.ShapeDtypeStruct((E, R, ln), I32),
                   jax.ShapeDtypeStruct((E, R, ln), F32)],
        compiler_params=_cparams(("arbitrary",)),
        name="select",
    )(lg_t)


TOK_TILE = 256
SLOT_CHUNK = 64


def _dispatch_kernel(off_ref, base_ref, cnt_ref, xa_ref, xb_ref, pos_ref, xe_ref,
                     x16_ref, stage_ref, ostage_ref, carry_ref, sem, osem, *, tiles_a, cap_total, pad):
    E = N_EXPERTS
    CH = SLOT_CHUNK
    SUB = SUBLANES
    tc = xa_ref.shape[0]
    t = pl.program_id(0)
    nt = pl.num_programs(0)
    slot = lax.rem(t, 2)

    @pl.when(t < tiles_a)
    def _():
        x16_ref[...] = xa_ref[...].astype(BF16)

    @pl.when(t >= tiles_a)
    def _():
        x16_ref[...] = xb_ref[...].astype(BF16)

    xb = x16_ref[...]

    @pl.when(t == 0)
    def _():
        carry_ref[...] = jnp.zeros_like(carry_ref)

    base = [base_ref[t * E + e] for e in range(E)]
    cnt = [cnt_ref[t * E + e] for e in range(E)]
    al = [pl.multiple_of((b // SUB) * SUB, SUB) for b in base]
    first = [off_ref[t * E + e] - (base[e] - al[e]) for e in range(E)]

    def onehot(e, start, rows):
        kio = lax.broadcasted_iota(I32, (rows, tc), 0)
        rel = pos_ref[e:e + 1, :] - (first[e] + start)
        return jnp.where(rel == kio, 1.0, 0.0).astype(BF16)

    ot = jnp.concatenate([onehot(e, 0, CH) for e in range(E)], axis=0)
    stage_ref[slot] = _dot(ot, xb)
    for e in range(E):
        stage_ref[slot, e * CH:e * CH + SUB, :] += carry_ref[e * SUB:(e + 1) * SUB, :]

    nxt = [((base[e] + cnt[e]) // SUB) * SUB - al[e] for e in range(E)]
    oc = jnp.concatenate([onehot(e, nxt[e], SUB) for e in range(E)], axis=0)
    new_carry = _dot(oc, xb)
    for e in range(E):
        keep = jnp.where(nxt[e] == 0, 1.0, 0.0)
        carry_ref[e * SUB:(e + 1) * SUB, :] = new_carry[e * SUB:(e + 1) * SUB, :] + keep * carry_ref[e * SUB:(e + 1) * SUB, :]

    def main_copy(s, e, row):
        return pltpu.make_async_copy(stage_ref.at[s, pl.ds(e * CH, CH)], xe_ref.at[pl.ds(row, CH)], sem.at[s])

    @pl.when(t > 0)
    def _():
        for e in range(E):
            main_copy(1 - slot, e, 0).wait()

    for e in range(E):
        main_copy(slot, e, al[e]).start()

    for e in range(E):
        nch = (base[e] - al[e] + cnt[e] + CH - 1) // CH

        def body(c, carry, e=e):
            ostage_ref[...] = _dot(onehot(e, c * CH, CH), xb)
            row = pl.multiple_of(al[e] + c * CH, SUB)
            cp = pltpu.make_async_copy(ostage_ref, xe_ref.at[pl.ds(row, CH)], osem)
            cp.start()
            cp.wait()
            return carry

        lax.fori_loop(1, nch, body, 0)

    @pl.when(t == nt - 1)
    def _():
        for e in range(E):
            main_copy(slot, e, 0).wait()
        ostage_ref[...] = jnp.zeros_like(ostage_ref)
        fills = [pltpu.make_async_copy(ostage_ref, xe_ref.at[pl.ds(e * (cap_total + pad) + cap_total + j * CH, CH)], osem)
                 for e in range(E) for j in range(pad // CH)]
        for cp in fills:
            cp.start()
        for cp in fills:
            cp.wait()


def _dispatch(off, base, cnt, x1_a, x1_b, pos_rows, cap_total, pad):
    d = x1_a.shape[1]
    E = N_EXPERTS
    tc = TOK_TILE
    tiles_a = x1_a.shape[0] // tc
    tiles_b = x1_b.shape[0] // tc
    grid_spec = pltpu.PrefetchScalarGridSpec(
        num_scalar_prefetch=3,
        grid=(tiles_a + tiles_b,),
        in_specs=[
            pl.BlockSpec((tc, d), lambda i, *_: (jnp.minimum(i, tiles_a - 1), 0)),
            pl.BlockSpec((tc, d), lambda i, *_: (jnp.maximum(i - tiles_a, 0), 0)),
            pl.BlockSpec((E, tc), lambda i, *_: (0, i)),
        ],
        out_specs=pl.BlockSpec(memory_space=pl.ANY),
        scratch_shapes=[
            pltpu.VMEM((tc, d), BF16),
            pltpu.VMEM((2, E * SLOT_CHUNK, d), F32),
            pltpu.VMEM((SLOT_CHUNK, d), F32),
            pltpu.VMEM((E * SUBLANES, d), F32),
            pltpu.SemaphoreType.DMA((2,)),
            pltpu.SemaphoreType.DMA(()),
        ],
    )
    return pl.pallas_call(
        functools.partial(_dispatch_kernel, tiles_a=tiles_a, cap_total=cap_total, pad=pad),
        grid_spec=grid_spec,
        out_shape=jax.ShapeDtypeStruct((E * (cap_total + pad), d), F32),
        compiler_params=_cparams(("arbitrary",)),
        name="dispatch",
    )(off, base, cnt, x1_a, x1_b, pos_rows)


def _ffn_kernel(xe_ref, wg_ref, wu_ref, wd_ref, out_ref, xb_ref):
    f = pl.program_id(2)

    @pl.when(f == 0)
    def _():
        xb_ref[...] = xe_ref[0].astype(BF16)

    xb = xb_ref[...]
    hid = (_silu(_dot(xb, wg_ref[0])) * _dot(xb, wu_ref[0])).astype(BF16)
    part = _dot(hid, wd_ref[0])

    @pl.when(f == 0)
    def _():
        out_ref[0] = part

    @pl.when(f > 0)
    def _():
        out_ref[0] += part


def _ffn(xe, wg, wu, wd, cap_total, tf=512):
    E, _, d = xe.shape
    ff = wg.shape[2]
    tm = next(t for t in (512, 256, 128) if cap_total % t == 0)
    return pl.pallas_call(
        _ffn_kernel,
        grid=(E, cap_total // tm, ff // tf),
        in_specs=[
            pl.BlockSpec((1, tm, d), lambda e, i, f: (e, i, 0)),
            pl.BlockSpec((1, d, tf), lambda e, i, f: (e, 0, f)),
            pl.BlockSpec((1, d, tf), lambda e, i, f: (e, 0, f)),
            pl.BlockSpec((1, tf, d), lambda e, i, f: (e, f, 0)),
        ],
        out_specs=pl.BlockSpec((1, tm, d), lambda e, i, f: (e, i, 0)),
        out_shape=jax.ShapeDtypeStruct((E, cap_total, d), F32),
        scratch_shapes=[pltpu.VMEM((tm, d), BF16)],
        compiler_params=_cparams(("arbitrary", "arbitrary", "arbitrary")),
        name="ffn",
    )(xe, wg, wu, wd)


def _combine_kernel(base_ref, cnt_ref, x1_ref, grow_ref, w_ref, g2_ref, b2_ref, ye_ref, out_ref,
                    buf_ref, obuf_ref, acc_ref, sem, osem, *, rows_total):
    E = N_EXPERTS
    CH = SLOT_CHUNK
    PACK = 4
    tc = x1_ref.shape[0]
    t = pl.program_id(0)
    last = rows_total - CH
    SUB = SUBLANES

    def chunk_copy(e, start):
        return pltpu.make_async_copy(ye_ref.at[pl.ds(start, CH)], buf_ref.at[pl.ds(e * CH, CH)], sem.at[e])

    al = [(base_ref[t * E + e] // SUB) * SUB for e in range(E)]
    starts = [pl.multiple_of(jnp.minimum(al[e], last), SUB) for e in range(E)]
    for e in range(E):
        chunk_copy(e, starts[e]).start()

    kio = lax.broadcasted_iota(I32, (tc, CH), 1)
    acc_ref[...] = ALPHA * x1_ref[...]

    def weights(e, start, lo):
        grow = grow_ref[:, e:e + 1]
        hit = jnp.logical_and(grow - start == kio, grow >= lo)
        return jnp.where(hit, w_ref[:, e:e + 1], 0.0).astype(BF16)

    for g in range(E // PACK):
        es = range(g * PACK, (g + 1) * PACK)
        for e in es:
            chunk_copy(e, 0).wait()
        a = jnp.concatenate([weights(e, starts[e], 0) for e in es], axis=1)
        rows = buf_ref[g * PACK * CH:(g + 1) * PACK * CH, :].astype(BF16)
        acc_ref[...] += _dot(a, rows)

    for e in range(E):
        nch = (base_ref[t * E + e] - al[e] + cnt_ref[t * E + e] + CH - 1) // CH

        def body(c, carry, e=e):
            lo = al[e] + c * CH
            start = pl.multiple_of(jnp.minimum(lo, last), SUB)
            cp = pltpu.make_async_copy(ye_ref.at[pl.ds(start, CH)], obuf_ref, osem)
            cp.start()
            cp.wait()
            acc_ref[...] += _dot(weights(e, start, lo), obuf_ref[...].astype(BF16))
            return carry

        lax.fori_loop(1, nch, body, 0)

    out_ref[...] = _layer_norm(acc_ref[...], g2_ref[...], b2_ref[...])


def _combine(base, cnt, x1, grow_t, w_t, g2, b2, ye, rows_total):
    n, d = x1.shape
    E = N_EXPERTS
    tc = TOK_TILE
    grid_spec = pltpu.PrefetchScalarGridSpec(
        num_scalar_prefetch=2,
        grid=(n // tc,),
        in_specs=[
            pl.BlockSpec((tc, d), lambda i, *_: (i, 0)),
            pl.BlockSpec((tc, E), lambda i, *_: (i, 0)),
            pl.BlockSpec((tc, E), lambda i, *_: (i, 0)),
            pl.BlockSpec((1, d), lambda i, *_: (0, 0)),
            pl.BlockSpec((1, d), lambda i, *_: (0, 0)),
            pl.BlockSpec(memory_space=pl.ANY),
        ],
        out_specs=pl.BlockSpec((tc, d), lambda i, *_: (i, 0)),
        scratch_shapes=[
            pltpu.VMEM((E * SLOT_CHUNK, d), F32),
            pltpu.VMEM((SLOT_CHUNK, d), F32),
            pltpu.VMEM((tc, d), F32),
            pltpu.SemaphoreType.DMA((E,)),
            pltpu.SemaphoreType.DMA(()),
        ],
    )
    return pl.pallas_call(
        functools.partial(_combine_kernel, rows_total=rows_total),
        grid_spec=grid_spec,
        out_shape=jax.ShapeDtypeStruct((n, d), F32),
        compiler_params=_cparams(("arbitrary",)),
        name="combine",
    )(base, cnt, x1, grow_t, w_t, g2, b2, ye)


def _routing_tables(pos, posm, wts, cap, slot0, cap_total, pad):
    E = N_EXPERTS
    n = pos.shape[1] * pos.shape[2]
    tiles = n // TOK_TILE
    eidx = jnp.arange(E, dtype=I32)
    off = pos.reshape(E, n)[:, ::TOK_TILE].T
    cnt = jnp.concatenate([off[1:], jnp.full((1, E), cap, I32)], axis=0) - off
    base_x = off + eidx[None, :] * (cap_total + pad) + slot0
    base_y = off + eidx[None, :] * cap_total + slot0
    pos_rows = posm.reshape(E, n)
    grow_t = jnp.where(pos_rows >= 0, pos_rows + eidx[:, None] * cap_total + slot0, -1).T
    w_t = wts.reshape(E, n).T
    flat = lambda a: a.reshape(tiles * E)
    return flat(off), flat(base_x), flat(base_y), flat(cnt), pos_rows, grow_t, w_t


def kernel(x_prompt, x_sample, emb_ln_g, emb_ln_b, w_in, conv_w, ml_igate_b, ml_fgate_b, ml_norm_w, hg_lb, hg_norm_w, w_out, ln1_g, ln1_b, w_router, w_gate, w_up, w_down, ln2_g, ln2_b):
    E = N_EXPERTS
    d = D_MODEL
    groups = (x_prompt, x_sample)
    caps = [EC_FACTOR * x.shape[0] * x.shape[1] // E for x in groups]
    cap_total = sum(caps)
    pad = 2 * SLOT_CHUNK
    assert all(c % SUBLANES == 0 for c in caps)

    staged = []
    tables = []
    slot0 = 0
    for x, cap in zip(groups, caps):
        B, L, _ = x.shape
        n = B * L
        x1, lg = _token_mixer_stage(x.reshape(n, d), emb_ln_g, emb_ln_b, w_in[0], conv_w[0], ml_igate_b[0],
                                    ml_fgate_b[0], ml_norm_w[0], hg_lb, hg_norm_w[0], w_out[0], ln1_g[0],
                                    ln1_b[0], w_router[0], B, L)
        lg_t = lg[:, :E].T.reshape(E, n // LANES, LANES)
        pos, posm, wts = _select(lg_t, cap)
        off, base_x, base_y, cnt, pos_rows, grow_t, w_t = _routing_tables(pos, posm, wts, cap, slot0, cap_total, pad)
        tables.append((off, base_x, cnt, pos_rows))
        staged.append((x1, base_y, cnt, grow_t, w_t, (B, L)))
        slot0 += cap

    off, base_x, cnt, pos_rows = (jnp.concatenate(parts, axis=-1) for parts in zip(*tables))
    xe = _dispatch(off, base_x, cnt, staged[0][0], staged[1][0], pos_rows, cap_total, pad)

    ye = _ffn(xe.reshape(E, cap_total + pad, d), w_gate[0].astype(BF16), w_up[0].astype(BF16),
              w_down[0].astype(BF16), cap_total)
    ye = ye.reshape(E * cap_total, d)

    g2 = ln2_g[0].reshape(1, d)
    b2 = ln2_b[0].reshape(1, d)
    outs = []
    for x1, base_y, cnt, grow_t, w_t, (B, L) in staged:
        y = _combine(base_y, cnt, x1, grow_t, w_t, g2, b2, ye, E * cap_total)
        outs.append(y.reshape(B, L, d))
    return tuple(outs)
```

```python
import functools

import jax
import jax.numpy as jnp
from jax import lax
from jax.experimental import pallas as pl
from jax.experimental.pallas import tpu as pltpu

F32 = jnp.float32
BF16 = jnp.bfloat16
I32 = jnp.int32

D_MODEL = 2048
ML_WIDTH = 1024
ML_HEADS = 4
ML_DH = 256
HG_WIDTH = 1024
HG_HEADS = 8
HG_D = 128
N_EXPERTS = 16
EC_FACTOR = 2
EXPERT_FF = 2048
DEPTH = 1
ALPHA = (2.0 * DEPTH) ** 0.25
EPS = 1e-5

LANES = 128
SUBLANES = 8
MAIN_COLS = 9 * 1024
VMEM_LIMIT = 56 * 1024 * 1024

ML_CHUNK = 256
HG_CHUNK = 128
HG_STREAM_CHUNKS = 2


def _cparams(sem):
    return pltpu.CompilerParams(dimension_semantics=sem, vmem_limit_bytes=VMEM_LIMIT)


def _split_bf16(x):
    hi = x.astype(BF16)
    lo = (x - hi.astype(F32)).astype(BF16)
    return hi, lo


def _layer_norm(x, g, b):
    mu = jnp.mean(x, axis=-1, keepdims=True)
    xc = x - mu
    var = jnp.mean(xc * xc, axis=-1, keepdims=True)
    return xc * lax.rsqrt(var + EPS) * g + b


def _sigmoid(x):
    return 1.0 / (1.0 + jnp.exp(-x))


def _silu(x):
    return x * _sigmoid(x)


def _log_sigmoid(x):
    return jnp.minimum(x, 0.0) - jnp.log1p(jnp.exp(-jnp.abs(x)))


def _dot(a, b):
    return jnp.dot(a, b, preferred_element_type=F32)


def _dot_nt(a, b):
    return lax.dot_general(a, b, (((1,), (1,)), ((), ())), preferred_element_type=F32)


def _dot_tn(a, b):
    return lax.dot_general(a, b, (((0,), (0,)), ((), ())), preferred_element_type=F32)


def _ln_proj_kernel(x_ref, g_ref, b_ref, w_ref, wgh_ref, wgl_ref, proj_ref, gate_ref, hh_ref, hl_ref):
    @pl.when(pl.program_id(1) == 0)
    def _():
        h = _layer_norm(x_ref[...], g_ref[...], b_ref[...])
        hh, hl = _split_bf16(h)
        hh_ref[...] = hh
        hl_ref[...] = hl
        gate_ref[...] = _dot(hh, wgh_ref[...]) + _dot(hl, wgh_ref[...]) + _dot(hh, wgl_ref[...])

    proj_ref[...] = _dot(hh_ref[...], w_ref[...]).astype(proj_ref.dtype)


def _ln_proj(x, g, b, w_main, wg_hi, wg_lo, tm=1024, tn=1024):
    n, d = x.shape
    tm = min(tm, n)
    ncol = w_main.shape[1]
    return pl.pallas_call(
        _ln_proj_kernel,
        grid=(n // tm, ncol // tn),
        in_specs=[
            pl.BlockSpec((tm, d), lambda i, j: (i, 0)),
            pl.BlockSpec((1, d), lambda i, j: (0, 0)),
            pl.BlockSpec((1, d), lambda i, j: (0, 0)),
            pl.BlockSpec((d, tn), lambda i, j: (0, j)),
            pl.BlockSpec((d, LANES), lambda i, j: (0, 0)),
            pl.BlockSpec((d, LANES), lambda i, j: (0, 0)),
        ],
        out_specs=[
            pl.BlockSpec((tm, tn), lambda i, j: (i, j)),
            pl.BlockSpec((tm, LANES), lambda i, j: (i, 0)),
        ],
        out_shape=[
            jax.ShapeDtypeStruct((n, ncol), F32),
            jax.ShapeDtypeStruct((n, LANES), F32),
        ],
        scratch_shapes=[pltpu.VMEM((tm, d), BF16), pltpu.VMEM((tm, d), BF16)],
        compiler_params=_cparams(("arbitrary", "arbitrary")),
        name="ln_proj",
    )(x, g, b, w_main, wg_hi, wg_lo)


def _mlstm_chunk(qb, kb, vb, f_col, f_row, i_col, i_row, c_state, n_state, m_state, rev):
    c = qb.shape[0]
    tt = lax.broadcasted_iota(I32, (c, c), 0)
    ss = lax.broadcasted_iota(I32, (c, c), 1)
    causal = (ss >= tt) if rev else (ss <= tt)
    b_col = jnp.sum(jnp.where(causal, f_row, 0.0), axis=1, keepdims=True)
    anti = (tt >= ss) if rev else (tt <= ss)
    b_row = jnp.sum(jnp.where(anti, f_col, 0.0), axis=0, keepdims=True)
    total = jnp.sum(f_row, axis=1, keepdims=True)

    dmat = jnp.where(causal, b_col - b_row + i_row, -jnp.inf)
    a_inter = b_col + m_state
    m_t = jnp.maximum(a_inter, jnp.max(dmat, axis=1, keepdims=True))
    w_inter = jnp.exp(a_inter - m_t)
    s = _dot_nt(qb, kb) * jnp.exp(dmat - m_t)
    num = w_inter * _dot(qb, c_state.astype(BF16)) + _dot(s.astype(BF16), vb)
    qn = jnp.sum(qb.astype(F32) * n_state, axis=1, keepdims=True)
    den = w_inter * qn + jnp.sum(s, axis=1, keepdims=True)
    h = num / jnp.maximum(jnp.abs(den), jnp.exp(-m_t))

    g_col = total - b_col + i_col
    g_row = total - b_row + i_row
    m_new = jnp.maximum(total + m_state, jnp.max(g_row, axis=1, keepdims=True))
    decay = jnp.exp(total + m_state - m_new)
    ks = kb.astype(F32) * jnp.exp(g_col - m_new)
    c_new = decay * c_state + _dot_tn(ks.astype(BF16), vb)
    n_new = decay * n_state + jnp.sum(ks, axis=0, keepdims=True)
    return h, c_new, n_new, m_new


def _mlstm_kernel(bias_ref, q_ref, k_ref, v_ref, o_ref, cwq_ref, cwk_ref, gc_ref, gr_ref, nw_ref,
                  out_ref, qc_ref, kc_ref, vc_ref, hf_ref, hb_ref, cs_ref):
    L, d = q_ref.shape
    c = ML_CHUNK
    nc = L // c
    hd = pl.program_id(1)

    row = lax.broadcasted_iota(I32, (L, 1), 0)

    def conv_silu(x, w):
        xp = jnp.where(row == 0, 0.0, pltpu.roll(x, 1, 0))
        xn = jnp.where(row == L - 1, 0.0, pltpu.roll(x, L - 1, 0))
        return _silu(w[0:1, :] * xp + w[1:2, :] * x + w[2:3, :] * xn)

    qc_ref[...] = conv_silu(q_ref[...], cwq_ref[...]).astype(BF16)
    kc_ref[...] = (conv_silu(k_ref[...], cwk_ref[...]) * (ML_DH ** -0.5)).astype(BF16)
    vc_ref[...] = v_ref[...].astype(BF16)

    bi_f = bias_ref[hd]
    bi_b = bias_ref[ML_HEADS + hd]
    bf_f = bias_ref[2 * ML_HEADS + hd]
    bf_b = bias_ref[3 * ML_HEADS + hd]

    cs_ref[...] = jnp.zeros_like(cs_ref)

    def gates(t0):
        gc = gc_ref[0, 0, pl.ds(t0, c), :]
        gr = gr_ref[0, 0, :, pl.ds(t0, c)]
        return gc, gr

    def body(j, carry):
        n_f, m_f, n_b, m_b = carry
        t0 = pl.multiple_of(j * c, c)
        gc, gr = gates(t0)
        h, c_new, n_f, m_f = _mlstm_chunk(
            qc_ref[pl.ds(t0, c), :], kc_ref[pl.ds(t0, c), :], vc_ref[pl.ds(t0, c), :],
            _log_sigmoid(gc[:, 2:3] + bf_f), _log_sigmoid(gr[2:3, :] + bf_f),
            gc[:, 0:1] + bi_f, gr[0:1, :] + bi_f,
            cs_ref[0], n_f, m_f, rev=False)
        cs_ref[0] = c_new
        hf_ref[pl.ds(t0, c), :] = h
        t1 = pl.multiple_of((nc - 1 - j) * c, c)
        gc, gr = gates(t1)
        h, c_new, n_b, m_b = _mlstm_chunk(
            qc_ref[pl.ds(t1, c), :], kc_ref[pl.ds(t1, c), :], vc_ref[pl.ds(t1, c), :],
            _log_sigmoid(gc[:, 3:4] + bf_b), _log_sigmoid(gr[3:4, :] + bf_b),
            gc[:, 1:2] + bi_b, gr[1:2, :] + bi_b,
            cs_ref[1], n_b, m_b, rev=True)
        cs_ref[1] = c_new
        hb_ref[pl.ds(t1, c), :] = h
        return n_f, m_f, n_b, m_b

    zn = jnp.zeros((1, d), F32)
    m0 = jnp.full((1, 1), -1e30, F32)
    lax.fori_loop(0, nc, body, (zn, m0, zn, m0))

    hm = hf_ref[...] + hb_ref[...]
    mu = jnp.mean(hm, axis=1, keepdims=True)
    hc = hm - mu
    var = jnp.mean(hc * hc, axis=1, keepdims=True)
    y = hc * lax.rsqrt(var + EPS) * nw_ref[...] * _sigmoid(o_ref[...])
    out_ref[...] = y.astype(out_ref.dtype)


def _mlstm(proj, conv_w, gate_col, gate_row, bias, norm_w, B, L):
    d = ML_DH
    H = ML_HEADS
    grid_spec = pltpu.PrefetchScalarGridSpec(
        num_scalar_prefetch=1,
        grid=(B, H),
        in_specs=[
            pl.BlockSpec((L, d), lambda b, h, s: (b, h)),
            pl.BlockSpec((L, d), lambda b, h, s: (b, H + h)),
            pl.BlockSpec((L, d), lambda b, h, s: (b, 2 * H + h)),
            pl.BlockSpec((L, d), lambda b, h, s: (b, 3 * H + h)),
            pl.BlockSpec((3, d), lambda b, h, s: (0, h)),
            pl.BlockSpec((3, d), lambda b, h, s: (0, H + h)),
            pl.BlockSpec((1, 1, L, 4), lambda b, h, s: (b, h, 0, 0)),
            pl.BlockSpec((1, 1, 4, L), lambda b, h, s: (b, h, 0, 0)),
            pl.BlockSpec((1, d), lambda b, h, s: (0, h)),
        ],
        out_specs=pl.BlockSpec((L, d), lambda b, h, s: (b, h)),
        scratch_shapes=[
            pltpu.VMEM((L, d), BF16), pltpu.VMEM((L, d), BF16), pltpu.VMEM((L, d), BF16),
            pltpu.VMEM((L, d), F32), pltpu.VMEM((L, d), F32),
            pltpu.VMEM((2, d, d), F32),
        ],
    )
    return pl.pallas_call(
        _mlstm_kernel,
        grid_spec=grid_spec,
        out_shape=jax.ShapeDtypeStruct((B * L, ML_WIDTH), BF16),
        compiler_params=_cparams(("arbitrary", "arbitrary")),
        name="mlstm",
    )(bias, proj, proj, proj, proj, conv_w, conv_w, gate_col, gate_row, norm_w)


def _chunk_cumsum(x, rev):
    c, n = x.shape
    r = lax.broadcasted_iota(I32, (c, c), 0)
    s = lax.broadcasted_iota(I32, (c, c), 1)
    tri = jnp.where((s >= r) if rev else (s <= r), 1.0, 0.0).astype(BF16)
    hi = x.astype(BF16)
    r1 = x - hi.astype(F32)
    mid = r1.astype(BF16)
    lo = (r1 - mid.astype(F32)).astype(BF16)
    parts = _dot(tri, jnp.concatenate([hi, mid, lo], axis=1))
    return parts[:, :n] + parts[:, n:2 * n] + parts[:, 2 * n:]


class _HgStream:
    pass


def _hgrn2_chunks(streams):
    c, dk = streams[0].q.shape
    row = lax.broadcasted_iota(I32, (c, dk), 0)
    tt = lax.broadcasted_iota(I32, (c, c), 0)
    ss = lax.broadcasted_iota(I32, (c, c), 1)

    for s in streams:
        s.b = _chunk_cumsum(s.logf, s.rev)
        s.vb = s.v.astype(BF16)
        s.o = jnp.sum(s.q * s.k, axis=1, keepdims=True) * s.v

    for s in streams:
        q16 = s.q.astype(BF16)
        k16 = s.k.astype(BF16)
        f_prev = pltpu.roll(s.f, 1, 0)
        f_next = pltpu.roll(s.f, c - 1, 0)
        amat = None
        m = 1
        while m < c:
            blk = 2 * m
            u = row & (blk - 1)
            tgt = (u < m) if s.rev else (u >= m)
            if m == 1:
                e = jnp.where(tgt, s.f, 1.0)
            elif m == 2:
                if s.rev:
                    e = jnp.where(u == 0, s.f * f_next, jnp.where(u == 1, s.f, jnp.where(u == 2, 1.0, f_prev)))
                else:
                    e = jnp.where(u == 0, f_next, jnp.where(u == 1, 1.0, jnp.where(u == 2, s.f, s.f * f_prev)))
            else:
                bb = s.b.reshape(c // blk, blk, dk)
                ref = bb[:, m:m + 1, :] if s.rev else bb[:, m - 1:m, :]
                e = jnp.exp(-jnp.abs(bb - ref)).reshape(c, dk)
            qe = q16 * jnp.where(tgt, e, 0.0).astype(BF16)
            ke = k16 * jnp.where(tgt, 0.0, e).astype(BF16)
            a = _dot_nt(qe, ke)
            if blk < c:
                a = jnp.where((tt // blk) == (ss // blk), a, 0.0)
            amat = a if amat is None else amat + a
            m = blk
        s.amat = amat.astype(BF16)
        s.edge = s.b[0:1, :] if s.rev else s.b[c - 1:c, :]
        s.qi = (s.q * jnp.exp(s.b)).astype(BF16)
        s.kl = (s.k * jnp.exp(s.edge - s.b)).astype(BF16)

    for s in streams:
        s.o = s.o + _dot(s.amat, s.vb)

    for s in streams:
        st = s.get_state()
        s.o = s.o + _dot_nt(s.qi, st.astype(BF16))
        s.put_state(jnp.exp(s.edge) * st + _dot_tn(s.vb, s.kl))


def _hgrn2_kernel(q_ref, ff_ref, fb_ref, v_ref, g_ref, lb_ref, nw_ref, out_ref, of_ref, ob_ref, st_ref, qs_ref):
    L, dk = q_ref.shape
    c = HG_CHUNK
    nc = L // c

    lbp = lb_ref[...]
    mx = jnp.max(lbp, axis=0, keepdims=True)
    ex = jnp.exp(lbp - mx)
    lb = ex[0:1, :] / jnp.sum(ex, axis=0, keepdims=True)

    st_ref[...] = jnp.zeros_like(st_ref)
    qs_ref[...] = _silu(q_ref[...])

    def stream(t0, f_ref, slot, rev):
        s = _HgStream()
        sig = _sigmoid(f_ref[pl.ds(t0, c), :])
        s.f = lb + (1.0 - lb) * sig
        s.k = (1.0 - lb) * (1.0 - sig)
        s.logf = jnp.log(s.f)
        s.q = qs_ref[pl.ds(t0, c), :]
        s.v = v_ref[pl.ds(t0, c), :]
        s.rev = rev
        s.t0 = t0
        s.get_state = lambda: st_ref[slot]

        def put_state(x):
            st_ref[slot] = x

        s.put_state = put_state
        return s

    per_step = HG_STREAM_CHUNKS

    def body(j, carry):
        fwd = [stream(pl.multiple_of((j * per_step + i) * c, c), ff_ref, 0, False) for i in range(per_step)]
        bwd = [stream(pl.multiple_of((nc - 1 - j * per_step - i) * c, c), fb_ref, 1, True) for i in range(per_step)]
        _hgrn2_chunks(fwd + bwd)
        for s in fwd:
            of_ref[pl.ds(s.t0, c), :] = s.o
        for s in bwd:
            ob_ref[pl.ds(s.t0, c), :] = s.o
        return carry

    lax.fori_loop(0, nc // per_step, body, 0)

    o = of_ref[...] + ob_ref[...]
    o = o * lax.rsqrt(jnp.mean(o * o, axis=1, keepdims=True) + EPS)
    y = o * nw_ref[...] * _silu(g_ref[...])
    out_ref[...] = y.astype(out_ref.dtype)


def _hgrn2(proj, hg_lb, norm_w, B, L):
    dk = HG_D
    H = HG_HEADS
    base = 4 * ML_WIDTH // dk
    return pl.pallas_call(
        _hgrn2_kernel,
        grid=(B, H),
        in_specs=[
            pl.BlockSpec((L, dk), lambda b, h: (b, base + h)),
            pl.BlockSpec((L, dk), lambda b, h: (b, base + H + h)),
            pl.BlockSpec((L, dk), lambda b, h: (b, base + 2 * H + h)),
            pl.BlockSpec((L, dk), lambda b, h: (b, base + 3 * H + h)),
            pl.BlockSpec((L, dk), lambda b, h: (b, base + 4 * H + h)),
            pl.BlockSpec((2, dk), lambda b, h: (0, h)),
            pl.BlockSpec((1, dk), lambda b, h: (0, h)),
        ],
        out_specs=pl.BlockSpec((L, dk), lambda b, h: (b, h)),
        out_shape=jax.ShapeDtypeStruct((B * L, HG_WIDTH), BF16),
        scratch_shapes=[
            pltpu.VMEM((L, dk), F32), pltpu.VMEM((L, dk), F32),
            pltpu.VMEM((2, dk, dk), F32),
            pltpu.VMEM((L, dk), F32),
        ],
        compiler_params=_cparams(("arbitrary", "arbitrary")),
        name="hgrn2",
    )(proj, proj, proj, proj, proj, hg_lb, norm_w)


def _outproj_kernel(x_ref, ml_ref, hg_ref, eg_ref, eb_ref, wo_ref, g1_ref, b1_ref, wrh_ref, wrl_ref,
                    x1_ref, lg_ref):
    half = ml_ref.shape[1]
    mix = _dot(ml_ref[...], wo_ref[0:half, :]) + _dot(hg_ref[...], wo_ref[half:, :])
    h = _layer_norm(x_ref[...], eg_ref[...], eb_ref[...])
    x1 = _layer_norm(ALPHA * h + mix, g1_ref[...], b1_ref[...])
    x1_ref[...] = x1
    xh, xl = _split_bf16(x1)
    lg_ref[...] = _dot(xh, wrh_ref[...]) + _dot(xl, wrh_ref[...]) + _dot(xh, wrl_ref[...])


def _outproj(x, ml, hg, eg, eb, wo, g1, b1, wr_hi, wr_lo, tm=256):
    n, d = x.shape
    row = lambda i: (i, 0)
    fixed = lambda i: (0, 0)
    return pl.pallas_call(
        _outproj_kernel,
        grid=(n // tm,),
        in_specs=[
            pl.BlockSpec((tm, d), row),
            pl.BlockSpec((tm, ML_WIDTH), row),
            pl.BlockSpec((tm, HG_WIDTH), row),
            pl.BlockSpec((1, d), fixed),
            pl.BlockSpec((1, d), fixed),
            pl.BlockSpec((d, d), fixed),
            pl.BlockSpec((1, d), fixed),
            pl.BlockSpec((1, d), fixed),
            pl.BlockSpec((d, LANES), fixed),
            pl.BlockSpec((d, LANES), fixed),
        ],
        out_specs=[pl.BlockSpec((tm, d), row), pl.BlockSpec((tm, LANES), row)],
        out_shape=[jax.ShapeDtypeStruct((n, d), F32), jax.ShapeDtypeStruct((n, LANES), F32)],
        compiler_params=_cparams(("arbitrary",)),
        name="outproj",
    )(x, ml, hg, eg, eb, wo, g1, b1, wr_hi, wr_lo)


def _pad_cols(w, width=LANES):
    return jnp.pad(w, ((0, 0), (0, width - w.shape[1])))


def _token_mixer_stage(x, emb_ln_g, emb_ln_b, w_in, conv_w, ml_igate_b, ml_fgate_b, ml_norm_w,
                       hg_lb, hg_norm_w, w_out, ln1_g, ln1_b, w_router, B, L):
    d = D_MODEL
    g0 = 4 * ML_WIDTH
    w_main = jnp.concatenate([w_in[:, :g0], w_in[:, g0 + 16:]], axis=1).astype(BF16)
    wg_hi, wg_lo = _split_bf16(_pad_cols(w_in[:, g0:g0 + 16]))
    eg = emb_ln_g.reshape(1, d)
    eb = emb_ln_b.reshape(1, d)
    proj, gates = _ln_proj(x, eg, eb, w_main, wg_hi, wg_lo)

    g4 = gates[:, :16].reshape(B, L, 4, ML_HEADS)
    gate_col = g4.transpose(0, 3, 1, 2)
    gate_row = g4.transpose(0, 3, 2, 1)
    bias = jnp.concatenate([ml_igate_b[0], ml_igate_b[1], ml_fgate_b[0], ml_fgate_b[1]]).astype(F32)
    ml = _mlstm(proj, conv_w, gate_col, gate_row, bias, ml_norm_w.reshape(1, ML_WIDTH), B, L)
    hg = _hgrn2(proj, hg_lb, hg_norm_w.reshape(1, HG_WIDTH), B, L)

    wr_hi, wr_lo = _split_bf16(_pad_cols(w_router))
    return _outproj(x, ml, hg, eg, eb, w_out.astype(BF16), ln1_g.reshape(1, d), ln1_b.reshape(1, d),
                    wr_hi, wr_lo)


def _excl_token_cumsum(mask):
    E, R, ln = mask.shape
    mf = jnp.where(mask, 1.0, 0.0)
    mb = mf.astype(BF16)
    upper = jnp.where(lax.broadcasted_iota(I32, (ln, ln), 0) <= lax.broadcasted_iota(I32, (ln, ln), 1),
                      1.0, 0.0).astype(BF16)
    lower = jnp.where(lax.broadcasted_iota(I32, (R, R), 1) < lax.broadcasted_iota(I32, (R, R), 0),
                      1.0, 0.0).astype(BF16)
    ones = jnp.ones((ln, ln), BF16)
    within = _dot(mb.reshape(E * R, ln), upper).reshape(E, R, ln)
    rows = jnp.stack([_dot(_dot(lower, mb[e]).astype(BF16), ones) for e in range(E)], axis=0)
    return within - mf + rows


def _select_kernel(lg_ref, pos_ref, posm_ref, wts_ref, *, cap):
    E = lg_ref.shape[0]
    lg = lg_ref[...]
    mx = jnp.max(lg, axis=0, keepdims=True)
    ex = jnp.exp(lg - mx)
    aff = ex / jnp.sum(ex, axis=0, keepdims=True)

    def count(mask):
        ones = jnp.where(mask, 1.0, 0.0)
        return jnp.sum(jnp.sum(ones, axis=2, keepdims=True), axis=1, keepdims=True)

    def body(i, tbits):
        cand = tbits | lax.shift_left(jnp.int32(1), 30 - i)
        cnt = count(aff >= lax.bitcast_convert_type(cand, F32))
        return jnp.where(cnt >= cap, cand, tbits)

    tbits = lax.fori_loop(0, 31, body, jnp.zeros((E, 1, 1), I32))
    thr = lax.bitcast_convert_type(tbits, F32)
    nxt = lax.bitcast_convert_type(tbits + 1, F32)
    above = aff >= nxt
    band = jnp.logical_and(aff >= thr, jnp.logical_not(above))
    need = cap - count(above)
    sel = jnp.logical_or(above, jnp.logical_and(band, _excl_token_cumsum(band) < need))
    pos = _excl_token_cumsum(sel).astype(I32)
    pos_ref[...] = pos
    posm_ref[...] = jnp.where(sel, pos, -1)
    wts_ref[...] = jnp.where(sel, aff, 0.0)


def _select(lg_t, cap):
    E, R, ln = lg_t.shape
    full = pl.BlockSpec((E, R, ln), lambda i: (0, 0, 0))
    return pl.pallas_call(
        functools.partial(_select_kernel, cap=cap),
        grid=(1,),
        in_specs=[full],
        out_specs=[full, full, full],
        out_shape=[jax.ShapeDtypeStruct((E, R, ln), I32), jax.ShapeDtypeStruct((E, R, ln), I32),
                   jax.ShapeDtypeStruct((E, R, ln), F32)],
        compiler_params=_cparams(("arbitrary",)),
        name="select",
    )(lg_t)


TOK_TILE = 256
SLOT_CHUNK = 64


def _dispatch_kernel(off_ref, base_ref, cnt_ref, xa_ref, xb_ref, pos_ref, xe_ref,
                     x16_ref, stage_ref, ostage_ref, carry_ref, sem, osem, *, tiles_a, cap_total, pad):
    E = N_EXPERTS
    CH = SLOT_CHUNK
    SUB = SUBLANES
    tc = xa_ref.shape[0]
    t = pl.program_id(0)
    nt = pl.num_programs(0)
    slot = lax.rem(t, 2)

    @pl.when(t < tiles_a)
    def _():
        x16_ref[...] = xa_ref[...].astype(BF16)

    @pl.when(t >= tiles_a)
    def _():
        x16_ref[...] = xb_ref[...].astype(BF16)

    xb = x16_ref[...]

    @pl.when(t == 0)
    def _():
        carry_ref[...] = jnp.zeros_like(carry_ref)

    base = [base_ref[t * E + e] for e in range(E)]
    cnt = [cnt_ref[t * E + e] for e in range(E)]
    al = [pl.multiple_of((b // SUB) * SUB, SUB) for b in base]
    first = [off_ref[t * E + e] - (base[e] - al[e]) for e in range(E)]

    def onehot(e, start, rows):
        kio = lax.broadcasted_iota(I32, (rows, tc), 0)
        rel = pos_ref[e:e + 1, :] - (first[e] + start)
        return jnp.where(rel == kio, 1.0, 0.0).astype(BF16)

    ot = jnp.concatenate([onehot(e, 0, CH) for e in range(E)], axis=0)
    stage_ref[slot] = _dot(ot, xb)
    for e in range(E):
        stage_ref[slot, e * CH:e * CH + SUB, :] += carry_ref[e * SUB:(e + 1) * SUB, :]

    nxt = [((base[e] + cnt[e]) // SUB) * SUB - al[e] for e in range(E)]
    oc = jnp.concatenate([onehot(e, nxt[e], SUB) for e in range(E)], axis=0)
    new_carry = _dot(oc, xb)
    for e in range(E):
        keep = jnp.where(nxt[e] == 0, 1.0, 0.0)
        carry_ref[e * SUB:(e + 1) * SUB, :] = new_carry[e * SUB:(e + 1) * SUB, :] + keep * carry_ref[e * SUB:(e + 1) * SUB, :]

    def main_copy(s, e, row):
        return pltpu.make_async_copy(stage_ref.at[s, pl.ds(e * CH, CH)], xe_ref.at[pl.ds(row, CH)], sem.at[s])

    @pl.when(t > 0)
    def _():
        for e in range(E):
            main_copy(1 - slot, e, 0).wait()

    for e in range(E):
        main_copy(slot, e, al[e]).start()

    for e in range(E):
        nch = (base[e] - al[e] + cnt[e] + CH - 1) // CH

        def body(c, carry, e=e):
            ostage_ref[...] = _dot(onehot(e, c * CH, CH), xb)
            row = pl.multiple_of(al[e] + c * CH, SUB)
            cp = pltpu.make_async_copy(ostage_ref, xe_ref.at[pl.ds(row, CH)], osem)
            cp.start()
            cp.wait()
            return carry

        lax.fori_loop(1, nch, body, 0)

    @pl.when(t == nt - 1)
    def _():
        for e in range(E):
            main_copy(slot, e, 0).wait()
        ostage_ref[...] = jnp.zeros_like(ostage_ref)
        fills = [pltpu.make_async_copy(ostage_ref, xe_ref.at[pl.ds(e * (cap_total + pad) + cap_total + j * CH, CH)], osem)
                 for e in range(E) for j in range(pad // CH)]
        for cp in fills:
            cp.start()
        for cp in fills:
            cp.wait()


def _dispatch(off, base, cnt, x1_a, x1_b, pos_rows, cap_total, pad):
    d = x1_a.shape[1]
    E = N_EXPERTS
    tc = TOK_TILE
    tiles_a = x1_a.shape[0] // tc
    tiles_b = x1_b.shape[0] // tc
    grid_spec = pltpu.PrefetchScalarGridSpec(
        num_scalar_prefetch=3,
        grid=(tiles_a + tiles_b,),
        in_specs=[
            pl.BlockSpec((tc, d), lambda i, *_: (jnp.minimum(i, tiles_a - 1), 0)),
            pl.BlockSpec((tc, d), lambda i, *_: (jnp.maximum(i - tiles_a, 0), 0)),
            pl.BlockSpec((E, tc), lambda i, *_: (0, i)),
        ],
        out_specs=pl.BlockSpec(memory_space=pl.ANY),
        scratch_shapes=[
            pltpu.VMEM((tc, d), BF16),
            pltpu.VMEM((2, E * SLOT_CHUNK, d), F32),
            pltpu.VMEM((SLOT_CHUNK, d), F32),
            pltpu.VMEM((E * SUBLANES, d), F32),
            pltpu.SemaphoreType.DMA((2,)),
            pltpu.SemaphoreType.DMA(()),
        ],
    )
    return pl.pallas_call(
        functools.partial(_dispatch_kernel, tiles_a=tiles_a, cap_total=cap_total, pad=pad),
        grid_spec=grid_spec,
        out_shape=jax.ShapeDtypeStruct((E * (cap_total + pad), d), F32),
        compiler_params=_cparams(("arbitrary",)),
        name="dispatch",
    )(off, base, cnt, x1_a, x1_b, pos_rows)


def _ffn_kernel(xe_ref, wg_ref, wu_ref, wd_ref, out_ref, xb_ref, acc_ref):
    f = pl.program_id(2)

    @pl.when(f == 0)
    def _():
        xb_ref[...] = xe_ref[0].astype(BF16)

    xb = xb_ref[...]
    hid = (_silu(_dot(xb, wg_ref[0])) * _dot(xb, wu_ref[0])).astype(BF16)
    part = _dot(hid, wd_ref[0])

    @pl.when(f == 0)
    def _():
        acc_ref[...] = part

    @pl.when(f > 0)
    def _():
        acc_ref[...] += part

    @pl.when(f == pl.num_programs(2) - 1)
    def _():
        out_ref[0] = acc_ref[...].astype(out_ref.dtype)


def _ffn(xe, wg, wu, wd, cap_total, tf=512):
    E, _, d = xe.shape
    ff = wg.shape[2]
    tm = next(t for t in (1024, 512, 256, 128) if cap_total % t == 0)
    return pl.pallas_call(
        _ffn_kernel,
        grid=(E, cap_total // tm, ff // tf),
        in_specs=[
            pl.BlockSpec((1, tm, d), lambda e, i, f: (e, i, 0)),
            pl.BlockSpec((1, d, tf), lambda e, i, f: (e, 0, f)),
            pl.BlockSpec((1, d, tf), lambda e, i, f: (e, 0, f)),
            pl.BlockSpec((1, tf, d), lambda e, i, f: (e, f, 0)),
        ],
        out_specs=pl.BlockSpec((1, tm, d), lambda e, i, f: (e, i, 0)),
        out_shape=jax.ShapeDtypeStruct((E, cap_total, d), BF16),
        scratch_shapes=[pltpu.VMEM((tm, d), BF16), pltpu.VMEM((tm, d), F32)],
        compiler_params=_cparams(("arbitrary", "arbitrary", "arbitrary")),
        name="ffn",
    )(xe, wg, wu, wd)


def _combine_kernel(base_ref, cnt_ref, x1_ref, grow_ref, w_ref, g2_ref, b2_ref, ye_ref, out_ref,
                    buf_ref, obuf_ref, acc_ref, sem, osem, *, rows_total):
    E = N_EXPERTS
    CH = SLOT_CHUNK
    PACK = 4
    tc = x1_ref.shape[0]
    t = pl.program_id(0)
    nt = pl.num_programs(0)
    slot = lax.rem(t, 2)
    last = rows_total - CH
    ALIGN = 2 * SUBLANES

    def window(step, e):
        al = (base_ref[step * E + e] // ALIGN) * ALIGN
        return al, pl.multiple_of(jnp.minimum(al, last), ALIGN)

    def chunk_copy(s, e, start):
        return pltpu.make_async_copy(ye_ref.at[pl.ds(start, CH)], buf_ref.at[s, pl.ds(e * CH, CH)], sem.at[s, e])

    @pl.when(t == 0)
    def _():
        for e in range(E):
            chunk_copy(0, e, window(0, e)[1]).start()

    @pl.when(t + 1 < nt)
    def _():
        for e in range(E):
            chunk_copy(1 - slot, e, window(t + 1, e)[1]).start()

    kio = lax.broadcasted_iota(I32, (tc, CH), 1)
    acc_ref[...] = ALPHA * x1_ref[...]

    def weights(e, start, lo):
        grow = grow_ref[:, e:e + 1]
        hit = jnp.logical_and(grow - start == kio, grow >= lo)
        return jnp.where(hit, w_ref[:, e:e + 1], 0.0).astype(BF16)

    for g in range(E // PACK):
        es = range(g * PACK, (g + 1) * PACK)
        for e in es:
            chunk_copy(slot, e, 0).wait()
        a = jnp.concatenate([weights(e, window(t, e)[1], 0) for e in es], axis=1)
        acc_ref[...] += _dot(a, buf_ref[slot, g * PACK * CH:(g + 1) * PACK * CH, :])

    for e in range(E):
        al = window(t, e)[0]
        nch = (base_ref[t * E + e] - al + cnt_ref[t * E + e] + CH - 1) // CH

        def body(c, carry, e=e, al=al):
            lo = al + c * CH
            start = pl.multiple_of(jnp.minimum(lo, last), ALIGN)
            cp = pltpu.make_async_copy(ye_ref.at[pl.ds(start, CH)], obuf_ref, osem)
            cp.start()
            cp.wait()
            acc_ref[...] += _dot(weights(e, start, lo), obuf_ref[...])
            return carry

        lax.fori_loop(1, nch, body, 0)

    out_ref[...] = _layer_norm(acc_ref[...], g2_ref[...], b2_ref[...])


def _combine(base, cnt, x1, grow_t, w_t, g2, b2, ye, rows_total):
    n, d = x1.shape
    E = N_EXPERTS
    tc = TOK_TILE
    grid_spec = pltpu.PrefetchScalarGridSpec(
        num_scalar_prefetch=2,
        grid=(n // tc,),
        in_specs=[
            pl.BlockSpec((tc, d), lambda i, *_: (i, 0)),
            pl.BlockSpec((tc, E), lambda i, *_: (i, 0)),
            pl.BlockSpec((tc, E), lambda i, *_: (i, 0)),
            pl.BlockSpec((1, d), lambda i, *_: (0, 0)),
            pl.BlockSpec((1, d), lambda i, *_: (0, 0)),
            pl.BlockSpec(memory_space=pl.ANY),
        ],
        out_specs=pl.BlockSpec((tc, d), lambda i, *_: (i, 0)),
        scratch_shapes=[
            pltpu.VMEM((2, E * SLOT_CHUNK, d), BF16),
            pltpu.VMEM((SLOT_CHUNK, d), BF16),
            pltpu.VMEM((tc, d), F32),
            pltpu.SemaphoreType.DMA((2, E)),
            pltpu.SemaphoreType.DMA(()),
        ],
    )
    return pl.pallas_call(
        functools.partial(_combine_kernel, rows_total=rows_total),
        grid_spec=grid_spec,
        out_shape=jax.ShapeDtypeStruct((n, d), F32),
        compiler_params=_cparams(("arbitrary",)),
        name="combine",
    )(base, cnt, x1, grow_t, w_t, g2, b2, ye)


def _routing_tables(pos, posm, wts, cap, slot0, cap_total, pad):
    E = N_EXPERTS
    n = pos.shape[1] * pos.shape[2]
    tiles = n // TOK_TILE
    eidx = jnp.arange(E, dtype=I32)
    off = pos.reshape(E, n)[:, ::TOK_TILE].T
    cnt = jnp.concatenate([off[1:], jnp.full((1, E), cap, I32)], axis=0) - off
    base_x = off + eidx[None, :] * (cap_total + pad) + slot0
    base_y = off + eidx[None, :] * cap_total + slot0
    pos_rows = posm.reshape(E, n)
    grow_t = jnp.where(pos_rows >= 0, pos_rows + eidx[:, None] * cap_total + slot0, -1).T
    w_t = wts.reshape(E, n).T
    flat = lambda a: a.reshape(tiles * E)
    return flat(off), flat(base_x), flat(base_y), flat(cnt), pos_rows, grow_t, w_t


def kernel(x_prompt, x_sample, emb_ln_g, emb_ln_b, w_in, conv_w, ml_igate_b, ml_fgate_b, ml_norm_w, hg_lb, hg_norm_w, w_out, ln1_g, ln1_b, w_router, w_gate, w_up, w_down, ln2_g, ln2_b):
    E = N_EXPERTS
    d = D_MODEL
    groups = (x_prompt, x_sample)
    caps = [EC_FACTOR * x.shape[0] * x.shape[1] // E for x in groups]
    cap_total = sum(caps)
    pad = 2 * SLOT_CHUNK
    assert all(c % (2 * SUBLANES) == 0 for c in caps)

    staged = []
    tables = []
    slot0 = 0
    for x, cap in zip(groups, caps):
        B, L, _ = x.shape
        n = B * L
        x1, lg = _token_mixer_stage(x.reshape(n, d), emb_ln_g, emb_ln_b, w_in[0], conv_w[0], ml_igate_b[0],
                                    ml_fgate_b[0], ml_norm_w[0], hg_lb, hg_norm_w[0], w_out[0], ln1_g[0],
                                    ln1_b[0], w_router[0], B, L)
        lg_t = lg[:, :E].T.reshape(E, n // LANES, LANES)
        pos, posm, wts = _select(lg_t, cap)
        off, base_x, base_y, cnt, pos_rows, grow_t, w_t = _routing_tables(pos, posm, wts, cap, slot0, cap_total, pad)
        tables.append((off, base_x, cnt, pos_rows))
        staged.append((x1, base_y, cnt, grow_t, w_t, (B, L)))
        slot0 += cap

    off, base_x, cnt, pos_rows = (jnp.concatenate(parts, axis=-1) for parts in zip(*tables))
    xe = _dispatch(off, base_x, cnt, staged[0][0], staged[1][0], pos_rows, cap_total, pad)

    ye = _ffn(xe.reshape(E, cap_total + pad, d), w_gate[0].astype(BF16), w_up[0].astype(BF16),
              w_down[0].astype(BF16), cap_total)
    ye = ye.reshape(E * cap_total, d)

    g2 = ln2_g[0].reshape(1, d)
    b2 = ln2_b[0].reshape(1, d)
    outs = []
    for x1, base_y, cnt, grow_t, w_t, (B, L) in staged:
        y = _combine(base_y, cnt, x1, grow_t, w_t, g2, b2, ye, E * cap_total)
        outs.append(y.reshape(B, L, d))
    return tuple(outs)
```

```python
import functools

import jax
import jax.numpy as jnp
from jax import lax
from jax.experimental import pallas as pl
from jax.experimental.pallas import tpu as pltpu

F32 = jnp.float32
BF16 = jnp.bfloat16
I32 = jnp.int32
U32 = jnp.uint32

D_MODEL = 2048
ML_WIDTH = 1024
ML_HEADS = 4
ML_DH = 256
HG_WIDTH = 1024
HG_HEADS = 8
HG_D = 128
N_EXPERTS = 16
EC_FACTOR = 2
EXPERT_FF = 2048
DEPTH = 1
ALPHA = (2.0 * DEPTH) ** 0.25
EPS = 1e-5

LANES = 128
SUBLANES = 8
MAIN_COLS = 9 * 1024
VMEM_LIMIT = 56 * 1024 * 1024

ML_CHUNK = 256
HG_CHUNK = 128
HG_STREAM_CHUNKS = 2


def _cparams(sem):
    return pltpu.CompilerParams(dimension_semantics=sem, vmem_limit_bytes=VMEM_LIMIT)


def _split_bf16(x):
    hi = x.astype(BF16)
    lo = (x - hi.astype(F32)).astype(BF16)
    return hi, lo


def _layer_norm(x, g, b):
    mu = jnp.mean(x, axis=-1, keepdims=True)
    xc = x - mu
    var = jnp.mean(xc * xc, axis=-1, keepdims=True)
    return xc * lax.rsqrt(var + EPS) * g + b


def _sigmoid(x):
    return 1.0 / (1.0 + jnp.exp(-x))


def _silu(x):
    return x * _sigmoid(x)


def _log_sigmoid(x):
    return jnp.minimum(x, 0.0) - jnp.log1p(jnp.exp(-jnp.abs(x)))


def _dot(a, b):
    return jnp.dot(a, b, preferred_element_type=F32)


def _dot_nt(a, b):
    return lax.dot_general(a, b, (((1,), (1,)), ((), ())), preferred_element_type=F32)


def _dot_tn(a, b):
    return lax.dot_general(a, b, (((0,), (0,)), ((), ())), preferred_element_type=F32)


def _ln_proj_kernel(x_ref, g_ref, b_ref, w_ref, wgh_ref, wgl_ref, proj_ref, gate_ref, hh_ref, hl_ref):
    @pl.when(pl.program_id(1) == 0)
    def _():
        h = _layer_norm(x_ref[...], g_ref[...], b_ref[...])
        hh, hl = _split_bf16(h)
        hh_ref[...] = hh
        hl_ref[...] = hl
        gate_ref[...] = _dot(hh, wgh_ref[...]) + _dot(hl, wgh_ref[...]) + _dot(hh, wgl_ref[...])

    proj_ref[...] = _dot(hh_ref[...], w_ref[...]).astype(proj_ref.dtype)


def _ln_proj(x, g, b, w_main, wg_hi, wg_lo, tm=1024, tn=1024):
    n, d = x.shape
    tm = min(tm, n)
    ncol = w_main.shape[1]
    return pl.pallas_call(
        _ln_proj_kernel,
        grid=(n // tm, ncol // tn),
        in_specs=[
            pl.BlockSpec((tm, d), lambda i, j: (i, 0)),
            pl.BlockSpec((1, d), lambda i, j: (0, 0)),
            pl.BlockSpec((1, d), lambda i, j: (0, 0)),
            pl.BlockSpec((d, tn), lambda i, j: (0, j)),
            pl.BlockSpec((d, LANES), lambda i, j: (0, 0)),
            pl.BlockSpec((d, LANES), lambda i, j: (0, 0)),
        ],
        out_specs=[
            pl.BlockSpec((tm, tn), lambda i, j: (i, j)),
            pl.BlockSpec((tm, LANES), lambda i, j: (i, 0)),
        ],
        out_shape=[
            jax.ShapeDtypeStruct((n, ncol), F32),
            jax.ShapeDtypeStruct((n, LANES), F32),
        ],
        scratch_shapes=[pltpu.VMEM((tm, d), BF16), pltpu.VMEM((tm, d), BF16)],
        compiler_params=_cparams(("arbitrary", "arbitrary")),
        name="ln_proj",
    )(x, g, b, w_main, wg_hi, wg_lo)


class _MlStream:
    pass


def _mlstm_chunks(streams):
    c = streams[0].qb.shape[0]
    tt = lax.broadcasted_iota(I32, (c, c), 0)
    ss = lax.broadcasted_iota(I32, (c, c), 1)

    for s in streams:
        causal = (ss >= tt) if s.rev else (ss <= tt)
        b_col = jnp.sum(jnp.where(causal, s.f_row, 0.0), axis=1, keepdims=True)
        anti = (tt >= ss) if s.rev else (tt <= ss)
        b_row = jnp.sum(jnp.where(anti, s.f_col, 0.0), axis=0, keepdims=True)
        total = jnp.sum(s.f_row, axis=1, keepdims=True)

        dmat = jnp.where(causal, b_col - b_row + s.i_row, -jnp.inf)
        a_inter = b_col + s.m_state
        s.m_t = jnp.maximum(a_inter, jnp.max(dmat, axis=1, keepdims=True))
        s.w_inter = jnp.exp(a_inter - s.m_t)
        s.p = jnp.exp(dmat - s.m_t)
        s.qk = _dot_nt(s.qb, s.kb)
        s.qc = _dot(s.qb, s.c_state.astype(BF16))

        g_col = total - b_col + s.i_col
        g_row = total - b_row + s.i_row
        s.m_new = jnp.maximum(total + s.m_state, jnp.max(g_row, axis=1, keepdims=True))
        s.decay = jnp.exp(total + s.m_state - s.m_new)
        ks = s.kb.astype(F32) * jnp.exp(g_col - s.m_new)
        s.kv = _dot_tn(ks.astype(BF16), s.vb)
        s.n_new = s.decay * s.n_state + jnp.sum(ks, axis=0, keepdims=True)

    for s in streams:
        s.s = s.qk * s.p
        s.sv = _dot(s.s.astype(BF16), s.vb)

    for s in streams:
        num = s.w_inter * s.qc + s.sv
        qn = jnp.sum(s.qb.astype(F32) * s.n_state, axis=1, keepdims=True)
        den = s.w_inter * qn + jnp.sum(s.s, axis=1, keepdims=True)
        s.h = num / jnp.maximum(jnp.abs(den), jnp.exp(-s.m_t))
        s.c_new = s.decay * s.c_state + s.kv


def _mlstm_kernel(bias_ref, q_ref, k_ref, v_ref, o_ref, cwq_ref, cwk_ref, gc_ref, gr_ref, nw_ref,
                  out_ref, qc_ref, kc_ref, vc_ref, hf_ref, hb_ref, cs_ref):
    L, d = q_ref.shape
    c = ML_CHUNK
    nc = L // c
    hd = pl.program_id(1)

    row = lax.broadcasted_iota(I32, (L, 1), 0)

    def conv_silu(x, w):
        xp = jnp.where(row == 0, 0.0, pltpu.roll(x, 1, 0))
        xn = jnp.where(row == L - 1, 0.0, pltpu.roll(x, L - 1, 0))
        return _silu(w[0:1, :] * xp + w[1:2, :] * x + w[2:3, :] * xn)

    qc_ref[...] = conv_silu(q_ref[...], cwq_ref[...]).astype(BF16)
    kc_ref[...] = (conv_silu(k_ref[...], cwk_ref[...]) * (ML_DH ** -0.5)).astype(BF16)
    vc_ref[...] = v_ref[...].astype(BF16)

    bi_f = bias_ref[hd]
    bi_b = bias_ref[ML_HEADS + hd]
    bf_f = bias_ref[2 * ML_HEADS + hd]
    bf_b = bias_ref[3 * ML_HEADS + hd]

    cs_ref[...] = jnp.zeros_like(cs_ref)

    def gates(t0):
        gc = gc_ref[0, 0, pl.ds(t0, c), :]
        gr = gr_ref[0, 0, :, pl.ds(t0, c)]
        return gc, gr

    def stream(t0, slot, rev, bias_i, bias_f, n_state, m_state):
        s = _MlStream()
        gc, gr = gates(t0)
        gi, gf = (1, 3) if rev else (0, 2)
        s.qb = qc_ref[pl.ds(t0, c), :]
        s.kb = kc_ref[pl.ds(t0, c), :]
        s.vb = vc_ref[pl.ds(t0, c), :]
        s.f_col = _log_sigmoid(gc[:, gf:gf + 1] + bias_f)
        s.f_row = _log_sigmoid(gr[gf:gf + 1, :] + bias_f)
        s.i_col = gc[:, gi:gi + 1] + bias_i
        s.i_row = gr[gi:gi + 1, :] + bias_i
        s.c_state = cs_ref[slot]
        s.n_state = n_state
        s.m_state = m_state
        s.rev = rev
        return s

    def body(j, carry):
        n_f, m_f, n_b, m_b = carry
        t0 = pl.multiple_of(j * c, c)
        t1 = pl.multiple_of((nc - 1 - j) * c, c)
        fwd = stream(t0, 0, False, bi_f, bf_f, n_f, m_f)
        bwd = stream(t1, 1, True, bi_b, bf_b, n_b, m_b)
        _mlstm_chunks([fwd, bwd])
        cs_ref[0] = fwd.c_new
        hf_ref[pl.ds(t0, c), :] = fwd.h
        cs_ref[1] = bwd.c_new
        hb_ref[pl.ds(t1, c), :] = bwd.h
        return fwd.n_new, fwd.m_new, bwd.n_new, bwd.m_new

    zn = jnp.zeros((1, d), F32)
    m0 = jnp.full((1, 1), -1e30, F32)
    lax.fori_loop(0, nc, body, (zn, m0, zn, m0))

    hm = hf_ref[...] + hb_ref[...]
    mu = jnp.mean(hm, axis=1, keepdims=True)
    hc = hm - mu
    var = jnp.mean(hc * hc, axis=1, keepdims=True)
    y = hc * lax.rsqrt(var + EPS) * nw_ref[...] * _sigmoid(o_ref[...])
    out_ref[...] = y.astype(out_ref.dtype)


def _mlstm(proj, conv_w, gate_col, gate_row, bias, norm_w, B, L):
    d = ML_DH
    H = ML_HEADS
    grid_spec = pltpu.PrefetchScalarGridSpec(
        num_scalar_prefetch=1,
        grid=(B, H),
        in_specs=[
            pl.BlockSpec((L, d), lambda b, h, s: (b, h)),
            pl.BlockSpec((L, d), lambda b, h, s: (b, H + h)),
            pl.BlockSpec((L, d), lambda b, h, s: (b, 2 * H + h)),
            pl.BlockSpec((L, d), lambda b, h, s: (b, 3 * H + h)),
            pl.BlockSpec((3, d), lambda b, h, s: (0, h)),
            pl.BlockSpec((3, d), lambda b, h, s: (0, H + h)),
            pl.BlockSpec((1, 1, L, 4), lambda b, h, s: (b, h, 0, 0)),
            pl.BlockSpec((1, 1, 4, L), lambda b, h, s: (b, h, 0, 0)),
            pl.BlockSpec((1, d), lambda b, h, s: (0, h)),
        ],
        out_specs=pl.BlockSpec((L, d), lambda b, h, s: (b, h)),
        scratch_shapes=[
            pltpu.VMEM((L, d), BF16), pltpu.VMEM((L, d), BF16), pltpu.VMEM((L, d), BF16),
            pltpu.VMEM((L, d), F32), pltpu.VMEM((L, d), F32),
            pltpu.VMEM((2, d, d), F32),
        ],
    )
    return pl.pallas_call(
        _mlstm_kernel,
        grid_spec=grid_spec,
        out_shape=jax.ShapeDtypeStruct((B * L, ML_WIDTH), BF16),
        compiler_params=_cparams(("arbitrary", "arbitrary")),
        name="mlstm",
    )(bias, proj, proj, proj, proj, conv_w, conv_w, gate_col, gate_row, norm_w)


def _chunk_cumsum(x, rev):
    c, n = x.shape
    r = lax.broadcasted_iota(I32, (c, c), 0)
    s = lax.broadcasted_iota(I32, (c, c), 1)
    tri = jnp.where((s >= r) if rev else (s <= r), 1.0, 0.0).astype(BF16)
    hi = x.astype(BF16)
    r1 = x - hi.astype(F32)
    mid = r1.astype(BF16)
    lo = (r1 - mid.astype(F32)).astype(BF16)
    parts = _dot(tri, jnp.concatenate([hi, mid, lo], axis=1))
    return parts[:, :n] + parts[:, n:2 * n] + parts[:, 2 * n:]


class _HgStream:
    pass


def _hgrn2_chunks(streams):
    c, dk = streams[0].q.shape
    row = lax.broadcasted_iota(I32, (c, dk), 0)
    tt = lax.broadcasted_iota(I32, (c, c), 0)
    ss = lax.broadcasted_iota(I32, (c, c), 1)

    for s in streams:
        s.b = _chunk_cumsum(s.logf, s.rev)
        s.vb = s.v.astype(BF16)
        s.o = jnp.sum(s.q * s.k, axis=1, keepdims=True) * s.v

    for s in streams:
        q16 = s.q.astype(BF16)
        k16 = s.k.astype(BF16)
        f_prev = pltpu.roll(s.f, 1, 0)
        f_next = pltpu.roll(s.f, c - 1, 0)
        amat = None
        m = c // 2
        while m >= 1:
            blk = 2 * m
            u = row & (blk - 1)
            tgt = (u < m) if s.rev else (u >= m)
            if m == 1:
                e = jnp.where(tgt, s.f, 1.0)
            elif m == 2:
                if s.rev:
                    e = jnp.where(u == 0, s.f * f_next, jnp.where(u == 1, s.f, jnp.where(u == 2, 1.0, f_prev)))
                else:
                    e = jnp.where(u == 0, f_next, jnp.where(u == 1, 1.0, jnp.where(u == 2, s.f, s.f * f_prev)))
            else:
                bb = s.b.reshape(c // blk, blk, dk)
                ref = bb[:, m:m + 1, :] if s.rev else bb[:, m - 1:m, :]
                e = jnp.exp(-jnp.abs(bb - ref)).reshape(c, dk)
            qe = q16 * jnp.where(tgt, e, 0.0).astype(BF16)
            ke = k16 * jnp.where(tgt, 0.0, e).astype(BF16)
            a = _dot_nt(qe, ke).astype(BF16)
            amat = a if amat is None else jnp.where((tt // blk) == (ss // blk), a, amat)
            m = m // 2
        s.amat = amat
        s.edge = s.b[0:1, :] if s.rev else s.b[c - 1:c, :]
        s.qi = (s.q * jnp.exp(s.b)).astype(BF16)
        s.kl = (s.k * jnp.exp(s.edge - s.b)).astype(BF16)

    for s in streams:
        s.o = s.o + _dot(s.amat, s.vb)

    for s in streams:
        st = s.get_state()
        s.o = s.o + _dot_nt(s.qi, st.astype(BF16))
        s.put_state(jnp.exp(s.edge) * st + _dot_tn(s.vb, s.kl))


def _hgrn2_kernel(q_ref, ff_ref, fb_ref, v_ref, g_ref, lb_ref, nw_ref, out_ref, of_ref, ob_ref, st_ref, qs_ref):
    L, dk = q_ref.shape
    c = HG_CHUNK
    nc = L // c

    lbp = lb_ref[...]
    mx = jnp.max(lbp, axis=0, keepdims=True)
    ex = jnp.exp(lbp - mx)
    lb = ex[0:1, :] / jnp.sum(ex, axis=0, keepdims=True)

    st_ref[...] = jnp.zeros_like(st_ref)
    qs_ref[...] = _silu(q_ref[...])

    def stream(t0, f_ref, slot, rev):
        s = _HgStream()
        sig = _sigmoid(f_ref[pl.ds(t0, c), :])
        s.f = lb + (1.0 - lb) * sig
        s.k = (1.0 - lb) * (1.0 - sig)
        s.logf = jnp.log(s.f)
        s.q = qs_ref[pl.ds(t0, c), :]
        s.v = v_ref[pl.ds(t0, c), :]
        s.rev = rev
        s.t0 = t0
        s.get_state = lambda: st_ref[slot]

        def put_state(x):
            st_ref[slot] = x

        s.put_state = put_state
        return s

    per_step = HG_STREAM_CHUNKS

    def body(j, carry):
        fwd = [stream(pl.multiple_of((j * per_step + i) * c, c), ff_ref, 0, False) for i in range(per_step)]
        bwd = [stream(pl.multiple_of((nc - 1 - j * per_step - i) * c, c), fb_ref, 1, True) for i in range(per_step)]
        _hgrn2_chunks(fwd + bwd)
        for s in fwd:
            of_ref[pl.ds(s.t0, c), :] = s.o
        for s in bwd:
            ob_ref[pl.ds(s.t0, c), :] = s.o
        return carry

    lax.fori_loop(0, nc // per_step, body, 0)

    o = of_ref[...] + ob_ref[...]
    o = o * lax.rsqrt(jnp.mean(o * o, axis=1, keepdims=True) + EPS)
    y = o * nw_ref[...] * _silu(g_ref[...])
    out_ref[...] = y.astype(out_ref.dtype)


def _hgrn2(proj, hg_lb, norm_w, B, L):
    dk = HG_D
    H = HG_HEADS
    base = 4 * ML_WIDTH // dk
    return pl.pallas_call(
        _hgrn2_kernel,
        grid=(B, H),
        in_specs=[
            pl.BlockSpec((L, dk), lambda b, h: (b, base + h)),
            pl.BlockSpec((L, dk), lambda b, h: (b, base + H + h)),
            pl.BlockSpec((L, dk), lambda b, h: (b, base + 2 * H + h)),
            pl.BlockSpec((L, dk), lambda b, h: (b, base + 3 * H + h)),
            pl.BlockSpec((L, dk), lambda b, h: (b, base + 4 * H + h)),
            pl.BlockSpec((2, dk), lambda b, h: (0, h)),
            pl.BlockSpec((1, dk), lambda b, h: (0, h)),
        ],
        out_specs=pl.BlockSpec((L, dk), lambda b, h: (b, h)),
        out_shape=jax.ShapeDtypeStruct((B * L, HG_WIDTH), BF16),
        scratch_shapes=[
            pltpu.VMEM((L, dk), F32), pltpu.VMEM((L, dk), F32),
            pltpu.VMEM((2, dk, dk), F32),
            pltpu.VMEM((L, dk), F32),
        ],
        compiler_params=_cparams(("arbitrary", "arbitrary")),
        name="hgrn2",
    )(proj, proj, proj, proj, proj, hg_lb, norm_w)


def _outproj_kernel(x_ref, ml_ref, hg_ref, eg_ref, eb_ref, wo_ref, g1_ref, b1_ref, wrh_ref, wrl_ref,
                    x1_ref, lg_ref):
    half = ml_ref.shape[1]
    mix = _dot(ml_ref[...], wo_ref[0:half, :]) + _dot(hg_ref[...], wo_ref[half:, :])
    h = _layer_norm(x_ref[...], eg_ref[...], eb_ref[...])
    x1 = _layer_norm(ALPHA * h + mix, g1_ref[...], b1_ref[...])
    x1_ref[...] = x1
    xh, xl = _split_bf16(x1)
    lg_ref[...] = _dot(xh, wrh_ref[...]) + _dot(xl, wrh_ref[...]) + _dot(xh, wrl_ref[...])


def _outproj(x, ml, hg, eg, eb, wo, g1, b1, wr_hi, wr_lo, tm=256):
    n, d = x.shape
    row = lambda i: (i, 0)
    fixed = lambda i: (0, 0)
    return pl.pallas_call(
        _outproj_kernel,
        grid=(n // tm,),
        in_specs=[
            pl.BlockSpec((tm, d), row),
            pl.BlockSpec((tm, ML_WIDTH), row),
            pl.BlockSpec((tm, HG_WIDTH), row),
            pl.BlockSpec((1, d), fixed),
            pl.BlockSpec((1, d), fixed),
            pl.BlockSpec((d, d), fixed),
            pl.BlockSpec((1, d), fixed),
            pl.BlockSpec((1, d), fixed),
            pl.BlockSpec((d, LANES), fixed),
            pl.BlockSpec((d, LANES), fixed),
        ],
        out_specs=[pl.BlockSpec((tm, d), row), pl.BlockSpec((tm, LANES), row)],
        out_shape=[jax.ShapeDtypeStruct((n, d), F32), jax.ShapeDtypeStruct((n, LANES), F32)],
        compiler_params=_cparams(("arbitrary",)),
        name="outproj",
    )(x, ml, hg, eg, eb, wo, g1, b1, wr_hi, wr_lo)


def _pad_cols(w, width=LANES):
    return jnp.pad(w, ((0, 0), (0, width - w.shape[1])))


def _token_mixer_stage(x, emb_ln_g, emb_ln_b, w_in, conv_w, ml_igate_b, ml_fgate_b, ml_norm_w,
                       hg_lb, hg_norm_w, w_out, ln1_g, ln1_b, w_router, B, L):
    d = D_MODEL
    g0 = 4 * ML_WIDTH
    w_main = jnp.concatenate([w_in[:, :g0], w_in[:, g0 + 16:]], axis=1).astype(BF16)
    wg_hi, wg_lo = _split_bf16(_pad_cols(w_in[:, g0:g0 + 16]))
    eg = emb_ln_g.reshape(1, d)
    eb = emb_ln_b.reshape(1, d)
    proj, gates = _ln_proj(x, eg, eb, w_main, wg_hi, wg_lo)

    g4 = gates[:, :16].reshape(B, L, 4, ML_HEADS)
    gate_col = g4.transpose(0, 3, 1, 2)
    gate_row = g4.transpose(0, 3, 2, 1)
    bias = jnp.concatenate([ml_igate_b[0], ml_igate_b[1], ml_fgate_b[0], ml_fgate_b[1]]).astype(F32)
    ml = _mlstm(proj, conv_w, gate_col, gate_row, bias, ml_norm_w.reshape(1, ML_WIDTH), B, L)
    hg = _hgrn2(proj, hg_lb, hg_norm_w.reshape(1, HG_WIDTH), B, L)

    wr_hi, wr_lo = _split_bf16(_pad_cols(w_router))
    return _outproj(x, ml, hg, eg, eb, w_out.astype(BF16), ln1_g.reshape(1, d), ln1_b.reshape(1, d),
                    wr_hi, wr_lo)


def _excl_token_cumsum(mask):
    E, R, ln = mask.shape
    mf = jnp.where(mask, 1.0, 0.0)
    mb = mf.astype(BF16)
    upper = jnp.where(lax.broadcasted_iota(I32, (ln, ln), 0) <= lax.broadcasted_iota(I32, (ln, ln), 1),
                      1.0, 0.0).astype(BF16)
    lower = jnp.where(lax.broadcasted_iota(I32, (R, R), 1) < lax.broadcasted_iota(I32, (R, R), 0),
                      1.0, 0.0).astype(BF16)
    ones = jnp.ones((ln, ln), BF16)
    within = _dot(mb.reshape(E * R, ln), upper).reshape(E, R, ln)
    rows = jnp.stack([_dot(_dot(lower, mb[e]).astype(BF16), ones) for e in range(E)], axis=0)
    return within - mf + rows


def _select_kernel(lg_ref, pos_ref, posm_ref, wts_ref, *, cap):
    E = lg_ref.shape[0]
    lg = lg_ref[...]
    mx = jnp.max(lg, axis=0, keepdims=True)
    ex = jnp.exp(lg - mx)
    aff = ex / jnp.sum(ex, axis=0, keepdims=True)

    def count(mask):
        ones = jnp.where(mask, 1.0, 0.0)
        return jnp.sum(jnp.sum(ones, axis=2, keepdims=True), axis=1, keepdims=True)

    def body(i, tbits):
        cand = tbits | lax.shift_left(jnp.int32(1), 30 - i)
        cnt = count(aff >= lax.bitcast_convert_type(cand, F32))
        return jnp.where(cnt >= cap, cand, tbits)

    tbits = lax.fori_loop(0, 31, body, jnp.zeros((E, 1, 1), I32))
    thr = lax.bitcast_convert_type(tbits, F32)
    nxt = lax.bitcast_convert_type(tbits + 1, F32)
    above = aff >= nxt
    band = jnp.logical_and(aff >= thr, jnp.logical_not(above))
    need = cap - count(above)
    sel = jnp.logical_or(above, jnp.logical_and(band, _excl_token_cumsum(band) < need))
    pos = _excl_token_cumsum(sel).astype(I32)
    pos_ref[...] = pos
    posm_ref[...] = jnp.where(sel, pos, -1)
    wts_ref[...] = jnp.where(sel, aff, 0.0)


def _select(lg_t, cap):
    E, R, ln = lg_t.shape
    full = pl.BlockSpec((E, R, ln), lambda i: (0, 0, 0))
    return pl.pallas_call(
        functools.partial(_select_kernel, cap=cap),
        grid=(1,),
        in_specs=[full],
        out_specs=[full, full, full],
        out_shape=[jax.ShapeDtypeStruct((E, R, ln), I32), jax.ShapeDtypeStruct((E, R, ln), I32),
                   jax.ShapeDtypeStruct((E, R, ln), F32)],
        compiler_params=_cparams(("arbitrary",)),
        name="select",
    )(lg_t)


TOK_TILE = 256
SLOT_CHUNK = 64


def _pack_bf16_pairs(x):
    h = x.shape[1] // 2
    lo = lax.shift_right_logical(lax.bitcast_convert_type(x[:, :h], U32), jnp.uint32(16))
    hi = lax.bitcast_convert_type(x[:, h:], U32) & jnp.uint32(0xFFFF0000)
    return hi | lo


def _unpack_bf16_pairs(w):
    lo = lax.bitcast_convert_type(lax.shift_left(w, jnp.uint32(16)), F32).astype(BF16)
    hi = lax.bitcast_convert_type(w & jnp.uint32(0xFFFF0000), F32).astype(BF16)
    return lo, hi


def _dispatch_kernel(off_ref, base_ref, cnt_ref, xa_ref, xb_ref, pos_ref, xe_ref,
                     x16_ref, res_ref, stage_ref, ostage_ref, carry_ref, sem, osem, *, tiles_a, cap_total, pad):
    E = N_EXPERTS
    CH = SLOT_CHUNK
    SUB = SUBLANES
    tc = xa_ref.shape[0]
    t = pl.program_id(0)
    nt = pl.num_programs(0)
    slot = lax.rem(t, 2)

    @pl.when(t < tiles_a)
    def _():
        x16_ref[...] = xa_ref[...].astype(BF16)

    @pl.when(t >= tiles_a)
    def _():
        x16_ref[...] = xb_ref[...].astype(BF16)

    xb = x16_ref[...]

    @pl.when(t == 0)
    def _():
        carry_ref[...] = jnp.zeros_like(carry_ref)

    base = [base_ref[t * E + e] for e in range(E)]
    cnt = [cnt_ref[t * E + e] for e in range(E)]
    al = [pl.multiple_of((b // SUB) * SUB, SUB) for b in base]
    first = [off_ref[t * E + e] - (base[e] - al[e]) for e in range(E)]

    def onehot(e, start, rows):
        kio = lax.broadcasted_iota(I32, (rows, tc), 0)
        rel = pos_ref[e:e + 1, :] - (first[e] + start)
        return jnp.where(rel == kio, 1.0, 0.0).astype(BF16)

    ot = jnp.concatenate([onehot(e, 0, CH) for e in range(E)], axis=0)
    res_ref[...] = _dot(ot, xb)
    for e in range(E):
        res_ref[e * CH:e * CH + SUB, :] += carry_ref[e * SUB:(e + 1) * SUB, :]
    stage_ref[slot] = _pack_bf16_pairs(res_ref[...])

    nxt = [((base[e] + cnt[e]) // SUB) * SUB - al[e] for e in range(E)]
    oc = jnp.concatenate([onehot(e, nxt[e], SUB) for e in range(E)], axis=0)
    new_carry = _dot(oc, xb)
    for e in range(E):
        keep = jnp.where(nxt[e] == 0, 1.0, 0.0)
        carry_ref[e * SUB:(e + 1) * SUB, :] = new_carry[e * SUB:(e + 1) * SUB, :] + keep * carry_ref[e * SUB:(e + 1) * SUB, :]

    def main_copy(s, e, row):
        return pltpu.make_async_copy(stage_ref.at[s, pl.ds(e * CH, CH)], xe_ref.at[pl.ds(row, CH)], sem.at[s])

    @pl.when(t > 0)
    def _():
        for e in range(E):
            main_copy(1 - slot, e, 0).wait()

    for e in range(E):
        main_copy(slot, e, al[e]).start()

    for e in range(E):
        nch = (base[e] - al[e] + cnt[e] + CH - 1) // CH

        def body(c, carry, e=e):
            ostage_ref[...] = _pack_bf16_pairs(_dot(onehot(e, c * CH, CH), xb))
            row = pl.multiple_of(al[e] + c * CH, SUB)
            cp = pltpu.make_async_copy(ostage_ref, xe_ref.at[pl.ds(row, CH)], osem)
            cp.start()
            cp.wait()
            return carry

        lax.fori_loop(1, nch, body, 0)

    @pl.when(t == nt - 1)
    def _():
        for e in range(E):
            main_copy(slot, e, 0).wait()
        ostage_ref[...] = jnp.zeros_like(ostage_ref)
        fills = [pltpu.make_async_copy(ostage_ref, xe_ref.at[pl.ds(e * (cap_total + pad) + cap_total + j * CH, CH)], osem)
                 for e in range(E) for j in range(pad // CH)]
        for cp in fills:
            cp.start()
        for cp in fills:
            cp.wait()


def _dispatch(off, base, cnt, x1_a, x1_b, pos_rows, cap_total, pad):
    d = x1_a.shape[1]
    E = N_EXPERTS
    tc = TOK_TILE
    tiles_a = x1_a.shape[0] // tc
    tiles_b = x1_b.shape[0] // tc
    grid_spec = pltpu.PrefetchScalarGridSpec(
        num_scalar_prefetch=3,
        grid=(tiles_a + tiles_b,),
        in_specs=[
            pl.BlockSpec((tc, d), lambda i, *_: (jnp.minimum(i, tiles_a - 1), 0)),
            pl.BlockSpec((tc, d), lambda i, *_: (jnp.maximum(i - tiles_a, 0), 0)),
            pl.BlockSpec((E, tc), lambda i, *_: (0, i)),
        ],
        out_specs=pl.BlockSpec(memory_space=pl.ANY),
        scratch_shapes=[
            pltpu.VMEM((tc, d), BF16),
            pltpu.VMEM((E * SLOT_CHUNK, d), F32),
            pltpu.VMEM((2, E * SLOT_CHUNK, d // 2), U32),
            pltpu.VMEM((SLOT_CHUNK, d // 2), U32),
            pltpu.VMEM((E * SUBLANES, d), F32),
            pltpu.SemaphoreType.DMA((2,)),
            pltpu.SemaphoreType.DMA(()),
        ],
    )
    return pl.pallas_call(
        functools.partial(_dispatch_kernel, tiles_a=tiles_a, cap_total=cap_total, pad=pad),
        grid_spec=grid_spec,
        out_shape=jax.ShapeDtypeStruct((E * (cap_total + pad), d // 2), U32),
        compiler_params=_cparams(("arbitrary",)),
        name="dispatch",
    )(off, base, cnt, x1_a, x1_b, pos_rows)


def _ffn_kernel(xe_ref, wg_ref, wu_ref, wd_ref, out_ref, xb_ref, acc_ref):
    f = pl.program_id(2)

    @pl.when(f == 0)
    def _():
        h = xe_ref.shape[2]
        lo, hi = _unpack_bf16_pairs(xe_ref[0])
        xb_ref[:, :h] = lo
        xb_ref[:, h:] = hi

    xb = xb_ref[...]
    hid = (_silu(_dot(xb, wg_ref[0])) * _dot(xb, wu_ref[0])).astype(BF16)
    part = _dot(hid, wd_ref[0])

    @pl.when(f == 0)
    def _():
        acc_ref[...] = part

    @pl.when(f > 0)
    def _():
        acc_ref[...] += part

    @pl.when(f == pl.num_programs(2) - 1)
    def _():
        out_ref[0] = acc_ref[...].astype(out_ref.dtype)


def _ffn(xe, wg, wu, wd, cap_total, tf=512):
    E, _, dh = xe.shape
    d = 2 * dh
    ff = wg.shape[2]
    tm = next(t for t in (1024, 512, 256, 128) if cap_total % t == 0)
    return pl.pallas_call(
        _ffn_kernel,
        grid=(E, cap_total // tm, ff // tf),
        in_specs=[
            pl.BlockSpec((1, tm, dh), lambda e, i, f: (e, i, 0)),
            pl.BlockSpec((1, d, tf), lambda e, i, f: (e, 0, f)),
            pl.BlockSpec((1, d, tf), lambda e, i, f: (e, 0, f)),
            pl.BlockSpec((1, tf, d), lambda e, i, f: (e, f, 0)),
        ],
        out_specs=pl.BlockSpec((1, tm, d), lambda e, i, f: (e, i, 0)),
        out_shape=jax.ShapeDtypeStruct((E, cap_total, d), BF16),
        scratch_shapes=[pltpu.VMEM((tm, d), BF16), pltpu.VMEM((tm, d), F32)],
        compiler_params=_cparams(("arbitrary", "arbitrary", "arbitrary")),
        name="ffn",
    )(xe, wg, wu, wd)


def _combine_kernel(base_ref, cnt_ref, x1_ref, grow_ref, w_ref, wstart_ref, g2_ref, b2_ref, ye_ref, out_ref,
                    buf_ref, obuf_ref, acc_ref, sem, osem, *, rows_total):
    E = N_EXPERTS
    CH = SLOT_CHUNK
    tc = x1_ref.shape[0]
    t = pl.program_id(0)
    nt = pl.num_programs(0)
    slot = lax.rem(t, 2)
    last = rows_total - CH
    ALIGN = 2 * SUBLANES

    def window(step, e):
        al = (base_ref[step * E + e] // ALIGN) * ALIGN
        return al, pl.multiple_of(jnp.minimum(al, last), ALIGN)

    def chunk_copy(s, e, start):
        return pltpu.make_async_copy(ye_ref.at[pl.ds(start, CH)], buf_ref.at[s, pl.ds(e * CH, CH)], sem.at[s, e])

    @pl.when(t == 0)
    def _():
        for e in range(E):
            chunk_copy(0, e, window(0, e)[1]).start()

    @pl.when(t + 1 < nt)
    def _():
        for e in range(E):
            chunk_copy(1 - slot, e, window(t + 1, e)[1]).start()

    kio = lax.broadcasted_iota(I32, (tc, CH), 1)

    def weights(e, start, lo):
        grow = grow_ref[:, e:e + 1]
        hit = jnp.logical_and(grow - start == kio, grow >= lo)
        return jnp.where(hit, w_ref[:, e:e + 1], 0.0).astype(BF16)

    col = lax.broadcasted_iota(I32, (E, E * CH), 1)
    spread = jnp.where(col // CH == lax.broadcasted_iota(I32, (E, E * CH), 0), 1.0, 0.0).astype(BF16)
    rel = jnp.clip(grow_ref[...] - wstart_ref[0], -1, CH).astype(F32).astype(BF16)
    k_of_col = (lax.broadcasted_iota(I32, (tc, E * CH), 1) % CH).astype(F32)
    a = jnp.where(_dot(rel, spread) == k_of_col, _dot(w_ref[...].astype(BF16), spread), 0.0).astype(BF16)
    for e in range(E):
        chunk_copy(slot, e, 0).wait()
    acc_ref[...] = ALPHA * x1_ref[...] + _dot(a, buf_ref[slot])

    for e in range(E):
        al = window(t, e)[0]
        nch = (base_ref[t * E + e] - al + cnt_ref[t * E + e] + CH - 1) // CH

        def body(c, carry, e=e, al=al):
            lo = al + c * CH
            start = pl.multiple_of(jnp.minimum(lo, last), ALIGN)
            cp = pltpu.make_async_copy(ye_ref.at[pl.ds(start, CH)], obuf_ref, osem)
            cp.start()
            cp.wait()
            acc_ref[...] += _dot(weights(e, start, lo), obuf_ref[...])
            return carry

        lax.fori_loop(1, nch, body, 0)

    out_ref[...] = _layer_norm(acc_ref[...], g2_ref[...], b2_ref[...])


def _combine(base, cnt, x1, grow_t, w_t, g2, b2, ye, rows_total):
    n, d = x1.shape
    E = N_EXPERTS
    tc = TOK_TILE
    align = 2 * SUBLANES
    wstart = jnp.minimum((base // align) * align, rows_total - SLOT_CHUNK).reshape(n // tc, 1, E)
    grid_spec = pltpu.PrefetchScalarGridSpec(
        num_scalar_prefetch=2,
        grid=(n // tc,),
        in_specs=[
            pl.BlockSpec((tc, d), lambda i, *_: (i, 0)),
            pl.BlockSpec((tc, E), lambda i, *_: (i, 0)),
            pl.BlockSpec((tc, E), lambda i, *_: (i, 0)),
            pl.BlockSpec((1, 1, E), lambda i, *_: (i, 0, 0)),
            pl.BlockSpec((1, d), lambda i, *_: (0, 0)),
            pl.BlockSpec((1, d), lambda i, *_: (0, 0)),
            pl.BlockSpec(memory_space=pl.ANY),
        ],
        out_specs=pl.BlockSpec((tc, d), lambda i, *_: (i, 0)),
        scratch_shapes=[
            pltpu.VMEM((2, E * SLOT_CHUNK, d), BF16),
            pltpu.VMEM((SLOT_CHUNK, d), BF16),
            pltpu.VMEM((tc, d), F32),
            pltpu.SemaphoreType.DMA((2, E)),
            pltpu.SemaphoreType.DMA(()),
        ],
    )
    return pl.pallas_call(
        functools.partial(_combine_kernel, rows_total=rows_total),
        grid_spec=grid_spec,
        out_shape=jax.ShapeDtypeStruct((n, d), F32),
        compiler_params=_cparams(("arbitrary",)),
        name="combine",
    )(base, cnt, x1, grow_t, w_t, wstart, g2, b2, ye)


def _routing_tables(pos, posm, wts, cap, slot0, cap_total, pad):
    E = N_EXPERTS
    n = pos.shape[1] * pos.shape[2]
    tiles = n // TOK_TILE
    eidx = jnp.arange(E, dtype=I32)
    off = pos.reshape(E, n)[:, ::TOK_TILE].T
    cnt = jnp.concatenate([off[1:], jnp.full((1, E), cap, I32)], axis=0) - off
    base_x = off + eidx[None, :] * (cap_total + pad) + slot0
    base_y = off + eidx[None, :] * cap_total + slot0
    pos_rows = posm.reshape(E, n)
    grow_t = jnp.where(pos_rows >= 0, pos_rows + eidx[:, None] * cap_total + slot0, -1).T
    w_t = wts.reshape(E, n).T
    flat = lambda a: a.reshape(tiles * E)
    return flat(off), flat(base_x), flat(base_y), flat(cnt), pos_rows, grow_t, w_t


def kernel(x_prompt, x_sample, emb_ln_g, emb_ln_b, w_in, conv_w, ml_igate_b, ml_fgate_b, ml_norm_w, hg_lb, hg_norm_w, w_out, ln1_g, ln1_b, w_router, w_gate, w_up, w_down, ln2_g, ln2_b):
    E = N_EXPERTS
    d = D_MODEL
    groups = (x_prompt, x_sample)
    caps = [EC_FACTOR * x.shape[0] * x.shape[1] // E for x in groups]
    cap_total = sum(caps)
    pad = 2 * SLOT_CHUNK
    assert all(c % (2 * SUBLANES) == 0 for c in caps)

    staged = []
    tables = []
    slot0 = 0
    for x, cap in zip(groups, caps):
        B, L, _ = x.shape
        n = B * L
        x1, lg = _token_mixer_stage(x.reshape(n, d), emb_ln_g, emb_ln_b, w_in[0], conv_w[0], ml_igate_b[0],
                                    ml_fgate_b[0], ml_norm_w[0], hg_lb, hg_norm_w[0], w_out[0], ln1_g[0],
                                    ln1_b[0], w_router[0], B, L)
        lg_t = lg[:, :E].T.reshape(E, n // LANES, LANES)
        pos, posm, wts = _select(lg_t, cap)
        off, base_x, base_y, cnt, pos_rows, grow_t, w_t = _routing_tables(pos, posm, wts, cap, slot0, cap_total, pad)
        tables.append((off, base_x, cnt, pos_rows))
        staged.append((x1, base_y, cnt, grow_t, w_t, (B, L)))
        slot0 += cap

    off, base_x, cnt, pos_rows = (jnp.concatenate(parts, axis=-1) for parts in zip(*tables))
    xe = _dispatch(off, base_x, cnt, staged[0][0], staged[1][0], pos_rows, cap_total, pad)

    ye = _ffn(xe.reshape(E, cap_total + pad, d // 2), w_gate[0].astype(BF16), w_up[0].astype(BF16),
              w_down[0].astype(BF16), cap_total)
    ye = ye.reshape(E * cap_total, d)

    g2 = ln2_g[0].reshape(1, d)
    b2 = ln2_b[0].reshape(1, d)
    outs = []
    for x1, base_y, cnt, grow_t, w_t, (B, L) in staged:
        y = _combine(base_y, cnt, x1, grow_t, w_t, g2, b2, ye, E * cap_total)
        outs.append(y.reshape(B, L, d))
    return tuple(outs)
```

```python
import functools

import jax
import jax.numpy as jnp
from jax import lax
from jax.experimental import pallas as pl
from jax.experimental.pallas import tpu as pltpu

F32 = jnp.float32
BF16 = jnp.bfloat16
I32 = jnp.int32
U32 = jnp.uint32

D_MODEL = 2048
ML_WIDTH = 1024
ML_HEADS = 4
ML_DH = 256
HG_WIDTH = 1024
HG_HEADS = 8
HG_D = 128
N_EXPERTS = 16
EC_FACTOR = 2
EXPERT_FF = 2048
DEPTH = 1
ALPHA = (2.0 * DEPTH) ** 0.25
EPS = 1e-5

LANES = 128
SUBLANES = 8
MAIN_COLS = 9 * 1024
VMEM_LIMIT = 56 * 1024 * 1024

ML_CHUNK = 256
HG_CHUNK = 128
HG_STREAM_CHUNKS = 2


def _cparams(sem):
    return pltpu.CompilerParams(dimension_semantics=sem, vmem_limit_bytes=VMEM_LIMIT)


def _split_bf16(x):
    hi = x.astype(BF16)
    lo = (x - hi.astype(F32)).astype(BF16)
    return hi, lo


def _layer_norm(x, g, b):
    mu = jnp.mean(x, axis=-1, keepdims=True)
    xc = x - mu
    var = jnp.mean(xc * xc, axis=-1, keepdims=True)
    return xc * lax.rsqrt(var + EPS) * g + b


def _sigmoid(x):
    return 1.0 / (1.0 + jnp.exp(-x))


def _silu(x):
    return x * _sigmoid(x)


def _log_sigmoid(x):
    return jnp.minimum(x, 0.0) - jnp.log1p(jnp.exp(-jnp.abs(x)))


def _dot(a, b):
    return jnp.dot(a, b, preferred_element_type=F32)


def _dot_nt(a, b):
    return lax.dot_general(a, b, (((1,), (1,)), ((), ())), preferred_element_type=F32)


def _dot_tn(a, b):
    return lax.dot_general(a, b, (((0,), (0,)), ((), ())), preferred_element_type=F32)


def _ln_proj_kernel(x_ref, g_ref, b_ref, w_ref, wgh_ref, wgl_ref, proj_ref, gate_ref, hh_ref, hl_ref):
    @pl.when(pl.program_id(1) == 0)
    def _():
        h = _layer_norm(x_ref[...], g_ref[...], b_ref[...])
        hh, hl = _split_bf16(h)
        hh_ref[...] = hh
        hl_ref[...] = hl
        gate_ref[...] = _dot(hh, wgh_ref[...]) + _dot(hl, wgh_ref[...]) + _dot(hh, wgl_ref[...])

    proj_ref[...] = _dot(hh_ref[...], w_ref[...]).astype(proj_ref.dtype)


def _ln_proj(x, g, b, w_main, wg_hi, wg_lo, tm=1024, tn=1024):
    n, d = x.shape
    tm = min(tm, n)
    ncol = w_main.shape[1]
    return pl.pallas_call(
        _ln_proj_kernel,
        grid=(n // tm, ncol // tn),
        in_specs=[
            pl.BlockSpec((tm, d), lambda i, j: (i, 0)),
            pl.BlockSpec((1, d), lambda i, j: (0, 0)),
            pl.BlockSpec((1, d), lambda i, j: (0, 0)),
            pl.BlockSpec((d, tn), lambda i, j: (0, j)),
            pl.BlockSpec((d, LANES), lambda i, j: (0, 0)),
            pl.BlockSpec((d, LANES), lambda i, j: (0, 0)),
        ],
        out_specs=[
            pl.BlockSpec((tm, tn), lambda i, j: (i, j)),
            pl.BlockSpec((tm, LANES), lambda i, j: (i, 0)),
        ],
        out_shape=[
            jax.ShapeDtypeStruct((n, ncol), F32),
            jax.ShapeDtypeStruct((n, LANES), F32),
        ],
        scratch_shapes=[pltpu.VMEM((tm, d), BF16), pltpu.VMEM((tm, d), BF16)],
        compiler_params=_cparams(("arbitrary", "arbitrary")),
        name="ln_proj",
    )(x, g, b, w_main, wg_hi, wg_lo)


class _MlStream:
    pass


def _mlstm_chunks(streams):
    c = streams[0].qb.shape[0]
    tt = lax.broadcasted_iota(I32, (c, c), 0)
    ss = lax.broadcasted_iota(I32, (c, c), 1)

    for s in streams:
        causal = (ss >= tt) if s.rev else (ss <= tt)
        b_col = jnp.sum(jnp.where(causal, s.f_row, 0.0), axis=1, keepdims=True)
        anti = (tt >= ss) if s.rev else (tt <= ss)
        b_row = jnp.sum(jnp.where(anti, s.f_col, 0.0), axis=0, keepdims=True)
        total = jnp.sum(s.f_row, axis=1, keepdims=True)

        dmat = jnp.where(causal, b_col - b_row + s.i_row, -jnp.inf)
        a_inter = b_col + s.m_state
        s.m_t = jnp.maximum(a_inter, jnp.max(dmat, axis=1, keepdims=True))
        s.w_inter = jnp.exp(a_inter - s.m_t)
        s.p = jnp.exp(dmat - s.m_t)
        s.qk = _dot_nt(s.qb, s.kb)
        s.qc = _dot(s.qb, s.c_state.astype(BF16))

        g_col = total - b_col + s.i_col
        g_row = total - b_row + s.i_row
        s.m_new = jnp.maximum(total + s.m_state, jnp.max(g_row, axis=1, keepdims=True))
        s.decay = jnp.exp(total + s.m_state - s.m_new)
        ks = s.kb.astype(F32) * jnp.exp(g_col - s.m_new)
        s.kv = _dot_tn(ks.astype(BF16), s.vb)
        s.n_new = s.decay * s.n_state + jnp.sum(ks, axis=0, keepdims=True)

    for s in streams:
        s.s = s.qk * s.p
        s.sv = _dot(s.s.astype(BF16), s.vb)

    for s in streams:
        num = s.w_inter * s.qc + s.sv
        qn = jnp.sum(s.qb.astype(F32) * s.n_state, axis=1, keepdims=True)
        den = s.w_inter * qn + jnp.sum(s.s, axis=1, keepdims=True)
        s.h = num / jnp.maximum(jnp.abs(den), jnp.exp(-s.m_t))
        s.c_new = s.decay * s.c_state + s.kv


def _mlstm_kernel(bias_ref, q_ref, k_ref, v_ref, o_ref, cwq_ref, cwk_ref, gc_ref, gr_ref, nw_ref,
                  out_ref, qc_ref, kc_ref, vc_ref, hf_ref, hb_ref, cs_ref):
    L, d = q_ref.shape
    c = ML_CHUNK
    nc = L // c
    hd = pl.program_id(1)

    row = lax.broadcasted_iota(I32, (L, 1), 0)

    def conv_silu(x, w):
        xp = jnp.where(row == 0, 0.0, pltpu.roll(x, 1, 0))
        xn = jnp.where(row == L - 1, 0.0, pltpu.roll(x, L - 1, 0))
        return _silu(w[0:1, :] * xp + w[1:2, :] * x + w[2:3, :] * xn)

    qc_ref[...] = conv_silu(q_ref[...], cwq_ref[...]).astype(BF16)
    kc_ref[...] = (conv_silu(k_ref[...], cwk_ref[...]) * (ML_DH ** -0.5)).astype(BF16)
    vc_ref[...] = v_ref[...].astype(BF16)

    bi_f = bias_ref[hd]
    bi_b = bias_ref[ML_HEADS + hd]
    bf_f = bias_ref[2 * ML_HEADS + hd]
    bf_b = bias_ref[3 * ML_HEADS + hd]

    cs_ref[...] = jnp.zeros_like(cs_ref)

    def gates(t0):
        gc = gc_ref[0, 0, pl.ds(t0, c), :]
        gr = gr_ref[0, 0, :, pl.ds(t0, c)]
        return gc, gr

    def stream(t0, slot, rev, bias_i, bias_f, n_state, m_state):
        s = _MlStream()
        gc, gr = gates(t0)
        gi, gf = (1, 3) if rev else (0, 2)
        s.qb = qc_ref[pl.ds(t0, c), :]
        s.kb = kc_ref[pl.ds(t0, c), :]
        s.vb = vc_ref[pl.ds(t0, c), :]
        s.f_col = _log_sigmoid(gc[:, gf:gf + 1] + bias_f)
        s.f_row = _log_sigmoid(gr[gf:gf + 1, :] + bias_f)
        s.i_col = gc[:, gi:gi + 1] + bias_i
        s.i_row = gr[gi:gi + 1, :] + bias_i
        s.c_state = cs_ref[slot]
        s.n_state = n_state
        s.m_state = m_state
        s.rev = rev
        return s

    def body(j, carry):
        n_f, m_f, n_b, m_b = carry
        t0 = pl.multiple_of(j * c, c)
        t1 = pl.multiple_of((nc - 1 - j) * c, c)
        fwd = stream(t0, 0, False, bi_f, bf_f, n_f, m_f)
        bwd = stream(t1, 1, True, bi_b, bf_b, n_b, m_b)
        _mlstm_chunks([fwd, bwd])
        cs_ref[0] = fwd.c_new
        hf_ref[pl.ds(t0, c), :] = fwd.h
        cs_ref[1] = bwd.c_new
        hb_ref[pl.ds(t1, c), :] = bwd.h
        return fwd.n_new, fwd.m_new, bwd.n_new, bwd.m_new

    zn = jnp.zeros((1, d), F32)
    m0 = jnp.full((1, 1), -1e30, F32)
    lax.fori_loop(0, nc, body, (zn, m0, zn, m0))

    hm = hf_ref[...] + hb_ref[...]
    mu = jnp.mean(hm, axis=1, keepdims=True)
    hc = hm - mu
    var = jnp.mean(hc * hc, axis=1, keepdims=True)
    y = hc * lax.rsqrt(var + EPS) * nw_ref[...] * _sigmoid(o_ref[...])
    out_ref[...] = y.astype(out_ref.dtype)


def _mlstm(proj, conv_w, gate_col, gate_row, bias, norm_w, B, L):
    d = ML_DH
    H = ML_HEADS
    grid_spec = pltpu.PrefetchScalarGridSpec(
        num_scalar_prefetch=1,
        grid=(B, H),
        in_specs=[
            pl.BlockSpec((L, d), lambda b, h, s: (b, h)),
            pl.BlockSpec((L, d), lambda b, h, s: (b, H + h)),
            pl.BlockSpec((L, d), lambda b, h, s: (b, 2 * H + h)),
            pl.BlockSpec((L, d), lambda b, h, s: (b, 3 * H + h)),
            pl.BlockSpec((3, d), lambda b, h, s: (0, h)),
            pl.BlockSpec((3, d), lambda b, h, s: (0, H + h)),
            pl.BlockSpec((1, 1, L, 4), lambda b, h, s: (b, h, 0, 0)),
            pl.BlockSpec((1, 1, 4, L), lambda b, h, s: (b, h, 0, 0)),
            pl.BlockSpec((1, d), lambda b, h, s: (0, h)),
        ],
        out_specs=pl.BlockSpec((L, d), lambda b, h, s: (b, h)),
        scratch_shapes=[
            pltpu.VMEM((L, d), BF16), pltpu.VMEM((L, d), BF16), pltpu.VMEM((L, d), BF16),
            pltpu.VMEM((L, d), F32), pltpu.VMEM((L, d), F32),
            pltpu.VMEM((2, d, d), F32),
        ],
    )
    return pl.pallas_call(
        _mlstm_kernel,
        grid_spec=grid_spec,
        out_shape=jax.ShapeDtypeStruct((B * L, ML_WIDTH), BF16),
        compiler_params=_cparams(("arbitrary", "arbitrary")),
        name="mlstm",
    )(bias, proj, proj, proj, proj, conv_w, conv_w, gate_col, gate_row, norm_w)


def _chunk_cumsum(x, rev):
    c, n = x.shape
    r = lax.broadcasted_iota(I32, (c, c), 0)
    s = lax.broadcasted_iota(I32, (c, c), 1)
    tri = jnp.where((s >= r) if rev else (s <= r), 1.0, 0.0).astype(BF16)
    hi = x.astype(BF16)
    r1 = x - hi.astype(F32)
    mid = r1.astype(BF16)
    lo = (r1 - mid.astype(F32)).astype(BF16)
    parts = _dot(tri, jnp.concatenate([hi, mid, lo], axis=1))
    return parts[:, :n] + parts[:, n:2 * n] + parts[:, 2 * n:]


class _HgStream:
    pass


def _hg_cumsum_stage(streams):
    for s in streams:
        s.b = _chunk_cumsum(s.logf, s.rev)
        s.vb = s.v.astype(BF16)
        s.o = jnp.sum(s.q * s.k, axis=1, keepdims=True) * s.v


def _hg_levels_stage(streams):
    c, dk = streams[0].q.shape
    row = lax.broadcasted_iota(I32, (c, dk), 0)
    tt = lax.broadcasted_iota(I32, (c, c), 0)
    ss = lax.broadcasted_iota(I32, (c, c), 1)

    for s in streams:
        strict = (ss > tt) if s.rev else (ss < tt)
        f_prev = pltpu.roll(s.f, 1, 0)
        f_next = pltpu.roll(s.f, c - 1, 0)
        amat = None
        m = c // 2
        while m >= 1:
            blk = 2 * m
            u = row & (blk - 1)
            tgt = (u < m) if s.rev else (u >= m)
            if m == 1:
                e = jnp.where(tgt, s.f, 1.0)
            elif m == 2:
                if s.rev:
                    e = jnp.where(u == 0, s.f * f_next, jnp.where(u == 1, s.f, jnp.where(u == 2, 1.0, f_prev)))
                else:
                    e = jnp.where(u == 0, f_next, jnp.where(u == 1, 1.0, jnp.where(u == 2, s.f, s.f * f_prev)))
            else:
                bb = s.b.reshape(c // blk, blk, dk)
                ref = bb[:, m:m + 1, :] if s.rev else bb[:, m - 1:m, :]
                e = jnp.exp(-jnp.abs(bb - ref)).reshape(c, dk)
            qk = (jnp.where(tgt, s.q, s.k) * e).astype(BF16)
            a = _dot_nt(qk, qk).astype(BF16)
            amat = a if amat is None else jnp.where((tt // blk) == (ss // blk), a, amat)
            m = m // 2
        s.amat = jnp.where(strict, amat, jnp.zeros_like(amat))
        s.edge = s.b[0:1, :] if s.rev else s.b[c - 1:c, :]
        s.qi = (s.q * jnp.exp(s.b)).astype(BF16)
        s.kl = (s.k * jnp.exp(s.edge - s.b)).astype(BF16)


def _hg_intra_stage(streams):
    for s in streams:
        s.o = s.o + _dot(s.amat, s.vb)


def _hg_state_stage(streams):
    for s in streams:
        st = s.get_state()
        s.o = s.o + _dot_nt(s.qi, st.astype(BF16))
        s.put_state(jnp.exp(s.edge) * st + _dot_tn(s.vb, s.kl))


def _hgrn2_kernel(q_ref, ff_ref, fb_ref, v_ref, g_ref, lb_ref, nw_ref, out_ref, of_ref, ob_ref, st_ref, qs_ref,
                  pb_ref, po_ref, pe_ref):
    L, dk = q_ref.shape
    c = HG_CHUNK
    nc = L // c

    lbp = lb_ref[...]
    mx = jnp.max(lbp, axis=0, keepdims=True)
    ex = jnp.exp(lbp - mx)
    lb = ex[0:1, :] / jnp.sum(ex, axis=0, keepdims=True)

    st_ref[...] = jnp.zeros_like(st_ref)
    qs_ref[...] = _silu(q_ref[...])

    per_step = HG_STREAM_CHUNKS
    n_steps = nc // per_step

    def shells(j):
        out = []
        for i in range(per_step):
            for rev in (False, True):
                s = _HgStream()
                chunk = (nc - 1 - j * per_step - i) if rev else (j * per_step + i)
                s.t0 = chunk * c if isinstance(chunk, int) else pl.multiple_of(chunk * c, c)
                s.rev = rev
                s.slot = 1 if rev else 0
                s.get_state = functools.partial(lambda slot: st_ref[slot], s.slot)
                s.put_state = functools.partial(st_ref.__setitem__, s.slot)
                out.append(s)
        return out

    def front(j):
        streams = shells(j)
        for s in streams:
            sig = _sigmoid((fb_ref if s.rev else ff_ref)[pl.ds(s.t0, c), :])
            s.f = lb + (1.0 - lb) * sig
            s.k = (1.0 - lb) * (1.0 - sig)
            s.logf = jnp.log(s.f)
            s.q = qs_ref[pl.ds(s.t0, c), :]
            s.v = v_ref[pl.ds(s.t0, c), :]
        return streams

    def save(streams):
        for i, s in enumerate(streams):
            pb_ref[i, 0] = s.amat
            pb_ref[i, 1] = s.vb
            pb_ref[i, 2] = s.qi
            pb_ref[i, 3] = s.kl
            po_ref[i] = s.o
            pe_ref[i] = jnp.broadcast_to(s.edge, (SUBLANES, dk))

    def load(j):
        streams = shells(j)
        for i, s in enumerate(streams):
            s.amat, s.vb, s.qi, s.kl = pb_ref[i, 0], pb_ref[i, 1], pb_ref[i, 2], pb_ref[i, 3]
            s.o = po_ref[i]
            s.edge = pe_ref[i, 0:1, :]
        return streams

    def emit(streams):
        for s in streams:
            (ob_ref if s.rev else of_ref)[pl.ds(s.t0, c), :] = s.o

    first = front(0)
    _hg_cumsum_stage(first)
    _hg_levels_stage(first)
    save(first)

    def body(j, carry):
        old = load(j - 1)
        new = front(j)
        _hg_intra_stage(old)
        _hg_state_stage(old[:2])
        _hg_cumsum_stage(new)
        _hg_state_stage(old[2:])
        _hg_levels_stage(new)
        emit(old)
        save(new)
        return carry

    lax.fori_loop(1, n_steps, body, 0)

    last = load(n_steps - 1)
    _hg_intra_stage(last)
    _hg_state_stage(last)
    emit(last)

    o = of_ref[...] + ob_ref[...]
    o = o * lax.rsqrt(jnp.mean(o * o, axis=1, keepdims=True) + EPS)
    y = o * nw_ref[...] * _silu(g_ref[...])
    out_ref[...] = y.astype(out_ref.dtype)


def _hgrn2(proj, hg_lb, norm_w, B, L):
    dk = HG_D
    H = HG_HEADS
    base = 4 * ML_WIDTH // dk
    return pl.pallas_call(
        _hgrn2_kernel,
        grid=(B, H),
        in_specs=[
            pl.BlockSpec((L, dk), lambda b, h: (b, base + h)),
            pl.BlockSpec((L, dk), lambda b, h: (b, base + H + h)),
            pl.BlockSpec((L, dk), lambda b, h: (b, base + 2 * H + h)),
            pl.BlockSpec((L, dk), lambda b, h: (b, base + 3 * H + h)),
            pl.BlockSpec((L, dk), lambda b, h: (b, base + 4 * H + h)),
            pl.BlockSpec((2, dk), lambda b, h: (0, h)),
            pl.BlockSpec((1, dk), lambda b, h: (0, h)),
        ],
        out_specs=pl.BlockSpec((L, dk), lambda b, h: (b, h)),
        out_shape=jax.ShapeDtypeStruct((B * L, HG_WIDTH), BF16),
        scratch_shapes=[
            pltpu.VMEM((L, dk), F32), pltpu.VMEM((L, dk), F32),
            pltpu.VMEM((2, dk, dk), F32),
            pltpu.VMEM((L, dk), F32),
            pltpu.VMEM((2 * HG_STREAM_CHUNKS, 4, HG_CHUNK, dk), BF16),
            pltpu.VMEM((2 * HG_STREAM_CHUNKS, HG_CHUNK, dk), F32),
            pltpu.VMEM((2 * HG_STREAM_CHUNKS, SUBLANES, dk), F32),
        ],
        compiler_params=_cparams(("arbitrary", "arbitrary")),
        name="hgrn2",
    )(proj, proj, proj, proj, proj, hg_lb, norm_w)


def _outproj_kernel(x_ref, ml_ref, hg_ref, eg_ref, eb_ref, wo_ref, g1_ref, b1_ref, wrh_ref, wrl_ref,
                    x1_ref, lg_ref):
    half = ml_ref.shape[1]
    mix = _dot(ml_ref[...], wo_ref[0:half, :]) + _dot(hg_ref[...], wo_ref[half:, :])
    h = _layer_norm(x_ref[...], eg_ref[...], eb_ref[...])
    x1 = _layer_norm(ALPHA * h + mix, g1_ref[...], b1_ref[...])
    x1_ref[...] = x1
    xh, xl = _split_bf16(x1)
    lg_ref[...] = _dot(xh, wrh_ref[...]) + _dot(xl, wrh_ref[...]) + _dot(xh, wrl_ref[...])


def _outproj(x, ml, hg, eg, eb, wo, g1, b1, wr_hi, wr_lo, tm=256):
    n, d = x.shape
    row = lambda i: (i, 0)
    fixed = lambda i: (0, 0)
    return pl.pallas_call(
        _outproj_kernel,
        grid=(n // tm,),
        in_specs=[
            pl.BlockSpec((tm, d), row),
            pl.BlockSpec((tm, ML_WIDTH), row),
            pl.BlockSpec((tm, HG_WIDTH), row),
            pl.BlockSpec((1, d), fixed),
            pl.BlockSpec((1, d), fixed),
            pl.BlockSpec((d, d), fixed),
            pl.BlockSpec((1, d), fixed),
            pl.BlockSpec((1, d), fixed),
            pl.BlockSpec((d, LANES), fixed),
            pl.BlockSpec((d, LANES), fixed),
        ],
        out_specs=[pl.BlockSpec((tm, d), row), pl.BlockSpec((tm, LANES), row)],
        out_shape=[jax.ShapeDtypeStruct((n, d), F32), jax.ShapeDtypeStruct((n, LANES), F32)],
        compiler_params=_cparams(("arbitrary",)),
        name="outproj",
    )(x, ml, hg, eg, eb, wo, g1, b1, wr_hi, wr_lo)


def _pad_cols(w, width=LANES):
    return jnp.pad(w, ((0, 0), (0, width - w.shape[1])))


def _token_mixer_stage(x, emb_ln_g, emb_ln_b, w_in, conv_w, ml_igate_b, ml_fgate_b, ml_norm_w,
                       hg_lb, hg_norm_w, w_out, ln1_g, ln1_b, w_router, B, L):
    d = D_MODEL
    g0 = 4 * ML_WIDTH
    w_main = jnp.concatenate([w_in[:, :g0], w_in[:, g0 + 16:]], axis=1).astype(BF16)
    wg_hi, wg_lo = _split_bf16(_pad_cols(w_in[:, g0:g0 + 16]))
    eg = emb_ln_g.reshape(1, d)
    eb = emb_ln_b.reshape(1, d)
    proj, gates = _ln_proj(x, eg, eb, w_main, wg_hi, wg_lo)

    g4 = gates[:, :16].reshape(B, L, 4, ML_HEADS)
    gate_col = g4.transpose(0, 3, 1, 2)
    gate_row = g4.transpose(0, 3, 2, 1)
    bias = jnp.concatenate([ml_igate_b[0], ml_igate_b[1], ml_fgate_b[0], ml_fgate_b[1]]).astype(F32)
    ml = _mlstm(proj, conv_w, gate_col, gate_row, bias, ml_norm_w.reshape(1, ML_WIDTH), B, L)
    hg = _hgrn2(proj, hg_lb, hg_norm_w.reshape(1, HG_WIDTH), B, L)

    wr_hi, wr_lo = _split_bf16(_pad_cols(w_router))
    return _outproj(x, ml, hg, eg, eb, w_out.astype(BF16), ln1_g.reshape(1, d), ln1_b.reshape(1, d),
                    wr_hi, wr_lo)


def _excl_token_cumsum(mask):
    E, R, ln = mask.shape
    mf = jnp.where(mask, 1.0, 0.0)
    mb = mf.astype(BF16)
    upper = jnp.where(lax.broadcasted_iota(I32, (ln, ln), 0) <= lax.broadcasted_iota(I32, (ln, ln), 1),
                      1.0, 0.0).astype(BF16)
    lower = jnp.where(lax.broadcasted_iota(I32, (R, R), 1) < lax.broadcasted_iota(I32, (R, R), 0),
                      1.0, 0.0).astype(BF16)
    ones = jnp.ones((ln, ln), BF16)
    within = _dot(mb.reshape(E * R, ln), upper).reshape(E, R, ln)
    rows = jnp.stack([_dot(_dot(lower, mb[e]).astype(BF16), ones) for e in range(E)], axis=0)
    return within - mf + rows


def _select_kernel(lg_ref, pos_ref, posm_ref, wts_ref, *, cap):
    E = lg_ref.shape[0]
    lg = lg_ref[...]
    mx = jnp.max(lg, axis=0, keepdims=True)
    ex = jnp.exp(lg - mx)
    aff = ex / jnp.sum(ex, axis=0, keepdims=True)

    def count(mask):
        ones = jnp.where(mask, 1.0, 0.0)
        return jnp.sum(jnp.sum(ones, axis=2, keepdims=True), axis=1, keepdims=True)

    def body(i, tbits):
        cand = tbits | lax.shift_left(jnp.int32(1), 30 - i)
        cnt = count(aff >= lax.bitcast_convert_type(cand, F32))
        return jnp.where(cnt >= cap, cand, tbits)

    tbits = lax.fori_loop(0, 31, body, jnp.zeros((E, 1, 1), I32))
    thr = lax.bitcast_convert_type(tbits, F32)
    nxt = lax.bitcast_convert_type(tbits + 1, F32)
    above = aff >= nxt
    band = jnp.logical_and(aff >= thr, jnp.logical_not(above))
    need = cap - count(above)
    sel = jnp.logical_or(above, jnp.logical_and(band, _excl_token_cumsum(band) < need))
    pos = _excl_token_cumsum(sel).astype(I32)
    pos_ref[...] = pos
    posm_ref[...] = jnp.where(sel, pos, -1)
    wts_ref[...] = jnp.where(sel, aff, 0.0)


def _select(lg_t, cap):
    E, R, ln = lg_t.shape
    full = pl.BlockSpec((E, R, ln), lambda i: (0, 0, 0))
    return pl.pallas_call(
        functools.partial(_select_kernel, cap=cap),
        grid=(1,),
        in_specs=[full],
        out_specs=[full, full, full],
        out_shape=[jax.ShapeDtypeStruct((E, R, ln), I32), jax.ShapeDtypeStruct((E, R, ln), I32),
                   jax.ShapeDtypeStruct((E, R, ln), F32)],
        compiler_params=_cparams(("arbitrary",)),
        name="select",
    )(lg_t)


TOK_TILE = 256
SLOT_CHUNK = 64


def _pack_bf16_pairs(x):
    h = x.shape[1] // 2
    lo = lax.shift_right_logical(lax.bitcast_convert_type(x[:, :h], U32), jnp.uint32(16))
    hi = lax.bitcast_convert_type(x[:, h:], U32) & jnp.uint32(0xFFFF0000)
    return hi | lo


def _unpack_bf16_pairs(w):
    lo = lax.bitcast_convert_type(lax.shift_left(w, jnp.uint32(16)), F32).astype(BF16)
    hi = lax.bitcast_convert_type(w & jnp.uint32(0xFFFF0000), F32).astype(BF16)
    return lo, hi


def _dispatch_kernel(off_ref, base_ref, cnt_ref, xa_ref, xb_ref, pos_ref, xe_ref,
                     x16_ref, res_ref, stage_ref, ostage_ref, carry_ref, sem, osem, *, tiles_a, cap_total, pad):
    E = N_EXPERTS
    CH = SLOT_CHUNK
    SUB = SUBLANES
    tc = xa_ref.shape[0]
    t = pl.program_id(0)
    nt = pl.num_programs(0)
    slot = lax.rem(t, 2)

    @pl.when(t < tiles_a)
    def _():
        x16_ref[...] = xa_ref[...].astype(BF16)

    @pl.when(t >= tiles_a)
    def _():
        x16_ref[...] = xb_ref[...].astype(BF16)

    xb = x16_ref[...]

    @pl.when(t == 0)
    def _():
        carry_ref[...] = jnp.zeros_like(carry_ref)

    base = [base_ref[t * E + e] for e in range(E)]
    cnt = [cnt_ref[t * E + e] for e in range(E)]
    al = [pl.multiple_of((b // SUB) * SUB, SUB) for b in base]
    first = [off_ref[t * E + e] - (base[e] - al[e]) for e in range(E)]

    def onehot(e, start, rows):
        kio = lax.broadcasted_iota(I32, (rows, tc), 0)
        rel = pos_ref[e:e + 1, :] - (first[e] + start)
        return jnp.where(rel == kio, 1.0, 0.0).astype(BF16)

    ot = jnp.concatenate([onehot(e, 0, CH) for e in range(E)], axis=0)
    res_ref[...] = _dot(ot, xb)
    for e in range(E):
        res_ref[e * CH:e * CH + SUB, :] += carry_ref[e * SUB:(e + 1) * SUB, :]
    stage_ref[slot] = _pack_bf16_pairs(res_ref[...])

    nxt = [((base[e] + cnt[e]) // SUB) * SUB - al[e] for e in range(E)]
    oc = jnp.concatenate([onehot(e, nxt[e], SUB) for e in range(E)], axis=0)
    new_carry = _dot(oc, xb)
    for e in range(E):
        keep = jnp.where(nxt[e] == 0, 1.0, 0.0)
        carry_ref[e * SUB:(e + 1) * SUB, :] = new_carry[e * SUB:(e + 1) * SUB, :] + keep * carry_ref[e * SUB:(e + 1) * SUB, :]

    def main_copy(s, e, row):
        return pltpu.make_async_copy(stage_ref.at[s, pl.ds(e * CH, CH)], xe_ref.at[pl.ds(row, CH)], sem.at[s])

    @pl.when(t > 0)
    def _():
        for e in range(E):
            main_copy(1 - slot, e, 0).wait()

    for e in range(E):
        main_copy(slot, e, al[e]).start()

    for e in range(E):
        nch = (base[e] - al[e] + cnt[e] + CH - 1) // CH

        def body(c, carry, e=e):
            ostage_ref[...] = _pack_bf16_pairs(_dot(onehot(e, c * CH, CH), xb))
            row = pl.multiple_of(al[e] + c * CH, SUB)
            cp = pltpu.make_async_copy(ostage_ref, xe_ref.at[pl.ds(row, CH)], osem)
            cp.start()
            cp.wait()
            return carry

        lax.fori_loop(1, nch, body, 0)

    @pl.when(t == nt - 1)
    def _():
        for e in range(E):
            main_copy(slot, e, 0).wait()
        ostage_ref[...] = jnp.zeros_like(ostage_ref)
        fills = [pltpu.make_async_copy(ostage_ref, xe_ref.at[pl.ds(e * (cap_total + pad) + cap_total + j * CH, CH)], osem)
                 for e in range(E) for j in range(pad // CH)]
        for cp in fills:
            cp.start()
        for cp in fills:
            cp.wait()


def _dispatch(off, base, cnt, x1_a, x1_b, pos_rows, cap_total, pad):
    d = x1_a.shape[1]
    E = N_EXPERTS
    tc = TOK_TILE
    tiles_a = x1_a.shape[0] // tc
    tiles_b = x1_b.shape[0] // tc
    grid_spec = pltpu.PrefetchScalarGridSpec(
        num_scalar_prefetch=3,
        grid=(tiles_a + tiles_b,),
        in_specs=[
            pl.BlockSpec((tc, d), lambda i, *_: (jnp.minimum(i, tiles_a - 1), 0)),
            pl.BlockSpec((tc, d), lambda i, *_: (jnp.maximum(i - tiles_a, 0), 0)),
            pl.BlockSpec((E, tc), lambda i, *_: (0, i)),
        ],
        out_specs=pl.BlockSpec(memory_space=pl.ANY),
        scratch_shapes=[
            pltpu.VMEM((tc, d), BF16),
            pltpu.VMEM((E * SLOT_CHUNK, d), F32),
            pltpu.VMEM((2, E * SLOT_CHUNK, d // 2), U32),
            pltpu.VMEM((SLOT_CHUNK, d // 2), U32),
            pltpu.VMEM((E * SUBLANES, d), F32),
            pltpu.SemaphoreType.DMA((2,)),
            pltpu.SemaphoreType.DMA(()),
        ],
    )
    return pl.pallas_call(
        functools.partial(_dispatch_kernel, tiles_a=tiles_a, cap_total=cap_total, pad=pad),
        grid_spec=grid_spec,
        out_shape=jax.ShapeDtypeStruct((E * (cap_total + pad), d // 2), U32),
        compiler_params=_cparams(("arbitrary",)),
        name="dispatch",
    )(off, base, cnt, x1_a, x1_b, pos_rows)


def _ffn_kernel(xe_ref, wg_ref, wu_ref, wd_ref, out_ref, xb_ref, acc_ref):
    f = pl.program_id(2)

    @pl.when(f == 0)
    def _():
        h = xe_ref.shape[2]
        lo, hi = _unpack_bf16_pairs(xe_ref[0])
        xb_ref[:, :h] = lo
        xb_ref[:, h:] = hi
        acc_ref[...] = jnp.zeros_like(acc_ref)

    xb = xb_ref[...]
    hid = (_silu(_dot(xb, wg_ref[0])) * _dot(xb, wu_ref[0])).astype(BF16)
    acc_ref[...] += _dot(hid, wd_ref[0])

    @pl.when(f == pl.num_programs(2) - 1)
    def _():
        out_ref[0] = acc_ref[...].astype(out_ref.dtype)


def _ffn(xe, wg, wu, wd, cap_total, tf=512):
    E, _, dh = xe.shape
    d = 2 * dh
    ff = wg.shape[2]
    tm = next(t for t in (1024, 512, 256, 128) if cap_total % t == 0)
    return pl.pallas_call(
        _ffn_kernel,
        grid=(E, cap_total // tm, ff // tf),
        in_specs=[
            pl.BlockSpec((1, tm, dh), lambda e, i, f: (e, i, 0)),
            pl.BlockSpec((1, d, tf), lambda e, i, f: (e, 0, f)),
            pl.BlockSpec((1, d, tf), lambda e, i, f: (e, 0, f)),
            pl.BlockSpec((1, tf, d), lambda e, i, f: (e, f, 0)),
        ],
        out_specs=pl.BlockSpec((1, tm, d), lambda e, i, f: (e, i, 0)),
        out_shape=jax.ShapeDtypeStruct((E, cap_total, d), BF16),
        scratch_shapes=[pltpu.VMEM((tm, d), BF16), pltpu.VMEM((tm, d), F32)],
        compiler_params=_cparams(("arbitrary", "arbitrary", "arbitrary")),
        name="ffn",
    )(xe, wg, wu, wd)


def _combine_kernel(base_ref, cnt_ref, x1_ref, grow_ref, w_ref, wstart_ref, g2_ref, b2_ref, ye_ref, out_ref,
                    buf_ref, obuf_ref, acc_ref, sem, osem, *, rows_total):
    E = N_EXPERTS
    CH = SLOT_CHUNK
    tc = x1_ref.shape[0]
    t = pl.program_id(0)
    nt = pl.num_programs(0)
    slot = lax.rem(t, 2)
    last = rows_total - CH
    ALIGN = 2 * SUBLANES

    def window(step, e):
        al = (base_ref[step * E + e] // ALIGN) * ALIGN
        return al, pl.multiple_of(jnp.minimum(al, last), ALIGN)

    def chunk_copy(s, e, start):
        return pltpu.make_async_copy(ye_ref.at[pl.ds(start, CH)], buf_ref.at[s, pl.ds(e * CH, CH)], sem.at[s, e])

    @pl.when(t == 0)
    def _():
        for e in range(E):
            chunk_copy(0, e, window(0, e)[1]).start()

    @pl.when(t + 1 < nt)
    def _():
        for e in range(E):
            chunk_copy(1 - slot, e, window(t + 1, e)[1]).start()

    kio = lax.broadcasted_iota(I32, (tc, CH), 1)

    def weights(e, start, lo):
        grow = grow_ref[:, e:e + 1]
        hit = jnp.logical_and(grow - start == kio, grow >= lo)
        return jnp.where(hit, w_ref[:, e:e + 1], 0.0).astype(BF16)

    col = lax.broadcasted_iota(I32, (E, E * CH), 1)
    spread = jnp.where(col // CH == lax.broadcasted_iota(I32, (E, E * CH), 0), 1.0, 0.0).astype(BF16)
    rel = jnp.clip(grow_ref[...] - wstart_ref[0], -1, CH).astype(F32).astype(BF16)
    k_of_col = (lax.broadcasted_iota(I32, (tc, E * CH), 1) % CH).astype(F32)
    a = jnp.where(_dot(rel, spread) == k_of_col, _dot(w_ref[...].astype(BF16), spread), 0.0).astype(BF16)
    for e in range(E):
        chunk_copy(slot, e, 0).wait()
    acc_ref[...] = ALPHA * x1_ref[...] + _dot(a, buf_ref[slot])

    for e in range(E):
        al = window(t, e)[0]
        nch = (base_ref[t * E + e] - al + cnt_ref[t * E + e] + CH - 1) // CH

        def body(c, carry, e=e, al=al):
            lo = al + c * CH
            start = pl.multiple_of(jnp.minimum(lo, last), ALIGN)
            cp = pltpu.make_async_copy(ye_ref.at[pl.ds(start, CH)], obuf_ref, osem)
            cp.start()
            cp.wait()
            acc_ref[...] += _dot(weights(e, start, lo), obuf_ref[...])
            return carry

        lax.fori_loop(1, nch, body, 0)

    out_ref[...] = _layer_norm(acc_ref[...], g2_ref[...], b2_ref[...])


def _combine(base, cnt, x1, grow_t, w_t, g2, b2, ye, rows_total):
    n, d = x1.shape
    E = N_EXPERTS
    tc = TOK_TILE
    align = 2 * SUBLANES
    wstart = jnp.minimum((base // align) * align, rows_total - SLOT_CHUNK).reshape(n // tc, 1, E)
    grid_spec = pltpu.PrefetchScalarGridSpec(
        num_scalar_prefetch=2,
        grid=(n // tc,),
        in_specs=[
            pl.BlockSpec((tc, d), lambda i, *_: (i, 0)),
            pl.BlockSpec((tc, E), lambda i, *_: (i, 0)),
            pl.BlockSpec((tc, E), lambda i, *_: (i, 0)),
            pl.BlockSpec((1, 1, E), lambda i, *_: (i, 0, 0)),
            pl.BlockSpec((1, d), lambda i, *_: (0, 0)),
            pl.BlockSpec((1, d), lambda i, *_: (0, 0)),
            pl.BlockSpec(memory_space=pl.ANY),
        ],
        out_specs=pl.BlockSpec((tc, d), lambda i, *_: (i, 0)),
        scratch_shapes=[
            pltpu.VMEM((2, E * SLOT_CHUNK, d), BF16),
            pltpu.VMEM((SLOT_CHUNK, d), BF16),
            pltpu.VMEM((tc, d), F32),
            pltpu.SemaphoreType.DMA((2, E)),
            pltpu.SemaphoreType.DMA(()),
        ],
    )
    return pl.pallas_call(
        functools.partial(_combine_kernel, rows_total=rows_total),
        grid_spec=grid_spec,
        out_shape=jax.ShapeDtypeStruct((n, d), F32),
        compiler_params=_cparams(("arbitrary",)),
        name="combine",
    )(base, cnt, x1, grow_t, w_t, wstart, g2, b2, ye)


def _routing_tables(pos, posm, wts, cap, slot0, cap_total, pad):
    E = N_EXPERTS
    n = pos.shape[1] * pos.shape[2]
    tiles = n // TOK_TILE
    eidx = jnp.arange(E, dtype=I32)
    off = pos.reshape(E, n)[:, ::TOK_TILE].T
    cnt = jnp.concatenate([off[1:], jnp.full((1, E), cap, I32)], axis=0) - off
    base_x = off + eidx[None, :] * (cap_total + pad) + slot0
    base_y = off + eidx[None, :] * cap_total + slot0
    pos_rows = posm.reshape(E, n)
    grow_t = jnp.where(pos_rows >= 0, pos_rows + eidx[:, None] * cap_total + slot0, -1).T
    w_t = wts.reshape(E, n).T
    flat = lambda a: a.reshape(tiles * E)
    return flat(off), flat(base_x), flat(base_y), flat(cnt), pos_rows, grow_t, w_t


def kernel(x_prompt, x_sample, emb_ln_g, emb_ln_b, w_in, conv_w, ml_igate_b, ml_fgate_b, ml_norm_w, hg_lb, hg_norm_w, w_out, ln1_g, ln1_b, w_router, w_gate, w_up, w_down, ln2_g, ln2_b):
    E = N_EXPERTS
    d = D_MODEL
    groups = (x_prompt, x_sample)
    caps = [EC_FACTOR * x.shape[0] * x.shape[1] // E for x in groups]
    cap_total = sum(caps)
    pad = 2 * SLOT_CHUNK
    assert all(c % (2 * SUBLANES) == 0 for c in caps)

    staged = []
    tables = []
    slot0 = 0
    for x, cap in zip(groups, caps):
        B, L, _ = x.shape
        n = B * L
        x1, lg = _token_mixer_stage(x.reshape(n, d), emb_ln_g, emb_ln_b, w_in[0], conv_w[0], ml_igate_b[0],
                                    ml_fgate_b[0], ml_norm_w[0], hg_lb, hg_norm_w[0], w_out[0], ln1_g[0],
                                    ln1_b[0], w_router[0], B, L)
        lg_t = lg[:, :E].T.reshape(E, n // LANES, LANES)
        pos, posm, wts = _select(lg_t, cap)
        off, base_x, base_y, cnt, pos_rows, grow_t, w_t = _routing_tables(pos, posm, wts, cap, slot0, cap_total, pad)
        tables.append((off, base_x, cnt, pos_rows))
        staged.append((x1, base_y, cnt, grow_t, w_t, (B, L)))
        slot0 += cap

    off, base_x, cnt, pos_rows = (jnp.concatenate(parts, axis=-1) for parts in zip(*tables))
    xe = _dispatch(off, base_x, cnt, staged[0][0], staged[1][0], pos_rows, cap_total, pad)

    ye = _ffn(xe.reshape(E, cap_total + pad, d // 2), w_gate[0].astype(BF16), w_up[0].astype(BF16),
              w_down[0].astype(BF16), cap_total)
    ye = ye.reshape(E * cap_total, d)

    g2 = ln2_g[0].reshape(1, d)
    b2 = ln2_b[0].reshape(1, d)
    outs = []
    for x1, base_y, cnt, grow_t, w_t, (B, L) in staged:
        y = _combine(base_y, cnt, x1, grow_t, w_t, g2, b2, ye, E * cap_total)
        outs.append(y.reshape(B, L, d))
    return tuple(outs)
```

```python
import functools

import jax
import jax.numpy as jnp
from jax import lax
from jax.experimental import pallas as pl
from jax.experimental.pallas import tpu as pltpu

F32 = jnp.float32
BF16 = jnp.bfloat16
I32 = jnp.int32
U32 = jnp.uint32

D_MODEL = 2048
ML_WIDTH = 1024
ML_HEADS = 4
ML_DH = 256
HG_WIDTH = 1024
HG_HEADS = 8
HG_D = 128
N_EXPERTS = 16
EC_FACTOR = 2
EXPERT_FF = 2048
DEPTH = 1
ALPHA = (2.0 * DEPTH) ** 0.25
EPS = 1e-5

LANES = 128
SUBLANES = 8
MAIN_COLS = 9 * 1024
VMEM_LIMIT = 56 * 1024 * 1024

ML_CHUNK = 256
HG_CHUNK = 128
HG_STREAM_CHUNKS = 2


def _cparams(sem):
    return pltpu.CompilerParams(dimension_semantics=sem, vmem_limit_bytes=VMEM_LIMIT)


def _split_bf16(x):
    hi = x.astype(BF16)
    lo = (x - hi.astype(F32)).astype(BF16)
    return hi, lo


def _layer_norm(x, g, b):
    mu = jnp.mean(x, axis=-1, keepdims=True)
    xc = x - mu
    var = jnp.mean(xc * xc, axis=-1, keepdims=True)
    return xc * lax.rsqrt(var + EPS) * g + b


def _sigmoid(x):
    return 1.0 / (1.0 + jnp.exp(-x))


def _silu(x):
    return x * _sigmoid(x)


def _log_sigmoid(x):
    return jnp.minimum(x, 0.0) - jnp.log1p(jnp.exp(-jnp.abs(x)))


def _dot(a, b):
    return jnp.dot(a, b, preferred_element_type=F32)


def _dot_nt(a, b):
    return lax.dot_general(a, b, (((1,), (1,)), ((), ())), preferred_element_type=F32)


def _dot_tn(a, b):
    return lax.dot_general(a, b, (((0,), (0,)), ((), ())), preferred_element_type=F32)


def _ln_proj_kernel(x_ref, g_ref, b_ref, w_ref, wg_ref, proj_ref, gate_ref,
                    hha_ref, hla_ref, hhb_ref, hlb_ref, *, n_slices):
    i = pl.program_id(0)
    j = pl.program_id(1)
    rs = x_ref.shape[0]

    @pl.when(jnp.logical_and(i == 0, j == 0))
    def _():
        hhb_ref[...] = jnp.zeros_like(hhb_ref)
        hlb_ref[...] = jnp.zeros_like(hlb_ref)

    def step(hh_new, hl_new, hh_old, hl_old):
        @pl.when(j == 0)
        def _():
            both = _dot(hh_old[...], wg_ref[...])
            gate_ref[...] = both[:, :LANES] + both[:, LANES:] + _dot(hl_old[...], wg_ref[:, :LANES])

        proj_ref[...] = _dot(hh_old[...], w_ref[...]).astype(proj_ref.dtype)
        r0 = pl.multiple_of(jnp.minimum(j, n_slices - 1) * rs, rs)
        hh, hl = _split_bf16(_layer_norm(x_ref[...], g_ref[...], b_ref[...]))
        hh_new[pl.ds(r0, rs), :] = hh
        hl_new[pl.ds(r0, rs), :] = hl

    @pl.when(lax.rem(i, 2) == 0)
    def _():
        step(hha_ref, hla_ref, hhb_ref, hlb_ref)

    @pl.when(lax.rem(i, 2) == 1)
    def _():
        step(hhb_ref, hlb_ref, hha_ref, hla_ref)


def _ln_proj(x, g, b, w_main, wg_both, tm=1024, tn=1024):
    n, d = x.shape
    tm = min(tm, n)
    ncol = w_main.shape[1] // tn
    n_tiles = n // tm
    n_slices = ncol - 1
    rs = tm // n_slices
    assert tm % n_slices == 0 and rs % SUBLANES == 0
    prev_row = lambda i, j: jnp.maximum(i - 1, 0)
    return pl.pallas_call(
        functools.partial(_ln_proj_kernel, n_slices=n_slices),
        grid=(n_tiles + 1, ncol),
        in_specs=[
            pl.BlockSpec((rs, d), lambda i, j: (jnp.minimum(i, n_tiles - 1) * n_slices
                                                + jnp.minimum(j, n_slices - 1), 0)),
            pl.BlockSpec((1, d), lambda i, j: (0, 0)),
            pl.BlockSpec((1, d), lambda i, j: (0, 0)),
            pl.BlockSpec((d, tn), lambda i, j: (0, j)),
            pl.BlockSpec((d, 2 * LANES), lambda i, j: (0, 0)),
        ],
        out_specs=[
            pl.BlockSpec((tm, tn), lambda i, j: (prev_row(i, j), jnp.where(i > 0, j, 0))),
            pl.BlockSpec((tm, LANES), lambda i, j: (prev_row(i, j), 0)),
        ],
        out_shape=[
            jax.ShapeDtypeStruct((n, ncol * tn), F32),
            jax.ShapeDtypeStruct((n, LANES), F32),
        ],
        scratch_shapes=[pltpu.VMEM((tm, d), BF16)] * 4,
        compiler_params=_cparams(("arbitrary", "arbitrary")),
        name="ln_proj",
    )(x, g, b, w_main, wg_both)


class _MlStream:
    pass


def _mlstm_chunks(streams):
    c = streams[0].qb.shape[0]
    tt = lax.broadcasted_iota(I32, (c, c), 0)
    ss = lax.broadcasted_iota(I32, (c, c), 1)

    for s in streams:
        causal = (ss >= tt) if s.rev else (ss <= tt)
        b_col = jnp.sum(jnp.where(causal, s.f_row, 0.0), axis=1, keepdims=True)
        anti = (tt >= ss) if s.rev else (tt <= ss)
        b_row = jnp.sum(jnp.where(anti, s.f_col, 0.0), axis=0, keepdims=True)
        total = jnp.sum(s.f_row, axis=1, keepdims=True)

        dmat = jnp.where(causal, b_col - b_row + s.i_row, -jnp.inf)
        a_inter = b_col + s.m_state
        s.m_t = jnp.maximum(a_inter, jnp.max(dmat, axis=1, keepdims=True))
        s.w_inter = jnp.exp(a_inter - s.m_t)
        s.p = jnp.exp(dmat - s.m_t)
        s.qk = _dot_nt(s.qb, s.kb)
        s.qc = _dot(s.qb, s.c_state.astype(BF16))

        g_col = total - b_col + s.i_col
        g_row = total - b_row + s.i_row
        s.m_new = jnp.maximum(total + s.m_state, jnp.max(g_row, axis=1, keepdims=True))
        s.decay = jnp.exp(total + s.m_state - s.m_new)
        ks = s.kb.astype(F32) * jnp.exp(g_col - s.m_new)
        s.kv = _dot_tn(ks.astype(BF16), s.vb)
        s.n_new = s.decay * s.n_state + jnp.sum(ks, axis=0, keepdims=True)

    for s in streams:
        s.s = s.qk * s.p
        s.sv = _dot(s.s.astype(BF16), s.vb)

    for s in streams:
        num = s.w_inter * s.qc + s.sv
        qn = jnp.sum(s.qb.astype(F32) * s.n_state, axis=1, keepdims=True)
        den = s.w_inter * qn + jnp.sum(s.s, axis=1, keepdims=True)
        s.h = num / jnp.maximum(jnp.abs(den), jnp.exp(-s.m_t))
        s.c_new = s.decay * s.c_state + s.kv


def _mlstm_kernel(bias_ref, q_ref, k_ref, v_ref, o_ref, cwq_ref, cwk_ref, gc_ref, gr_ref, nw_ref,
                  out_ref, qc_ref, kc_ref, vc_ref, hf_ref, hb_ref, cs_ref):
    L, d = q_ref.shape
    c = ML_CHUNK
    nc = L // c
    hd = pl.program_id(1)

    row = lax.broadcasted_iota(I32, (L, 1), 0)

    def conv_silu(x, w):
        xp = jnp.where(row == 0, 0.0, pltpu.roll(x, 1, 0))
        xn = jnp.where(row == L - 1, 0.0, pltpu.roll(x, L - 1, 0))
        return _silu(w[0:1, :] * xp + w[1:2, :] * x + w[2:3, :] * xn)

    qc_ref[...] = conv_silu(q_ref[...], cwq_ref[...]).astype(BF16)
    kc_ref[...] = (conv_silu(k_ref[...], cwk_ref[...]) * (ML_DH ** -0.5)).astype(BF16)
    vc_ref[...] = v_ref[...].astype(BF16)

    bi_f = bias_ref[hd]
    bi_b = bias_ref[ML_HEADS + hd]
    bf_f = bias_ref[2 * ML_HEADS + hd]
    bf_b = bias_ref[3 * ML_HEADS + hd]

    cs_ref[...] = jnp.zeros_like(cs_ref)

    def gates(t0):
        gc = gc_ref[0, 0, pl.ds(t0, c), :]
        gr = gr_ref[0, 0, :, pl.ds(t0, c)]
        return gc, gr

    def stream(t0, slot, rev, bias_i, bias_f, n_state, m_state):
        s = _MlStream()
        gc, gr = gates(t0)
        gi, gf = (1, 3) if rev else (0, 2)
        s.qb = qc_ref[pl.ds(t0, c), :]
        s.kb = kc_ref[pl.ds(t0, c), :]
        s.vb = vc_ref[pl.ds(t0, c), :]
        s.f_col = _log_sigmoid(gc[:, gf:gf + 1] + bias_f)
        s.f_row = _log_sigmoid(gr[gf:gf + 1, :] + bias_f)
        s.i_col = gc[:, gi:gi + 1] + bias_i
        s.i_row = gr[gi:gi + 1, :] + bias_i
        s.c_state = cs_ref[slot]
        s.n_state = n_state
        s.m_state = m_state
        s.rev = rev
        return s

    def body(j, carry):
        n_f, m_f, n_b, m_b = carry
        t0 = pl.multiple_of(j * c, c)
        t1 = pl.multiple_of((nc - 1 - j) * c, c)
        fwd = stream(t0, 0, False, bi_f, bf_f, n_f, m_f)
        bwd = stream(t1, 1, True, bi_b, bf_b, n_b, m_b)
        _mlstm_chunks([fwd, bwd])
        cs_ref[0] = fwd.c_new
        hf_ref[pl.ds(t0, c), :] = fwd.h
        cs_ref[1] = bwd.c_new
        hb_ref[pl.ds(t1, c), :] = bwd.h
        return fwd.n_new, fwd.m_new, bwd.n_new, bwd.m_new

    zn = jnp.zeros((1, d), F32)
    m0 = jnp.full((1, 1), -1e30, F32)
    lax.fori_loop(0, nc, body, (zn, m0, zn, m0))

    hm = hf_ref[...] + hb_ref[...]
    mu = jnp.mean(hm, axis=1, keepdims=True)
    hc = hm - mu
    var = jnp.mean(hc * hc, axis=1, keepdims=True)
    y = hc * lax.rsqrt(var + EPS) * nw_ref[...] * _sigmoid(o_ref[...])
    out_ref[...] = y.astype(out_ref.dtype)


def _mlstm(proj, conv_w, gate_col, gate_row, bias, norm_w, B, L):
    d = ML_DH
    H = ML_HEADS
    grid_spec = pltpu.PrefetchScalarGridSpec(
        num_scalar_prefetch=1,
        grid=(B, H),
        in_specs=[
            pl.BlockSpec((L, d), lambda b, h, s: (b, h)),
            pl.BlockSpec((L, d), lambda b, h, s: (b, H + h)),
            pl.BlockSpec((L, d), lambda b, h, s: (b, 2 * H + h)),
            pl.BlockSpec((L, d), lambda b, h, s: (b, 3 * H + h)),
            pl.BlockSpec((3, d), lambda b, h, s: (0, h)),
            pl.BlockSpec((3, d), lambda b, h, s: (0, H + h)),
            pl.BlockSpec((1, 1, L, 4), lambda b, h, s: (b, h, 0, 0)),
            pl.BlockSpec((1, 1, 4, L), lambda b, h, s: (b, h, 0, 0)),
            pl.BlockSpec((1, d), lambda b, h, s: (0, h)),
        ],
        out_specs=pl.BlockSpec((L, d), lambda b, h, s: (b, h)),
        scratch_shapes=[
            pltpu.VMEM((L, d), BF16), pltpu.VMEM((L, d), BF16), pltpu.VMEM((L, d), BF16),
            pltpu.VMEM((L, d), F32), pltpu.VMEM((L, d), F32),
            pltpu.VMEM((2, d, d), F32),
        ],
    )
    return pl.pallas_call(
        _mlstm_kernel,
        grid_spec=grid_spec,
        out_shape=jax.ShapeDtypeStruct((B * L, ML_WIDTH), BF16),
        compiler_params=_cparams(("arbitrary", "arbitrary")),
        name="mlstm",
    )(bias, proj, proj, proj, proj, conv_w, conv_w, gate_col, gate_row, norm_w)


def _chunk_cumsum(x, rev):
    c, n = x.shape
    r = lax.broadcasted_iota(I32, (c, c), 0)
    s = lax.broadcasted_iota(I32, (c, c), 1)
    tri = jnp.where((s >= r) if rev else (s <= r), 1.0, 0.0).astype(BF16)
    hi = x.astype(BF16)
    r1 = x - hi.astype(F32)
    mid = r1.astype(BF16)
    lo = (r1 - mid.astype(F32)).astype(BF16)
    parts = _dot(tri, jnp.concatenate([hi, mid, lo], axis=1))
    return parts[:, :n] + parts[:, n:2 * n] + parts[:, 2 * n:]


class _HgStream:
    pass


def _hg_cumsum_stage(streams):
    for s in streams:
        s.b = _chunk_cumsum(s.logf, s.rev)
        s.vb = s.v.astype(BF16)
        s.o = jnp.sum(s.q * s.k, axis=1, keepdims=True) * s.v


def _hg_levels_stage(streams):
    c, dk = streams[0].q.shape
    row = lax.broadcasted_iota(I32, (c, dk), 0)
    tt = lax.broadcasted_iota(I32, (c, c), 0)
    ss = lax.broadcasted_iota(I32, (c, c), 1)

    for s in streams:
        strict = (ss > tt) if s.rev else (ss < tt)
        f_prev = pltpu.roll(s.f, 1, 0)
        f_next = pltpu.roll(s.f, c - 1, 0)
        amat = None
        m = c // 2
        while m >= 1:
            blk = 2 * m
            u = row & (blk - 1)
            tgt = (u < m) if s.rev else (u >= m)
            if m == 1:
                e = jnp.where(tgt, s.f, 1.0)
            elif m == 2:
                if s.rev:
                    e = jnp.where(u == 0, s.f * f_next, jnp.where(u == 1, s.f, jnp.where(u == 2, 1.0, f_prev)))
                else:
                    e = jnp.where(u == 0, f_next, jnp.where(u == 1, 1.0, jnp.where(u == 2, s.f, s.f * f_prev)))
            else:
                bb = s.b.reshape(c // blk, blk, dk)
                ref = bb[:, m:m + 1, :] if s.rev else bb[:, m - 1:m, :]
                e = jnp.exp(-jnp.abs(bb - ref)).reshape(c, dk)
            qk = (jnp.where(tgt, s.q, s.k) * e).astype(BF16)
            a = _dot_nt(qk, qk).astype(BF16)
            amat = a if amat is None else jnp.where((tt // blk) == (ss // blk), a, amat)
            m = m // 2
        s.amat = jnp.where(strict, amat, jnp.zeros_like(amat))
        s.edge = s.b[0:1, :] if s.rev else s.b[c - 1:c, :]
        s.qi = (s.q * jnp.exp(s.b)).astype(BF16)
        s.kl = (s.k * jnp.exp(s.edge - s.b)).astype(BF16)


def _hg_intra_stage(streams):
    for s in streams:
        s.o = s.o + _dot(s.amat, s.vb)


def _hg_state_stage(streams):
    for s in streams:
        st = s.get_state()
        s.o = s.o + _dot_nt(s.qi, st.astype(BF16))
        s.put_state(jnp.exp(s.edge) * st + _dot_tn(s.vb, s.kl))


def _hgrn2_kernel(q_ref, ff_ref, fb_ref, v_ref, g_ref, lb_ref, nw_ref, out_ref, of_ref, ob_ref, st_ref, qs_ref,
                  pb_ref, po_ref, pe_ref):
    L, dk = q_ref.shape
    c = HG_CHUNK
    nc = L // c

    lbp = lb_ref[...]
    mx = jnp.max(lbp, axis=0, keepdims=True)
    ex = jnp.exp(lbp - mx)
    lb = ex[0:1, :] / jnp.sum(ex, axis=0, keepdims=True)

    st_ref[...] = jnp.zeros_like(st_ref)
    qs_ref[...] = _silu(q_ref[...])

    per_step = HG_STREAM_CHUNKS
    n_steps = nc // per_step

    def shells(j):
        out = []
        for i in range(per_step):
            for rev in (False, True):
                s = _HgStream()
                chunk = (nc - 1 - j * per_step - i) if rev else (j * per_step + i)
                s.t0 = chunk * c if isinstance(chunk, int) else pl.multiple_of(chunk * c, c)
                s.rev = rev
                s.slot = 1 if rev else 0
                s.get_state = functools.partial(lambda slot: st_ref[slot], s.slot)
                s.put_state = functools.partial(st_ref.__setitem__, s.slot)
                out.append(s)
        return out

    def front(j):
        streams = shells(j)
        for s in streams:
            sig = _sigmoid((fb_ref if s.rev else ff_ref)[pl.ds(s.t0, c), :])
            s.f = lb + (1.0 - lb) * sig
            s.k = (1.0 - lb) * (1.0 - sig)
            s.logf = jnp.log(s.f)
            s.q = qs_ref[pl.ds(s.t0, c), :]
            s.v = v_ref[pl.ds(s.t0, c), :]
        return streams

    def save(streams):
        for i, s in enumerate(streams):
            pb_ref[i, 0] = s.amat
            pb_ref[i, 1] = s.vb
            pb_ref[i, 2] = s.qi
            pb_ref[i, 3] = s.kl
            po_ref[i] = s.o
            pe_ref[i] = jnp.broadcast_to(s.edge, (SUBLANES, dk))

    def load(j):
        streams = shells(j)
        for i, s in enumerate(streams):
            s.amat, s.vb, s.qi, s.kl = pb_ref[i, 0], pb_ref[i, 1], pb_ref[i, 2], pb_ref[i, 3]
            s.o = po_ref[i]
            s.edge = pe_ref[i, 0:1, :]
        return streams

    def emit(streams):
        for s in streams:
            (ob_ref if s.rev else of_ref)[pl.ds(s.t0, c), :] = s.o

    first = front(0)
    _hg_cumsum_stage(first)
    _hg_levels_stage(first)
    save(first)

    def body(j, carry):
        old = load(j - 1)
        new = front(j)
        _hg_intra_stage(old)
        _hg_state_stage(old[:2])
        _hg_cumsum_stage(new)
        _hg_state_stage(old[2:])
        _hg_levels_stage(new)
        emit(old)
        save(new)
        return carry

    lax.fori_loop(1, n_steps, body, 0)

    last = load(n_steps - 1)
    _hg_intra_stage(last)
    _hg_state_stage(last)
    emit(last)

    o = of_ref[...] + ob_ref[...]
    o = o * lax.rsqrt(jnp.mean(o * o, axis=1, keepdims=True) + EPS)
    y = o * nw_ref[...] * _silu(g_ref[...])
    out_ref[...] = y.astype(out_ref.dtype)


def _hgrn2(proj, hg_lb, norm_w, B, L):
    dk = HG_D
    H = HG_HEADS
    base = 4 * ML_WIDTH // dk
    return pl.pallas_call(
        _hgrn2_kernel,
        grid=(B, H),
        in_specs=[
            pl.BlockSpec((L, dk), lambda b, h: (b, base + h)),
            pl.BlockSpec((L, dk), lambda b, h: (b, base + H + h)),
            pl.BlockSpec((L, dk), lambda b, h: (b, base + 2 * H + h)),
            pl.BlockSpec((L, dk), lambda b, h: (b, base + 3 * H + h)),
            pl.BlockSpec((L, dk), lambda b, h: (b, base + 4 * H + h)),
            pl.BlockSpec((2, dk), lambda b, h: (0, h)),
            pl.BlockSpec((1, dk), lambda b, h: (0, h)),
        ],
        out_specs=pl.BlockSpec((L, dk), lambda b, h: (b, h)),
        out_shape=jax.ShapeDtypeStruct((B * L, HG_WIDTH), BF16),
        scratch_shapes=[
            pltpu.VMEM((L, dk), F32), pltpu.VMEM((L, dk), F32),
            pltpu.VMEM((2, dk, dk), F32),
            pltpu.VMEM((L, dk), F32),
            pltpu.VMEM((2 * HG_STREAM_CHUNKS, 4, HG_CHUNK, dk), BF16),
            pltpu.VMEM((2 * HG_STREAM_CHUNKS, HG_CHUNK, dk), F32),
            pltpu.VMEM((2 * HG_STREAM_CHUNKS, SUBLANES, dk), F32),
        ],
        compiler_params=_cparams(("arbitrary", "arbitrary")),
        name="hgrn2",
    )(proj, proj, proj, proj, proj, hg_lb, norm_w)


def _outproj_kernel(x_ref, ml_ref, hg_ref, eg_ref, eb_ref, wo_ref, g1_ref, b1_ref, wrh_ref, wrl_ref,
                    x1_ref, lgt_ref, ya_ref, yb_ref):
    i = pl.program_id(0)
    half = ml_ref.shape[1]
    n_exp = lgt_ref.shape[0]

    @pl.when(i == 0)
    def _():
        yb_ref[...] = jnp.zeros_like(yb_ref)

    def step(y_new, y_old):
        mix = _dot(ml_ref[...], wo_ref[0:half, :]) + _dot(hg_ref[...], wo_ref[half:, :])
        y_new[...] = ALPHA * _layer_norm(x_ref[...], eg_ref[...], eb_ref[...]) + mix
        x1 = _layer_norm(y_old[...], g1_ref[...], b1_ref[...])
        x1_ref[...] = x1
        xh, xl = _split_bf16(x1)
        lgt = _dot_nt(wrh_ref[...], xh) + _dot_nt(wrh_ref[...], xl) + _dot_nt(wrl_ref[...], xh)
        lgt_ref[...] = lgt[0:n_exp, :]

    @pl.when(lax.rem(i, 2) == 0)
    def _():
        step(ya_ref, yb_ref)

    @pl.when(lax.rem(i, 2) == 1)
    def _():
        step(yb_ref, ya_ref)


def _outproj(x, ml, hg, eg, eb, wo, g1, b1, wrt_hi, wrt_lo, n_exp, tm=256):
    n, d = x.shape
    n_tiles = n // tm
    row = lambda i: (jnp.minimum(i, n_tiles - 1), 0)
    prev_row = lambda i: (jnp.maximum(i - 1, 0), 0)
    fixed = lambda i: (0, 0)
    return pl.pallas_call(
        _outproj_kernel,
        grid=(n_tiles + 1,),
        in_specs=[
            pl.BlockSpec((tm, d), row),
            pl.BlockSpec((tm, ML_WIDTH), row),
            pl.BlockSpec((tm, HG_WIDTH), row),
            pl.BlockSpec((1, d), fixed),
            pl.BlockSpec((1, d), fixed),
            pl.BlockSpec((d, d), fixed),
            pl.BlockSpec((1, d), fixed),
            pl.BlockSpec((1, d), fixed),
            pl.BlockSpec((LANES, d), fixed),
            pl.BlockSpec((LANES, d), fixed),
        ],
        out_specs=[pl.BlockSpec((tm, d), prev_row),
                   pl.BlockSpec((n_exp, tm), lambda i: (0, jnp.maximum(i - 1, 0)))],
        out_shape=[jax.ShapeDtypeStruct((n, d), F32), jax.ShapeDtypeStruct((n_exp, n), F32)],
        scratch_shapes=[pltpu.VMEM((tm, d), F32), pltpu.VMEM((tm, d), F32)],
        compiler_params=_cparams(("arbitrary",)),
        name="outproj",
    )(x, ml, hg, eg, eb, wo, g1, b1, wrt_hi, wrt_lo)


def _pad_cols(w, width=LANES):
    return jnp.pad(w, ((0, 0), (0, width - w.shape[1])))


def _token_mixer_stage(x, emb_ln_g, emb_ln_b, w_in, conv_w, ml_igate_b, ml_fgate_b, ml_norm_w,
                       hg_lb, hg_norm_w, w_out, ln1_g, ln1_b, w_router, B, L):
    d = D_MODEL
    g0 = 4 * ML_WIDTH
    w_main = jnp.concatenate([w_in[:, :g0], w_in[:, g0 + 16:]], axis=1).astype(BF16)
    wg_both = jnp.concatenate(_split_bf16(_pad_cols(w_in[:, g0:g0 + 16])), axis=1)
    eg = emb_ln_g.reshape(1, d)
    eb = emb_ln_b.reshape(1, d)
    proj, gates = _ln_proj(x, eg, eb, w_main, wg_both)

    g4 = gates[:, :16].reshape(B, L, 4, ML_HEADS)
    gate_col = g4.transpose(0, 3, 1, 2)
    gate_row = g4.transpose(0, 3, 2, 1)
    bias = jnp.concatenate([ml_igate_b[0], ml_igate_b[1], ml_fgate_b[0], ml_fgate_b[1]]).astype(F32)
    ml = _mlstm(proj, conv_w, gate_col, gate_row, bias, ml_norm_w.reshape(1, ML_WIDTH), B, L)
    hg = _hgrn2(proj, hg_lb, hg_norm_w.reshape(1, HG_WIDTH), B, L)

    wrt_hi, wrt_lo = _split_bf16(_pad_cols(w_router).T)
    return _outproj(x, ml, hg, eg, eb, w_out.astype(BF16), ln1_g.reshape(1, d), ln1_b.reshape(1, d),
                    wrt_hi, wrt_lo, w_router.shape[1])


def _excl_token_cumsum(mask):
    E, R, ln = mask.shape
    mf = jnp.where(mask, 1.0, 0.0)
    mb = mf.astype(BF16)
    upper = jnp.where(lax.broadcasted_iota(I32, (ln, ln), 0) <= lax.broadcasted_iota(I32, (ln, ln), 1),
                      1.0, 0.0).astype(BF16)
    lower = jnp.where(lax.broadcasted_iota(I32, (R, R), 1) < lax.broadcasted_iota(I32, (R, R), 0),
                      1.0, 0.0).astype(BF16)
    ones = jnp.ones((ln, ln), BF16)
    within = _dot(mb.reshape(E * R, ln), upper).reshape(E, R, ln)
    rows = jnp.stack([_dot(_dot(lower, mb[e]).astype(BF16), ones) for e in range(E)], axis=0)
    return within - mf + rows


def _select_kernel(lg_ref, pos_ref, posm_ref, wts_ref, *, cap):
    E = lg_ref.shape[0]
    lg = lg_ref[...]
    mx = jnp.max(lg, axis=0, keepdims=True)
    ex = jnp.exp(lg - mx)
    aff = ex / jnp.sum(ex, axis=0, keepdims=True)

    def count(mask):
        ones = jnp.where(mask, 1.0, 0.0)
        return jnp.sum(jnp.sum(ones, axis=2, keepdims=True), axis=1, keepdims=True)

    def body(i, tbits):
        cand = tbits | lax.shift_left(jnp.int32(1), 30 - i)
        cnt = count(aff >= lax.bitcast_convert_type(cand, F32))
        return jnp.where(cnt >= cap, cand, tbits)

    tbits = lax.fori_loop(0, 31, body, jnp.zeros((E, 1, 1), I32))
    thr = lax.bitcast_convert_type(tbits, F32)
    nxt = lax.bitcast_convert_type(tbits + 1, F32)
    above = aff >= nxt
    band = jnp.logical_and(aff >= thr, jnp.logical_not(above))
    need = cap - count(above)
    sel = jnp.logical_or(above, jnp.logical_and(band, _excl_token_cumsum(band) < need))
    pos = _excl_token_cumsum(sel).astype(I32)
    pos_ref[...] = pos
    posm_ref[...] = jnp.where(sel, pos, -1)
    wts_ref[...] = jnp.where(sel, aff, 0.0)


def _select(lg_t, cap):
    E, R, ln = lg_t.shape
    full = pl.BlockSpec((E, R, ln), lambda i: (0, 0, 0))
    return pl.pallas_call(
        functools.partial(_select_kernel, cap=cap),
        grid=(1,),
        in_specs=[full],
        out_specs=[full, full, full],
        out_shape=[jax.ShapeDtypeStruct((E, R, ln), I32), jax.ShapeDtypeStruct((E, R, ln), I32),
                   jax.ShapeDtypeStruct((E, R, ln), F32)],
        compiler_params=_cparams(("arbitrary",)),
        name="select",
    )(lg_t)


TOK_TILE = 256
SLOT_CHUNK = 64


def _pack_bf16_pairs(x):
    h = x.shape[1] // 2
    lo = lax.shift_right_logical(lax.bitcast_convert_type(x[:, :h], U32), jnp.uint32(16))
    hi = lax.bitcast_convert_type(x[:, h:], U32) & jnp.uint32(0xFFFF0000)
    return hi | lo


def _unpack_bf16_pairs(w):
    lo = lax.bitcast_convert_type(lax.shift_left(w, jnp.uint32(16)), F32).astype(BF16)
    hi = lax.bitcast_convert_type(w & jnp.uint32(0xFFFF0000), F32).astype(BF16)
    return lo, hi


def _dispatch_kernel(off_ref, base_ref, cnt_ref, xa_ref, xb_ref, pos_ref, xe_ref,
                     x16_ref, res_ref, stage_ref, ostage_ref, carry_ref, sem, osem, *, tiles_a, cap_total, pad):
    E = N_EXPERTS
    CH = SLOT_CHUNK
    SUB = SUBLANES
    tc = xa_ref.shape[0]
    t = pl.program_id(0)
    nt = pl.num_programs(0)
    slot = lax.rem(t, 2)

    @pl.when(t < tiles_a)
    def _():
        x16_ref[...] = xa_ref[...].astype(BF16)

    @pl.when(t >= tiles_a)
    def _():
        x16_ref[...] = xb_ref[...].astype(BF16)

    xb = x16_ref[...]

    @pl.when(t == 0)
    def _():
        carry_ref[...] = jnp.zeros_like(carry_ref)

    base = [base_ref[t * E + e] for e in range(E)]
    cnt = [cnt_ref[t * E + e] for e in range(E)]
    al = [pl.multiple_of((b // SUB) * SUB, SUB) for b in base]
    first = [off_ref[t * E + e] - (base[e] - al[e]) for e in range(E)]

    def onehot(e, start, rows):
        kio = lax.broadcasted_iota(I32, (rows, tc), 0)
        rel = pos_ref[e:e + 1, :] - (first[e] + start)
        return jnp.where(rel == kio, 1.0, 0.0).astype(BF16)

    ot = jnp.concatenate([onehot(e, 0, CH) for e in range(E)], axis=0)
    res_ref[...] = _dot(ot, xb)
    for e in range(E):
        res_ref[e * CH:e * CH + SUB, :] += carry_ref[e * SUB:(e + 1) * SUB, :]
    stage_ref[slot] = _pack_bf16_pairs(res_ref[...])

    nxt = [((base[e] + cnt[e]) // SUB) * SUB - al[e] for e in range(E)]
    oc = jnp.concatenate([onehot(e, nxt[e], SUB) for e in range(E)], axis=0)
    new_carry = _dot(oc, xb)
    for e in range(E):
        keep = jnp.where(nxt[e] == 0, 1.0, 0.0)
        carry_ref[e * SUB:(e + 1) * SUB, :] = new_carry[e * SUB:(e + 1) * SUB, :] + keep * carry_ref[e * SUB:(e + 1) * SUB, :]

    def main_copy(s, e, row):
        return pltpu.make_async_copy(stage_ref.at[s, pl.ds(e * CH, CH)], xe_ref.at[pl.ds(row, CH)], sem.at[s])

    @pl.when(t > 0)
    def _():
        for e in range(E):
            main_copy(1 - slot, e, 0).wait()

    for e in range(E):
        main_copy(slot, e, al[e]).start()

    for e in range(E):
        nch = (base[e] - al[e] + cnt[e] + CH - 1) // CH

        def body(c, carry, e=e):
            ostage_ref[...] = _pack_bf16_pairs(_dot(onehot(e, c * CH, CH), xb))
            row = pl.multiple_of(al[e] + c * CH, SUB)
            cp = pltpu.make_async_copy(ostage_ref, xe_ref.at[pl.ds(row, CH)], osem)
            cp.start()
            cp.wait()
            return carry

        lax.fori_loop(1, nch, body, 0)

    @pl.when(t == nt - 1)
    def _():
        for e in range(E):
            main_copy(slot, e, 0).wait()
        ostage_ref[...] = jnp.zeros_like(ostage_ref)
        fills = [pltpu.make_async_copy(ostage_ref, xe_ref.at[pl.ds(e * (cap_total + pad) + cap_total + j * CH, CH)], osem)
                 for e in range(E) for j in range(pad // CH)]
        for cp in fills:
            cp.start()
        for cp in fills:
            cp.wait()


def _dispatch(off, base, cnt, x1_a, x1_b, pos_rows, cap_total, pad):
    d = x1_a.shape[1]
    E = N_EXPERTS
    tc = TOK_TILE
    tiles_a = x1_a.shape[0] // tc
    tiles_b = x1_b.shape[0] // tc
    grid_spec = pltpu.PrefetchScalarGridSpec(
        num_scalar_prefetch=3,
        grid=(tiles_a + tiles_b,),
        in_specs=[
            pl.BlockSpec((tc, d), lambda i, *_: (jnp.minimum(i, tiles_a - 1), 0)),
            pl.BlockSpec((tc, d), lambda i, *_: (jnp.maximum(i - tiles_a, 0), 0)),
            pl.BlockSpec((E, tc), lambda i, *_: (0, i)),
        ],
        out_specs=pl.BlockSpec(memory_space=pl.ANY),
        scratch_shapes=[
            pltpu.VMEM((tc, d), BF16),
            pltpu.VMEM((E * SLOT_CHUNK, d), F32),
            pltpu.VMEM((2, E * SLOT_CHUNK, d // 2), U32),
            pltpu.VMEM((SLOT_CHUNK, d // 2), U32),
            pltpu.VMEM((E * SUBLANES, d), F32),
            pltpu.SemaphoreType.DMA((2,)),
            pltpu.SemaphoreType.DMA(()),
        ],
    )
    return pl.pallas_call(
        functools.partial(_dispatch_kernel, tiles_a=tiles_a, cap_total=cap_total, pad=pad),
        grid_spec=grid_spec,
        out_shape=jax.ShapeDtypeStruct((E * (cap_total + pad), d // 2), U32),
        compiler_params=_cparams(("arbitrary",)),
        name="dispatch",
    )(off, base, cnt, x1_a, x1_b, pos_rows)


def _ffn_kernel(xe_ref, wg_ref, wu_ref, wd_ref, out_ref, xb_ref, acc_ref):
    f = pl.program_id(2)

    @pl.when(f == 0)
    def _():
        h = xe_ref.shape[2]
        lo, hi = _unpack_bf16_pairs(xe_ref[0])
        xb_ref[:, :h] = lo
        xb_ref[:, h:] = hi
        acc_ref[...] = jnp.zeros_like(acc_ref)

    xb = xb_ref[...]
    hid = (_silu(_dot(xb, wg_ref[0])) * _dot(xb, wu_ref[0])).astype(BF16)
    acc_ref[...] += _dot(hid, wd_ref[0])

    @pl.when(f == pl.num_programs(2) - 1)
    def _():
        out_ref[0] = acc_ref[...].astype(out_ref.dtype)


def _ffn(xe, wg, wu, wd, cap_total, tf=512):
    E, _, dh = xe.shape
    d = 2 * dh
    ff = wg.shape[2]
    tm = next(t for t in (1024, 512, 256, 128) if cap_total % t == 0)
    return pl.pallas_call(
        _ffn_kernel,
        grid=(E, cap_total // tm, ff // tf),
        in_specs=[
            pl.BlockSpec((1, tm, dh), lambda e, i, f: (e, i, 0)),
            pl.BlockSpec((1, d, tf), lambda e, i, f: (e, 0, f)),
            pl.BlockSpec((1, d, tf), lambda e, i, f: (e, 0, f)),
            pl.BlockSpec((1, tf, d), lambda e, i, f: (e, f, 0)),
        ],
        out_specs=pl.BlockSpec((1, tm, d), lambda e, i, f: (e, i, 0)),
        out_shape=jax.ShapeDtypeStruct((E, cap_total, d), BF16),
        scratch_shapes=[pltpu.VMEM((tm, d), BF16), pltpu.VMEM((tm, d), F32)],
        compiler_params=_cparams(("arbitrary", "arbitrary", "arbitrary")),
        name="ffn",
    )(xe, wg, wu, wd)


def _combine_kernel(base_ref, cnt_ref, x1_ref, grow_ref, w_ref, g2_ref, b2_ref, ye_ref, out_ref,
                    buf_ref, obuf_ref, acc_ref, sem, osem, *, rows_total):
    E = N_EXPERTS
    CH = SLOT_CHUNK
    tc = x1_ref.shape[0]
    t = pl.program_id(0)
    nt = pl.num_programs(0)
    slot = lax.rem(t, 2)
    last = rows_total - CH
    ALIGN = 2 * SUBLANES

    def window(step, e):
        al = (base_ref[step * E + e] // ALIGN) * ALIGN
        return al, pl.multiple_of(jnp.minimum(al, last), ALIGN)

    def chunk_copy(s, e, start):
        return pltpu.make_async_copy(ye_ref.at[pl.ds(start, CH)], buf_ref.at[s, pl.ds(e * CH, CH)], sem.at[s, e])

    @pl.when(t == 0)
    def _():
        for e in range(E):
            chunk_copy(0, e, window(0, e)[1]).start()

    @pl.when(t + 1 < nt)
    def _():
        for e in range(E):
            chunk_copy(1 - slot, e, window(t + 1, e)[1]).start()

    kio = lax.broadcasted_iota(I32, (CH, tc), 0)

    def weights(e, start, lo):
        grow = grow_ref[e:e + 1, :]
        hit = jnp.logical_and(grow - start == kio, grow >= lo)
        return jnp.where(hit, w_ref[e:e + 1, :], 0.0).astype(BF16)

    at = jnp.concatenate([weights(e, window(t, e)[1], 0) for e in range(E)], axis=0)
    for e in range(E):
        chunk_copy(slot, e, 0).wait()
    acc_ref[...] = ALPHA * x1_ref[...] + _dot_tn(at, buf_ref[slot])

    for e in range(E):
        al = window(t, e)[0]
        nch = (base_ref[t * E + e] - al + cnt_ref[t * E + e] + CH - 1) // CH

        def body(c, carry, e=e, al=al):
            lo = al + c * CH
            start = pl.multiple_of(jnp.minimum(lo, last), ALIGN)
            cp = pltpu.make_async_copy(ye_ref.at[pl.ds(start, CH)], obuf_ref, osem)
            cp.start()
            cp.wait()
            acc_ref[...] += _dot_tn(weights(e, start, lo), obuf_ref[...])
            return carry

        lax.fori_loop(1, nch, body, 0)

    out_ref[...] = _layer_norm(acc_ref[...], g2_ref[...], b2_ref[...])


def _combine(base, cnt, x1, grow_rows, w_rows, g2, b2, ye, rows_total):
    n, d = x1.shape
    E = N_EXPERTS
    tc = TOK_TILE
    grid_spec = pltpu.PrefetchScalarGridSpec(
        num_scalar_prefetch=2,
        grid=(n // tc,),
        in_specs=[
            pl.BlockSpec((tc, d), lambda i, *_: (i, 0)),
            pl.BlockSpec((E, tc), lambda i, *_: (0, i)),
            pl.BlockSpec((E, tc), lambda i, *_: (0, i)),
            pl.BlockSpec((1, d), lambda i, *_: (0, 0)),
            pl.BlockSpec((1, d), lambda i, *_: (0, 0)),
            pl.BlockSpec(memory_space=pl.ANY),
        ],
        out_specs=pl.BlockSpec((tc, d), lambda i, *_: (i, 0)),
        scratch_shapes=[
            pltpu.VMEM((2, E * SLOT_CHUNK, d), BF16),
            pltpu.VMEM((SLOT_CHUNK, d), BF16),
            pltpu.VMEM((tc, d), F32),
            pltpu.SemaphoreType.DMA((2, E)),
            pltpu.SemaphoreType.DMA(()),
        ],
    )
    return pl.pallas_call(
        functools.partial(_combine_kernel, rows_total=rows_total),
        grid_spec=grid_spec,
        out_shape=jax.ShapeDtypeStruct((n, d), F32),
        compiler_params=_cparams(("arbitrary",)),
        name="combine",
    )(base, cnt, x1, grow_rows, w_rows, g2, b2, ye)


def _routing_tables(pos, posm, wts, cap, slot0, cap_total, pad):
    E = N_EXPERTS
    n = pos.shape[1] * pos.shape[2]
    tiles = n // TOK_TILE
    eidx = jnp.arange(E, dtype=I32)
    off = pos.reshape(E, n)[:, ::TOK_TILE].T
    cnt = jnp.concatenate([off[1:], jnp.full((1, E), cap, I32)], axis=0) - off
    base_x = off + eidx[None, :] * (cap_total + pad) + slot0
    base_y = off + eidx[None, :] * cap_total + slot0
    pos_rows = posm.reshape(E, n)
    grow_rows = jnp.where(pos_rows >= 0, pos_rows + eidx[:, None] * cap_total + slot0, -1)
    flat = lambda a: a.reshape(tiles * E)
    return flat(off), flat(base_x), flat(base_y), flat(cnt), pos_rows, grow_rows, wts.reshape(E, n)


def kernel(x_prompt, x_sample, emb_ln_g, emb_ln_b, w_in, conv_w, ml_igate_b, ml_fgate_b, ml_norm_w, hg_lb, hg_norm_w, w_out, ln1_g, ln1_b, w_router, w_gate, w_up, w_down, ln2_g, ln2_b):
    E = N_EXPERTS
    d = D_MODEL
    groups = (x_prompt, x_sample)
    caps = [EC_FACTOR * x.shape[0] * x.shape[1] // E for x in groups]
    cap_total = sum(caps)
    pad = 2 * SLOT_CHUNK
    assert all(c % (2 * SUBLANES) == 0 for c in caps)

    staged = []
    tables = []
    slot0 = 0
    for x, cap in zip(groups, caps):
        B, L, _ = x.shape
        n = B * L
        x1, lg = _token_mixer_stage(x.reshape(n, d), emb_ln_g, emb_ln_b, w_in[0], conv_w[0], ml_igate_b[0],
                                    ml_fgate_b[0], ml_norm_w[0], hg_lb, hg_norm_w[0], w_out[0], ln1_g[0],
                                    ln1_b[0], w_router[0], B, L)
        lg_t = lg.reshape(E, n // LANES, LANES)
        pos, posm, wts = _select(lg_t, cap)
        off, base_x, base_y, cnt, pos_rows, grow_t, w_t = _routing_tables(pos, posm, wts, cap, slot0, cap_total, pad)
        tables.append((off, base_x, cnt, pos_rows))
        staged.append((x1, base_y, cnt, grow_t, w_t, (B, L)))
        slot0 += cap

    off, base_x, cnt, pos_rows = (jnp.concatenate(parts, axis=-1) for parts in zip(*tables))
    xe = _dispatch(off, base_x, cnt, staged[0][0], staged[1][0], pos_rows, cap_total, pad)

    ye = _ffn(xe.reshape(E, cap_total + pad, d // 2), w_gate[0].astype(BF16), w_up[0].astype(BF16),
              w_down[0].astype(BF16), cap_total)
    ye = ye.reshape(E * cap_total, d)

    g2 = ln2_g[0].reshape(1, d)
    b2 = ln2_b[0].reshape(1, d)
    outs = []
    for x1, base_y, cnt, grow_t, w_t, (B, L) in staged:
        y = _combine(base_y, cnt, x1, grow_t, w_t, g2, b2, ye, E * cap_total)
        outs.append(y.reshape(B, L, d))
    return tuple(outs)
```

```python
import functools

import jax
import jax.numpy as jnp
from jax import lax
from jax.experimental import pallas as pl
from jax.experimental.pallas import tpu as pltpu

F32 = jnp.float32
BF16 = jnp.bfloat16
I32 = jnp.int32
U32 = jnp.uint32

D_MODEL = 2048
ML_WIDTH = 1024
ML_HEADS = 4
ML_DH = 256
HG_WIDTH = 1024
HG_HEADS = 8
HG_D = 128
N_EXPERTS = 16
EC_FACTOR = 2
EXPERT_FF = 2048
DEPTH = 1
ALPHA = (2.0 * DEPTH) ** 0.25
EPS = 1e-5

LANES = 128
SUBLANES = 8
N_GATE_COLS = 16
VMEM_LIMIT = 56 * 1024 * 1024

ML_CHUNK = 256
HG_CHUNK = 128
HG_STREAM_CHUNKS = 2


def _cparams(sem):
    return pltpu.CompilerParams(dimension_semantics=sem, vmem_limit_bytes=VMEM_LIMIT)


def _split_bf16(x):
    hi = x.astype(BF16)
    lo = (x - hi.astype(F32)).astype(BF16)
    return hi, lo


def _layer_norm(x, g, b):
    mu = jnp.mean(x, axis=-1, keepdims=True)
    xc = x - mu
    var = jnp.mean(xc * xc, axis=-1, keepdims=True)
    return xc * lax.rsqrt(var + EPS) * g + b


def _sigmoid(x):
    return 1.0 / (1.0 + jnp.exp(-x))


def _silu(x):
    return x * _sigmoid(x)


def _log_sigmoid(x):
    return jnp.minimum(x, 0.0) - jnp.log1p(jnp.exp(-jnp.abs(x)))


def _dot(a, b):
    return jnp.dot(a, b, preferred_element_type=F32)


def _dot_nt(a, b):
    return lax.dot_general(a, b, (((1,), (1,)), ((), ())), preferred_element_type=F32)


def _dot_tn(a, b):
    return lax.dot_general(a, b, (((0,), (0,)), ((), ())), preferred_element_type=F32)


def _ln_proj_kernel(x_ref, g_ref, b_ref, w_ref, wg_ref, proj_ref, gate_ref, gatet_ref,
                    hha_ref, hla_ref, hhb_ref, hlb_ref, *, n_slices):
    i = pl.program_id(0)
    j = pl.program_id(1)
    rs = x_ref.shape[0]

    @pl.when(jnp.logical_and(i == 0, j == 0))
    def _():
        hhb_ref[...] = jnp.zeros_like(hhb_ref)
        hlb_ref[...] = jnp.zeros_like(hlb_ref)

    def step(hh_new, hl_new, hh_old, hl_old):
        @pl.when(j == 0)
        def _():
            both = _dot(hh_old[...], wg_ref[...])
            gates = both[:, :LANES] + both[:, LANES:] + _dot(hl_old[...], wg_ref[:, :LANES])
            gate_ref[...] = gates
            gatet_ref[...] = gates.T[0:gatet_ref.shape[0], :]

        proj_ref[...] = _dot(hh_old[...], w_ref[...]).astype(proj_ref.dtype)
        r0 = pl.multiple_of(jnp.minimum(j, n_slices - 1) * rs, rs)
        hh, hl = _split_bf16(_layer_norm(x_ref[...], g_ref[...], b_ref[...]))
        hh_new[pl.ds(r0, rs), :] = hh
        hl_new[pl.ds(r0, rs), :] = hl

    @pl.when(lax.rem(i, 2) == 0)
    def _():
        step(hha_ref, hla_ref, hhb_ref, hlb_ref)

    @pl.when(lax.rem(i, 2) == 1)
    def _():
        step(hhb_ref, hlb_ref, hha_ref, hla_ref)


def _ln_proj(x, g, b, w_main, wg_both, tm=1024, tn=1024):
    n, d = x.shape
    tm = min(tm, n)
    ncol = w_main.shape[1] // tn
    n_tiles = n // tm
    n_slices = ncol - 1
    rs = tm // n_slices
    assert tm % n_slices == 0 and rs % SUBLANES == 0
    prev_row = lambda i, j: jnp.maximum(i - 1, 0)
    return pl.pallas_call(
        functools.partial(_ln_proj_kernel, n_slices=n_slices),
        grid=(n_tiles + 1, ncol),
        in_specs=[
            pl.BlockSpec((rs, d), lambda i, j: (jnp.minimum(i, n_tiles - 1) * n_slices
                                                + jnp.minimum(j, n_slices - 1), 0)),
            pl.BlockSpec((1, d), lambda i, j: (0, 0)),
            pl.BlockSpec((1, d), lambda i, j: (0, 0)),
            pl.BlockSpec((d, tn), lambda i, j: (0, j)),
            pl.BlockSpec((d, 2 * LANES), lambda i, j: (0, 0)),
        ],
        out_specs=[
            pl.BlockSpec((tm, tn), lambda i, j: (prev_row(i, j), jnp.where(i > 0, j, 0))),
            pl.BlockSpec((tm, LANES), lambda i, j: (prev_row(i, j), 0)),
            pl.BlockSpec((N_GATE_COLS, tm), lambda i, j: (0, prev_row(i, j))),
        ],
        out_shape=[
            jax.ShapeDtypeStruct((n, ncol * tn), F32),
            jax.ShapeDtypeStruct((n, LANES), F32),
            jax.ShapeDtypeStruct((N_GATE_COLS, n), F32),
        ],
        scratch_shapes=[pltpu.VMEM((tm, d), BF16)] * 4,
        compiler_params=_cparams(("arbitrary", "arbitrary")),
        name="ln_proj",
    )(x, g, b, w_main, wg_both)


class _MlStream:
    pass


def _mlstm_chunks(streams):
    c = streams[0].qb.shape[0]
    tt = lax.broadcasted_iota(I32, (c, c), 0)
    ss = lax.broadcasted_iota(I32, (c, c), 1)

    for s in streams:
        causal = (ss >= tt) if s.rev else (ss <= tt)
        b_col = jnp.sum(jnp.where(causal, s.f_row, 0.0), axis=1, keepdims=True)
        anti = (tt >= ss) if s.rev else (tt <= ss)
        b_row = jnp.sum(jnp.where(anti, s.f_col, 0.0), axis=0, keepdims=True)
        total = jnp.sum(s.f_row, axis=1, keepdims=True)

        dmat = jnp.where(causal, b_col - b_row + s.i_row, -jnp.inf)
        a_inter = b_col + s.m_state
        s.m_t = jnp.maximum(a_inter, jnp.max(dmat, axis=1, keepdims=True))
        s.w_inter = jnp.exp(a_inter - s.m_t)
        s.p = jnp.exp(dmat - s.m_t)
        s.qk = _dot_nt(s.qb, s.kb)
        s.qc = _dot(s.qb, s.c_state.astype(BF16))

        g_col = total - b_col + s.i_col
        g_row = total - b_row + s.i_row
        s.m_new = jnp.maximum(total + s.m_state, jnp.max(g_row, axis=1, keepdims=True))
        s.decay = jnp.exp(total + s.m_state - s.m_new)
        ks = s.kb.astype(F32) * jnp.exp(g_col - s.m_new)
        s.kv = _dot_tn(ks.astype(BF16), s.vb)
        s.n_new = s.decay * s.n_state + jnp.sum(ks, axis=0, keepdims=True)

    for s in streams:
        s.s = s.qk * s.p
        s.sv = _dot(s.s.astype(BF16), s.vb)

    for s in streams:
        num = s.w_inter * s.qc + s.sv
        qn = jnp.sum(s.qb.astype(F32) * s.n_state, axis=1, keepdims=True)
        den = s.w_inter * qn + jnp.sum(s.s, axis=1, keepdims=True)
        s.h = num / jnp.maximum(jnp.abs(den), jnp.exp(-s.m_t))
        s.c_new = s.decay * s.c_state + s.kv


def _mlstm_kernel(bias_ref, q_ref, k_ref, v_ref, o_ref, cwq_ref, cwk_ref, gt_ref, gr_ref, nw_ref,
                  out_ref, qc_ref, kc_ref, vc_ref, hf_ref, hb_ref, cs_ref):
    L, d = q_ref.shape

    lane_shift = lax.rem(LANES - 4 * pl.program_id(1), LANES)
    c = ML_CHUNK
    nc = L // c
    hd = pl.program_id(1)

    row = lax.broadcasted_iota(I32, (L, 1), 0)

    def conv_silu(x, w):
        xp = jnp.where(row == 0, 0.0, pltpu.roll(x, 1, 0))
        xn = jnp.where(row == L - 1, 0.0, pltpu.roll(x, L - 1, 0))
        return _silu(w[0:1, :] * xp + w[1:2, :] * x + w[2:3, :] * xn)

    qc_ref[...] = conv_silu(q_ref[...], cwq_ref[...]).astype(BF16)
    kc_ref[...] = (conv_silu(k_ref[...], cwk_ref[...]) * (ML_DH ** -0.5)).astype(BF16)
    vc_ref[...] = v_ref[...].astype(BF16)

    bi_f = bias_ref[hd]
    bi_b = bias_ref[ML_HEADS + hd]
    bf_f = bias_ref[2 * ML_HEADS + hd]
    bf_b = bias_ref[3 * ML_HEADS + hd]

    cs_ref[...] = jnp.zeros_like(cs_ref)

    def gates(t0):
        gc = pltpu.roll(gt_ref[pl.ds(t0, c), :], lane_shift, 1)[:, 0:4]
        gr = gr_ref[0, 0, :, pl.ds(t0, c)]
        return gc, gr

    def stream(t0, slot, rev, bias_i, bias_f, n_state, m_state):
        s = _MlStream()
        gc, gr = gates(t0)
        gi, gf = (1, 3) if rev else (0, 2)
        s.qb = qc_ref[pl.ds(t0, c), :]
        s.kb = kc_ref[pl.ds(t0, c), :]
        s.vb = vc_ref[pl.ds(t0, c), :]
        s.f_col = _log_sigmoid(gc[:, gf:gf + 1] + bias_f)
        s.f_row = _log_sigmoid(gr[gf:gf + 1, :] + bias_f)
        s.i_col = gc[:, gi:gi + 1] + bias_i
        s.i_row = gr[gi:gi + 1, :] + bias_i
        s.c_state = cs_ref[slot]
        s.n_state = n_state
        s.m_state = m_state
        s.rev = rev
        return s

    def body(j, carry):
        n_f, m_f, n_b, m_b = carry
        t0 = pl.multiple_of(j * c, c)
        t1 = pl.multiple_of((nc - 1 - j) * c, c)
        fwd = stream(t0, 0, False, bi_f, bf_f, n_f, m_f)
        bwd = stream(t1, 1, True, bi_b, bf_b, n_b, m_b)
        _mlstm_chunks([fwd, bwd])
        cs_ref[0] = fwd.c_new
        hf_ref[pl.ds(t0, c), :] = fwd.h
        cs_ref[1] = bwd.c_new
        hb_ref[pl.ds(t1, c), :] = bwd.h
        return fwd.n_new, fwd.m_new, bwd.n_new, bwd.m_new

    zn = jnp.zeros((1, d), F32)
    m0 = jnp.full((1, 1), -1e30, F32)
    lax.fori_loop(0, nc, body, (zn, m0, zn, m0))

    hm = hf_ref[...] + hb_ref[...]
    mu = jnp.mean(hm, axis=1, keepdims=True)
    hc = hm - mu
    var = jnp.mean(hc * hc, axis=1, keepdims=True)
    y = hc * lax.rsqrt(var + EPS) * nw_ref[...] * _sigmoid(o_ref[...])
    out_ref[...] = y.astype(out_ref.dtype)


def _mlstm(proj, conv_w, gate_tok, gate_row, bias, norm_w, B, L):
    d = ML_DH
    H = ML_HEADS
    grid_spec = pltpu.PrefetchScalarGridSpec(
        num_scalar_prefetch=1,
        grid=(B, H),
        in_specs=[
            pl.BlockSpec((L, d), lambda b, h, s: (b, h)),
            pl.BlockSpec((L, d), lambda b, h, s: (b, H + h)),
            pl.BlockSpec((L, d), lambda b, h, s: (b, 2 * H + h)),
            pl.BlockSpec((L, d), lambda b, h, s: (b, 3 * H + h)),
            pl.BlockSpec((3, d), lambda b, h, s: (0, h)),
            pl.BlockSpec((3, d), lambda b, h, s: (0, H + h)),
            pl.BlockSpec((L, LANES), lambda b, h, s: (b, 0)),
            pl.BlockSpec((1, 1, 4, L), lambda b, h, s: (b, h, 0, 0)),
            pl.BlockSpec((1, d), lambda b, h, s: (0, h)),
        ],
        out_specs=pl.BlockSpec((L, d), lambda b, h, s: (b, h)),
        scratch_shapes=[
            pltpu.VMEM((L, d), BF16), pltpu.VMEM((L, d), BF16), pltpu.VMEM((L, d), BF16),
            pltpu.VMEM((L, d), F32), pltpu.VMEM((L, d), F32),
            pltpu.VMEM((2, d, d), F32),
        ],
    )
    return pl.pallas_call(
        _mlstm_kernel,
        grid_spec=grid_spec,
        out_shape=jax.ShapeDtypeStruct((B * L, ML_WIDTH), BF16),
        compiler_params=_cparams(("arbitrary", "arbitrary")),
        name="mlstm",
    )(bias, proj, proj, proj, proj, conv_w, conv_w, gate_tok, gate_row, norm_w)


def _chunk_cumsum(x, rev):
    c, n = x.shape
    r = lax.broadcasted_iota(I32, (c, c), 0)
    s = lax.broadcasted_iota(I32, (c, c), 1)
    tri = jnp.where((s >= r) if rev else (s <= r), 1.0, 0.0).astype(BF16)
    hi = x.astype(BF16)
    r1 = x - hi.astype(F32)
    mid = r1.astype(BF16)
    lo = (r1 - mid.astype(F32)).astype(BF16)
    parts = _dot(tri, jnp.concatenate([hi, mid, lo], axis=1))
    return parts[:, :n] + parts[:, n:2 * n] + parts[:, 2 * n:]


class _HgStream:
    pass


def _hg_cumsum_stage(streams):
    for s in streams:
        s.b = _chunk_cumsum(s.logf, s.rev)
        s.vb = s.v.astype(BF16)
        s.o = jnp.sum(s.q * s.k, axis=1, keepdims=True) * s.v


def _hg_levels_stage(streams):
    c, dk = streams[0].q.shape
    row = lax.broadcasted_iota(I32, (c, dk), 0)
    tt = lax.broadcasted_iota(I32, (c, c), 0)
    ss = lax.broadcasted_iota(I32, (c, c), 1)

    for s in streams:
        strict = (ss > tt) if s.rev else (ss < tt)
        f_prev = pltpu.roll(s.f, 1, 0)
        f_next = pltpu.roll(s.f, c - 1, 0)
        amat = None
        m = c // 2
        while m >= 1:
            blk = 2 * m
            u = row & (blk - 1)
            tgt = (u < m) if s.rev else (u >= m)
            if m == 1:
                e = jnp.where(tgt, s.f, 1.0)
            elif m == 2:
                if s.rev:
                    e = jnp.where(u == 0, s.f * f_next, jnp.where(u == 1, s.f, jnp.where(u == 2, 1.0, f_prev)))
                else:
                    e = jnp.where(u == 0, f_next, jnp.where(u == 1, 1.0, jnp.where(u == 2, s.f, s.f * f_prev)))
            else:
                bb = s.b.reshape(c // blk, blk, dk)
                ref = bb[:, m:m + 1, :] if s.rev else bb[:, m - 1:m, :]
                e = jnp.exp(-jnp.abs(bb - ref)).reshape(c, dk)
            qk = (jnp.where(tgt, s.q, s.k) * e).astype(BF16)
            a = _dot_nt(qk, qk).astype(BF16)
            amat = a if amat is None else jnp.where((tt // blk) == (ss // blk), a, amat)
            m = m // 2
        s.amat = jnp.where(strict, amat, jnp.zeros_like(amat))
        s.edge = s.b[0:1, :] if s.rev else s.b[c - 1:c, :]
        s.qi = (s.q * jnp.exp(s.b)).astype(BF16)
        s.kl = (s.k * jnp.exp(s.edge - s.b)).astype(BF16)


def _hg_intra_stage(streams):
    for s in streams:
        s.o = s.o + _dot(s.amat, s.vb)


def _hg_state_stage(streams):
    for s in streams:
        st = s.get_state()
        s.o = s.o + _dot_nt(s.qi, st.astype(BF16))
        s.put_state(jnp.exp(s.edge) * st + _dot_tn(s.vb, s.kl))


def _hgrn2_kernel(q_ref, ff_ref, fb_ref, v_ref, g_ref, lb_ref, nw_ref, out_ref, of_ref, ob_ref, st_ref, qs_ref,
                  pb_ref, po_ref, pe_ref):
    L, dk = q_ref.shape
    c = HG_CHUNK
    nc = L // c

    lbp = lb_ref[...]
    mx = jnp.max(lbp, axis=0, keepdims=True)
    ex = jnp.exp(lbp - mx)
    lb = ex[0:1, :] / jnp.sum(ex, axis=0, keepdims=True)

    st_ref[...] = jnp.zeros_like(st_ref)
    qs_ref[...] = _silu(q_ref[...])

    per_step = HG_STREAM_CHUNKS
    n_steps = nc // per_step

    def shells(j):
        out = []
        for i in range(per_step):
            for rev in (False, True):
                s = _HgStream()
                chunk = (nc - 1 - j * per_step - i) if rev else (j * per_step + i)
                s.t0 = chunk * c if isinstance(chunk, int) else pl.multiple_of(chunk * c, c)
                s.rev = rev
                s.slot = 1 if rev else 0
                s.get_state = functools.partial(lambda slot: st_ref[slot], s.slot)
                s.put_state = functools.partial(st_ref.__setitem__, s.slot)
                out.append(s)
        return out

    def front(j):
        streams = shells(j)
        for s in streams:
            sig = _sigmoid((fb_ref if s.rev else ff_ref)[pl.ds(s.t0, c), :])
            s.f = lb + (1.0 - lb) * sig
            s.k = (1.0 - lb) * (1.0 - sig)
            s.logf = jnp.log(s.f)
            s.q = qs_ref[pl.ds(s.t0, c), :]
            s.v = v_ref[pl.ds(s.t0, c), :]
        return streams

    def save(streams):
        for i, s in enumerate(streams):
            pb_ref[i, 0] = s.amat
            pb_ref[i, 1] = s.vb
            pb_ref[i, 2] = s.qi
            pb_ref[i, 3] = s.kl
            po_ref[i] = s.o
            pe_ref[i] = jnp.broadcast_to(s.edge, (SUBLANES, dk))

    def load(j):
        streams = shells(j)
        for i, s in enumerate(streams):
            s.amat, s.vb, s.qi, s.kl = pb_ref[i, 0], pb_ref[i, 1], pb_ref[i, 2], pb_ref[i, 3]
            s.o = po_ref[i]
            s.edge = pe_ref[i, 0:1, :]
        return streams

    def emit(streams):
        for s in streams:
            (ob_ref if s.rev else of_ref)[pl.ds(s.t0, c), :] = s.o

    first = front(0)
    _hg_cumsum_stage(first)
    _hg_levels_stage(first)
    save(first)

    def body(j, carry):
        old = load(j - 1)
        new = front(j)
        _hg_intra_stage(old)
        _hg_state_stage(old[:2])
        _hg_cumsum_stage(new)
        _hg_state_stage(old[2:])
        _hg_levels_stage(new)
        emit(old)
        save(new)
        return carry

    lax.fori_loop(1, n_steps, body, 0)

    last = load(n_steps - 1)
    _hg_intra_stage(last)
    _hg_state_stage(last)
    emit(last)

    o = of_ref[...] + ob_ref[...]
    o = o * lax.rsqrt(jnp.mean(o * o, axis=1, keepdims=True) + EPS)
    y = o * nw_ref[...] * _silu(g_ref[...])
    out_ref[...] = y.astype(out_ref.dtype)


def _hgrn2(proj, hg_lb, norm_w, B, L):
    dk = HG_D
    H = HG_HEADS
    base = 4 * ML_WIDTH // dk
    return pl.pallas_call(
        _hgrn2_kernel,
        grid=(B, H),
        in_specs=[
            pl.BlockSpec((L, dk), lambda b, h: (b, base + h)),
            pl.BlockSpec((L, dk), lambda b, h: (b, base + H + h)),
            pl.BlockSpec((L, dk), lambda b, h: (b, base + 2 * H + h)),
            pl.BlockSpec((L, dk), lambda b, h: (b, base + 3 * H + h)),
            pl.BlockSpec((L, dk), lambda b, h: (b, base + 4 * H + h)),
            pl.BlockSpec((2, dk), lambda b, h: (0, h)),
            pl.BlockSpec((1, dk), lambda b, h: (0, h)),
        ],
        out_specs=pl.BlockSpec((L, dk), lambda b, h: (b, h)),
        out_shape=jax.ShapeDtypeStruct((B * L, HG_WIDTH), BF16),
        scratch_shapes=[
            pltpu.VMEM((L, dk), F32), pltpu.VMEM((L, dk), F32),
            pltpu.VMEM((2, dk, dk), F32),
            pltpu.VMEM((L, dk), F32),
            pltpu.VMEM((2 * HG_STREAM_CHUNKS, 4, HG_CHUNK, dk), BF16),
            pltpu.VMEM((2 * HG_STREAM_CHUNKS, HG_CHUNK, dk), F32),
            pltpu.VMEM((2 * HG_STREAM_CHUNKS, SUBLANES, dk), F32),
        ],
        compiler_params=_cparams(("arbitrary", "arbitrary")),
        name="hgrn2",
    )(proj, proj, proj, proj, proj, hg_lb, norm_w)


def _outproj_kernel(x_ref, ml_ref, hg_ref, eg_ref, eb_ref, wo_ref, g1_ref, b1_ref, wrh_ref, wrl_ref,
                    x1_ref, lgt_ref, ya_ref, yb_ref):
    i = pl.program_id(0)
    half = ml_ref.shape[1]
    n_exp = lgt_ref.shape[0]

    @pl.when(i == 0)
    def _():
        yb_ref[...] = jnp.zeros_like(yb_ref)

    def step(y_new, y_old):
        mix = _dot(ml_ref[...], wo_ref[0:half, :]) + _dot(hg_ref[...], wo_ref[half:, :])
        y_new[...] = ALPHA * _layer_norm(x_ref[...], eg_ref[...], eb_ref[...]) + mix
        x1 = _layer_norm(y_old[...], g1_ref[...], b1_ref[...])
        x1_ref[...] = x1
        xh, xl = _split_bf16(x1)
        lgt = _dot_nt(wrh_ref[...], xh) + _dot_nt(wrh_ref[...], xl) + _dot_nt(wrl_ref[...], xh)
        lgt_ref[...] = lgt[0:n_exp, :]

    @pl.when(lax.rem(i, 2) == 0)
    def _():
        step(ya_ref, yb_ref)

    @pl.when(lax.rem(i, 2) == 1)
    def _():
        step(yb_ref, ya_ref)


def _outproj(x, ml, hg, eg, eb, wo, g1, b1, wrt_hi, wrt_lo, n_exp, tm=256):
    n, d = x.shape
    n_tiles = n // tm
    row = lambda i: (jnp.minimum(i, n_tiles - 1), 0)
    prev_row = lambda i: (jnp.maximum(i - 1, 0), 0)
    fixed = lambda i: (0, 0)
    return pl.pallas_call(
        _outproj_kernel,
        grid=(n_tiles + 1,),
        in_specs=[
            pl.BlockSpec((tm, d), row),
            pl.BlockSpec((tm, ML_WIDTH), row),
            pl.BlockSpec((tm, HG_WIDTH), row),
            pl.BlockSpec((1, d), fixed),
            pl.BlockSpec((1, d), fixed),
            pl.BlockSpec((d, d), fixed),
            pl.BlockSpec((1, d), fixed),
            pl.BlockSpec((1, d), fixed),
            pl.BlockSpec((LANES, d), fixed),
            pl.BlockSpec((LANES, d), fixed),
        ],
        out_specs=[pl.BlockSpec((tm, d), prev_row),
                   pl.BlockSpec((n_exp, tm), lambda i: (0, jnp.maximum(i - 1, 0)))],
        out_shape=[jax.ShapeDtypeStruct((n, d), F32), jax.ShapeDtypeStruct((n_exp, n), F32)],
        scratch_shapes=[pltpu.VMEM((tm, d), F32), pltpu.VMEM((tm, d), F32)],
        compiler_params=_cparams(("arbitrary",)),
        name="outproj",
    )(x, ml, hg, eg, eb, wo, g1, b1, wrt_hi, wrt_lo)


def _pad_cols(w, width=LANES):
    return jnp.pad(w, ((0, 0), (0, width - w.shape[1])))


def _token_mixer_stage(x, emb_ln_g, emb_ln_b, w_in, conv_w, ml_igate_b, ml_fgate_b, ml_norm_w,
                       hg_lb, hg_norm_w, w_out, ln1_g, ln1_b, w_router, B, L):
    d = D_MODEL
    g0 = 4 * ML_WIDTH
    w_main = jnp.concatenate([w_in[:, :g0], w_in[:, g0 + 16:]], axis=1).astype(BF16)
    wg = w_in[:, g0:g0 + 16].reshape(d, 4, ML_HEADS).transpose(0, 2, 1).reshape(d, 16)
    wg_both = jnp.concatenate(_split_bf16(_pad_cols(wg)), axis=1)
    eg = emb_ln_g.reshape(1, d)
    eb = emb_ln_b.reshape(1, d)
    proj, gate_tok, gate_t = _ln_proj(x, eg, eb, w_main, wg_both)

    gate_row = gate_t.reshape(ML_HEADS, 4, B, L).transpose(2, 0, 1, 3)
    bias = jnp.concatenate([ml_igate_b[0], ml_igate_b[1], ml_fgate_b[0], ml_fgate_b[1]]).astype(F32)
    ml = _mlstm(proj, conv_w, gate_tok, gate_row, bias, ml_norm_w.reshape(1, ML_WIDTH), B, L)
    hg = _hgrn2(proj, hg_lb, hg_norm_w.reshape(1, HG_WIDTH), B, L)

    wrt_hi, wrt_lo = _split_bf16(_pad_cols(w_router).T)
    return _outproj(x, ml, hg, eg, eb, w_out.astype(BF16), ln1_g.reshape(1, d), ln1_b.reshape(1, d),
                    wrt_hi, wrt_lo, w_router.shape[1])


def _excl_token_cumsum(mask):
    E, R, ln = mask.shape
    mf = jnp.where(mask, 1.0, 0.0)
    mb = mf.astype(BF16)
    upper = jnp.where(lax.broadcasted_iota(I32, (ln, ln), 0) <= lax.broadcasted_iota(I32, (ln, ln), 1),
                      1.0, 0.0).astype(BF16)
    lower = jnp.where(lax.broadcasted_iota(I32, (R, R), 1) < lax.broadcasted_iota(I32, (R, R), 0),
                      1.0, 0.0).astype(BF16)
    ones = jnp.ones((ln, ln), BF16)
    within = _dot(mb.reshape(E * R, ln), upper).reshape(E, R, ln)
    rows = jnp.stack([_dot(_dot(lower, mb[e]).astype(BF16), ones) for e in range(E)], axis=0)
    return within - mf + rows


def _select_kernel(lg_ref, pos_ref, posm_ref, wts_ref, *, cap):
    E = lg_ref.shape[0]
    lg = lg_ref[...]
    mx = jnp.max(lg, axis=0, keepdims=True)
    ex = jnp.exp(lg - mx)
    aff = ex / jnp.sum(ex, axis=0, keepdims=True)

    def count(mask):
        ones = jnp.where(mask, 1.0, 0.0)
        return jnp.sum(jnp.sum(ones, axis=2, keepdims=True), axis=1, keepdims=True)

    def body(i, tbits):
        cand = tbits | lax.shift_left(jnp.int32(1), 30 - i)
        cnt = count(aff >= lax.bitcast_convert_type(cand, F32))
        return jnp.where(cnt >= cap, cand, tbits)

    tbits = lax.fori_loop(0, 31, body, jnp.zeros((E, 1, 1), I32))
    thr = lax.bitcast_convert_type(tbits, F32)
    nxt = lax.bitcast_convert_type(tbits + 1, F32)
    above = aff >= nxt
    band = jnp.logical_and(aff >= thr, jnp.logical_not(above))
    need = cap - count(above)
    sel = jnp.logical_or(above, jnp.logical_and(band, _excl_token_cumsum(band) < need))
    pos = _excl_token_cumsum(sel).astype(I32)
    pos_ref[...] = pos
    posm_ref[...] = jnp.where(sel, pos, -1)
    wts_ref[...] = jnp.where(sel, aff, 0.0)


def _select(lg_t, cap):
    E, R, ln = lg_t.shape
    full = pl.BlockSpec((E, R, ln), lambda i: (0, 0, 0))
    return pl.pallas_call(
        functools.partial(_select_kernel, cap=cap),
        grid=(1,),
        in_specs=[full],
        out_specs=[full, full, full],
        out_shape=[jax.ShapeDtypeStruct((E, R, ln), I32), jax.ShapeDtypeStruct((E, R, ln), I32),
                   jax.ShapeDtypeStruct((E, R, ln), F32)],
        compiler_params=_cparams(("arbitrary",)),
        name="select",
    )(lg_t)


TOK_TILE = 256
SLOT_CHUNK = 64


def _pack_bf16_pairs(x):
    h = x.shape[1] // 2
    lo = lax.shift_right_logical(lax.bitcast_convert_type(x[:, :h], U32), jnp.uint32(16))
    hi = lax.bitcast_convert_type(x[:, h:], U32) & jnp.uint32(0xFFFF0000)
    return hi | lo


def _unpack_bf16_pairs(w):
    lo = lax.bitcast_convert_type(lax.shift_left(w, jnp.uint32(16)), F32).astype(BF16)
    hi = lax.bitcast_convert_type(w & jnp.uint32(0xFFFF0000), F32).astype(BF16)
    return lo, hi


def _dispatch_kernel(off_ref, base_ref, cnt_ref, xa_ref, xb_ref, pos_ref, xe_ref,
                     x16_ref, res_ref, stage_ref, ostage_ref, carry_ref, sem, osem, *, tiles_a, cap_total, pad):
    E = N_EXPERTS
    CH = SLOT_CHUNK
    SUB = SUBLANES
    tc = xa_ref.shape[0]
    t = pl.program_id(0)
    nt = pl.num_programs(0)
    slot = lax.rem(t, 2)

    @pl.when(t < tiles_a)
    def _():
        x16_ref[...] = xa_ref[...].astype(BF16)

    @pl.when(t >= tiles_a)
    def _():
        x16_ref[...] = xb_ref[...].astype(BF16)

    xb = x16_ref[...]

    @pl.when(t == 0)
    def _():
        carry_ref[...] = jnp.zeros_like(carry_ref)

    base = [base_ref[t * E + e] for e in range(E)]
    cnt = [cnt_ref[t * E + e] for e in range(E)]
    al = [pl.multiple_of((b // SUB) * SUB, SUB) for b in base]
    first = [off_ref[t * E + e] - (base[e] - al[e]) for e in range(E)]

    def onehot(e, start, rows):
        kio = lax.broadcasted_iota(I32, (rows, tc), 0)
        rel = pos_ref[e:e + 1, :] - (first[e] + start)
        return jnp.where(rel == kio, 1.0, 0.0).astype(BF16)

    ot = jnp.concatenate([onehot(e, 0, CH) for e in range(E)], axis=0)
    res_ref[...] = _dot(ot, xb)
    for e in range(E):
        res_ref[e * CH:e * CH + SUB, :] += carry_ref[e * SUB:(e + 1) * SUB, :]
    stage_ref[slot] = _pack_bf16_pairs(res_ref[...])

    nxt = [((base[e] + cnt[e]) // SUB) * SUB - al[e] for e in range(E)]
    oc = jnp.concatenate([onehot(e, nxt[e], SUB) for e in range(E)], axis=0)
    new_carry = _dot(oc, xb)
    for e in range(E):
        keep = jnp.where(nxt[e] == 0, 1.0, 0.0)
        carry_ref[e * SUB:(e + 1) * SUB, :] = new_carry[e * SUB:(e + 1) * SUB, :] + keep * carry_ref[e * SUB:(e + 1) * SUB, :]

    def main_copy(s, e, row):
        return pltpu.make_async_copy(stage_ref.at[s, pl.ds(e * CH, CH)], xe_ref.at[pl.ds(row, CH)], sem.at[s])

    @pl.when(t > 0)
    def _():
        for e in range(E):
            main_copy(1 - slot, e, 0).wait()

    for e in range(E):
        main_copy(slot, e, al[e]).start()

    for e in range(E):
        nch = (base[e] - al[e] + cnt[e] + CH - 1) // CH

        def body(c, carry, e=e):
            ostage_ref[...] = _pack_bf16_pairs(_dot(onehot(e, c * CH, CH), xb))
            row = pl.multiple_of(al[e] + c * CH, SUB)
            cp = pltpu.make_async_copy(ostage_ref, xe_ref.at[pl.ds(row, CH)], osem)
            cp.start()
            cp.wait()
            return carry

        lax.fori_loop(1, nch, body, 0)

    @pl.when(t == nt - 1)
    def _():
        for e in range(E):
            main_copy(slot, e, 0).wait()
        ostage_ref[...] = jnp.zeros_like(ostage_ref)
        fills = [pltpu.make_async_copy(ostage_ref, xe_ref.at[pl.ds(e * (cap_total + pad) + cap_total + j * CH, CH)], osem)
                 for e in range(E) for j in range(pad // CH)]
        for cp in fills:
            cp.start()
        for cp in fills:
            cp.wait()


def _dispatch(off, base, cnt, x1_a, x1_b, pos_rows, cap_total, pad):
    d = x1_a.shape[1]
    E = N_EXPERTS
    tc = TOK_TILE
    tiles_a = x1_a.shape[0] // tc
    tiles_b = x1_b.shape[0] // tc
    grid_spec = pltpu.PrefetchScalarGridSpec(
        num_scalar_prefetch=3,
        grid=(tiles_a + tiles_b,),
        in_specs=[
            pl.BlockSpec((tc, d), lambda i, *_: (jnp.minimum(i, tiles_a - 1), 0)),
            pl.BlockSpec((tc, d), lambda i, *_: (jnp.maximum(i - tiles_a, 0), 0)),
            pl.BlockSpec((E, tc), lambda i, *_: (0, i)),
        ],
        out_specs=pl.BlockSpec(memory_space=pl.ANY),
        scratch_shapes=[
            pltpu.VMEM((tc, d), BF16),
            pltpu.VMEM((E * SLOT_CHUNK, d), F32),
            pltpu.VMEM((2, E * SLOT_CHUNK, d // 2), U32),
            pltpu.VMEM((SLOT_CHUNK, d // 2), U32),
            pltpu.VMEM((E * SUBLANES, d), F32),
            pltpu.SemaphoreType.DMA((2,)),
            pltpu.SemaphoreType.DMA(()),
        ],
    )
    return pl.pallas_call(
        functools.partial(_dispatch_kernel, tiles_a=tiles_a, cap_total=cap_total, pad=pad),
        grid_spec=grid_spec,
        out_shape=jax.ShapeDtypeStruct((E * (cap_total + pad), d // 2), U32),
        compiler_params=_cparams(("arbitrary",)),
        name="dispatch",
    )(off, base, cnt, x1_a, x1_b, pos_rows)


def _ffn_hidden_kernel(xe_ref, wg_ref, wu_ref, hid_ref, wg16_ref, wu16_ref):
    @pl.when(pl.program_id(2) == 0)
    def _():
        wg16_ref[...] = wg_ref[0].astype(BF16)
        wu16_ref[...] = wu_ref[0].astype(BF16)

    lo, hi = _unpack_bf16_pairs(xe_ref[0])
    h = lo.shape[1]
    gate = _dot(lo, wg16_ref[0:h, :]) + _dot(hi, wg16_ref[h:, :])
    up = _dot(lo, wu16_ref[0:h, :]) + _dot(hi, wu16_ref[h:, :])
    hid_ref[0] = (_silu(gate) * up).astype(hid_ref.dtype)


def _ffn_down_kernel(hid_ref, wd_ref, out_ref, wd16_ref):
    @pl.when(pl.program_id(2) == 0)
    def _():
        wd16_ref[...] = wd_ref[0].astype(BF16)

    out_ref[0] = _dot(hid_ref[0], wd16_ref[...]).astype(out_ref.dtype)


def _ffn(xe, wg, wu, wd, cap_total, tf=512, tn=1024):
    E, _, dh = xe.shape
    d = 2 * dh
    ff = wg.shape[2]
    tm = next(t for t in (1024, 512, 256, 128) if cap_total % t == 0)
    hid = pl.pallas_call(
        _ffn_hidden_kernel,
        grid=(E, ff // tf, cap_total // tm),
        in_specs=[
            pl.BlockSpec((1, tm, dh), lambda e, f, i: (e, i, 0)),
            pl.BlockSpec((1, d, tf), lambda e, f, i: (e, 0, f)),
            pl.BlockSpec((1, d, tf), lambda e, f, i: (e, 0, f)),
        ],
        out_specs=pl.BlockSpec((1, tm, tf), lambda e, f, i: (e, i, f)),
        out_shape=jax.ShapeDtypeStruct((E, cap_total, ff), BF16),
        scratch_shapes=[pltpu.VMEM((d, tf), BF16), pltpu.VMEM((d, tf), BF16)],
        compiler_params=_cparams(("arbitrary", "arbitrary", "arbitrary")),
        name="ffn_hidden",
    )(xe, wg, wu)
    return pl.pallas_call(
        _ffn_down_kernel,
        grid=(E, d // tn, cap_total // tm),
        in_specs=[
            pl.BlockSpec((1, tm, ff), lambda e, c, i: (e, i, 0)),
            pl.BlockSpec((1, ff, tn), lambda e, c, i: (e, 0, c)),
        ],
        out_specs=pl.BlockSpec((1, tm, tn), lambda e, c, i: (e, i, c)),
        out_shape=jax.ShapeDtypeStruct((E, cap_total, d), BF16),
        scratch_shapes=[pltpu.VMEM((ff, tn), BF16)],
        compiler_params=_cparams(("arbitrary", "arbitrary", "arbitrary")),
        name="ffn_down",
    )(hid, wd)


def _combine_kernel(base_ref, cnt_ref, x1_ref, grow_ref, w_ref, g2_ref, b2_ref, ye_ref, out_ref,
                    buf_ref, obuf_ref, acc_ref, sem, osem, *, rows_total):
    E = N_EXPERTS
    CH = SLOT_CHUNK
    tc = x1_ref.shape[0]
    t = pl.program_id(0)
    nt = pl.num_programs(0)
    slot = lax.rem(t, 2)
    last = rows_total - CH
    ALIGN = 2 * SUBLANES

    def window(step, e):
        al = (base_ref[step * E + e] // ALIGN) * ALIGN
        return al, pl.multiple_of(jnp.minimum(al, last), ALIGN)

    def chunk_copy(s, e, start):
        return pltpu.make_async_copy(ye_ref.at[pl.ds(start, CH)], buf_ref.at[s, pl.ds(e * CH, CH)], sem.at[s, e])

    @pl.when(t == 0)
    def _():
        for e in range(E):
            chunk_copy(0, e, window(0, e)[1]).start()

    @pl.when(t + 1 < nt)
    def _():
        for e in range(E):
            chunk_copy(1 - slot, e, window(t + 1, e)[1]).start()

    kio = lax.broadcasted_iota(I32, (CH, tc), 0)

    def weights(e, start, lo):
        grow = grow_ref[e:e + 1, :]
        hit = jnp.logical_and(grow - start == kio, grow >= lo)
        return jnp.where(hit, w_ref[e:e + 1, :], 0.0).astype(BF16)

    at = jnp.concatenate([weights(e, window(t, e)[1], 0) for e in range(E)], axis=0)
    for e in range(E):
        chunk_copy(slot, e, 0).wait()
    acc_ref[...] = ALPHA * x1_ref[...] + _dot_tn(at, buf_ref[slot])

    for e in range(E):
        al = window(t, e)[0]
        nch = (base_ref[t * E + e] - al + cnt_ref[t * E + e] + CH - 1) // CH

        def body(c, carry, e=e, al=al):
            lo = al + c * CH
            start = pl.multiple_of(jnp.minimum(lo, last), ALIGN)
            cp = pltpu.make_async_copy(ye_ref.at[pl.ds(start, CH)], obuf_ref, osem)
            cp.start()
            cp.wait()
            acc_ref[...] += _dot_tn(weights(e, start, lo), obuf_ref[...])
            return carry

        lax.fori_loop(1, nch, body, 0)

    out_ref[...] = _layer_norm(acc_ref[...], g2_ref[...], b2_ref[...])


def _combine(base, cnt, x1, grow_rows, w_rows, g2, b2, ye, rows_total):
    n, d = x1.shape
    E = N_EXPERTS
    tc = TOK_TILE
    grid_spec = pltpu.PrefetchScalarGridSpec(
        num_scalar_prefetch=2,
        grid=(n // tc,),
        in_specs=[
            pl.BlockSpec((tc, d), lambda i, *_: (i, 0)),
            pl.BlockSpec((E, tc), lambda i, *_: (0, i)),
            pl.BlockSpec((E, tc), lambda i, *_: (0, i)),
            pl.BlockSpec((1, d), lambda i, *_: (0, 0)),
            pl.BlockSpec((1, d), lambda i, *_: (0, 0)),
            pl.BlockSpec(memory_space=pl.ANY),
        ],
        out_specs=pl.BlockSpec((tc, d), lambda i, *_: (i, 0)),
        scratch_shapes=[
            pltpu.VMEM((2, E * SLOT_CHUNK, d), BF16),
            pltpu.VMEM((SLOT_CHUNK, d), BF16),
            pltpu.VMEM((tc, d), F32),
            pltpu.SemaphoreType.DMA((2, E)),
            pltpu.SemaphoreType.DMA(()),
        ],
    )
    return pl.pallas_call(
        functools.partial(_combine_kernel, rows_total=rows_total),
        grid_spec=grid_spec,
        out_shape=jax.ShapeDtypeStruct((n, d), F32),
        compiler_params=_cparams(("arbitrary",)),
        name="combine",
    )(base, cnt, x1, grow_rows, w_rows, g2, b2, ye)


def _routing_tables(pos, posm, wts, cap, slot0, cap_total, pad):
    E = N_EXPERTS
    n = pos.shape[1] * pos.shape[2]
    tiles = n // TOK_TILE
    eidx = jnp.arange(E, dtype=I32)
    off = pos.reshape(E, n)[:, ::TOK_TILE].T
    cnt = jnp.concatenate([off[1:], jnp.full((1, E), cap, I32)], axis=0) - off
    base_x = off + eidx[None, :] * (cap_total + pad) + slot0
    base_y = off + eidx[None, :] * cap_total + slot0
    pos_rows = posm.reshape(E, n)
    grow_rows = jnp.where(pos_rows >= 0, pos_rows + eidx[:, None] * cap_total + slot0, -1)
    flat = lambda a: a.reshape(tiles * E)
    return flat(off), flat(base_x), flat(base_y), flat(cnt), pos_rows, grow_rows, wts.reshape(E, n)


def kernel(x_prompt, x_sample, emb_ln_g, emb_ln_b, w_in, conv_w, ml_igate_b, ml_fgate_b, ml_norm_w, hg_lb, hg_norm_w, w_out, ln1_g, ln1_b, w_router, w_gate, w_up, w_down, ln2_g, ln2_b):
    E = N_EXPERTS
    d = D_MODEL
    groups = (x_prompt, x_sample)
    caps = [EC_FACTOR * x.shape[0] * x.shape[1] // E for x in groups]
    cap_total = sum(caps)
    pad = 2 * SLOT_CHUNK
    assert all(c % (2 * SUBLANES) == 0 for c in caps)

    staged = []
    tables = []
    slot0 = 0
    for x, cap in zip(groups, caps):
        B, L, _ = x.shape
        n = B * L
        x1, lg = _token_mixer_stage(x.reshape(n, d), emb_ln_g, emb_ln_b, w_in[0], conv_w[0], ml_igate_b[0],
                                    ml_fgate_b[0], ml_norm_w[0], hg_lb, hg_norm_w[0], w_out[0], ln1_g[0],
                                    ln1_b[0], w_router[0], B, L)
        lg_t = lg.reshape(E, n // LANES, LANES)
        pos, posm, wts = _select(lg_t, cap)
        off, base_x, base_y, cnt, pos_rows, grow_t, w_t = _routing_tables(pos, posm, wts, cap, slot0, cap_total, pad)
        tables.append((off, base_x, cnt, pos_rows))
        staged.append((x1, base_y, cnt, grow_t, w_t, (B, L)))
        slot0 += cap

    off, base_x, cnt, pos_rows = (jnp.concatenate(parts, axis=-1) for parts in zip(*tables))
    xe = _dispatch(off, base_x, cnt, staged[0][0], staged[1][0], pos_rows, cap_total, pad)

    ye = _ffn(xe.reshape(E, cap_total + pad, d // 2), w_gate[0], w_up[0], w_down[0], cap_total)
    ye = ye.reshape(E * cap_total, d)

    g2 = ln2_g[0].reshape(1, d)
    b2 = ln2_b[0].reshape(1, d)
    outs = []
    for x1, base_y, cnt, grow_t, w_t, (B, L) in staged:
        y = _combine(base_y, cnt, x1, grow_t, w_t, g2, b2, ye, E * cap_total)
        outs.append(y.reshape(B, L, d))
    return tuple(outs)
```

```python
import functools

import jax
import jax.numpy as jnp
from jax import lax
from jax.experimental import pallas as pl
from jax.experimental.pallas import tpu as pltpu

F32 = jnp.float32
BF16 = jnp.bfloat16
I32 = jnp.int32
U32 = jnp.uint32

D_MODEL = 2048
ML_WIDTH = 1024
ML_HEADS = 4
ML_DH = 256
HG_WIDTH = 1024
HG_HEADS = 8
HG_D = 128
N_EXPERTS = 16
EC_FACTOR = 2
EXPERT_FF = 2048
DEPTH = 1
ALPHA = (2.0 * DEPTH) ** 0.25
EPS = 1e-5

LANES = 128
SUBLANES = 8
N_GATE_COLS = 16
VMEM_LIMIT = 56 * 1024 * 1024

ML_CHUNK = 256
HG_CHUNK = 128
HG_STREAM_CHUNKS = 4


def _cparams(sem):
    return pltpu.CompilerParams(dimension_semantics=sem, vmem_limit_bytes=VMEM_LIMIT)


def _split_bf16(x):
    hi = x.astype(BF16)
    lo = (x - hi.astype(F32)).astype(BF16)
    return hi, lo


def _layer_norm(x, g, b):
    mu = jnp.mean(x, axis=-1, keepdims=True)
    xc = x - mu
    var = jnp.mean(xc * xc, axis=-1, keepdims=True)
    return xc * lax.rsqrt(var + EPS) * g + b


def _sigmoid(x):
    return 1.0 / (1.0 + jnp.exp(-x))


def _silu(x):
    return x * _sigmoid(x)


def _log_sigmoid(x):
    return jnp.minimum(x, 0.0) - jnp.log1p(jnp.exp(-jnp.abs(x)))


def _dot(a, b):
    return jnp.dot(a, b, preferred_element_type=F32)


def _dot_nt(a, b):
    return lax.dot_general(a, b, (((1,), (1,)), ((), ())), preferred_element_type=F32)


def _dot_tn(a, b):
    return lax.dot_general(a, b, (((0,), (0,)), ((), ())), preferred_element_type=F32)


def _ln_proj_kernel(x_ref, g_ref, b_ref, w_ref, wg_ref, proj_ref, gate_ref, gatet_ref,
                    hha_ref, hla_ref, hhb_ref, hlb_ref, *, n_slices):
    i = pl.program_id(0)
    j = pl.program_id(1)
    rs = x_ref.shape[0]

    @pl.when(jnp.logical_and(i == 0, j == 0))
    def _():
        hhb_ref[...] = jnp.zeros_like(hhb_ref)
        hlb_ref[...] = jnp.zeros_like(hlb_ref)

    def step(hh_new, hl_new, hh_old, hl_old):
        @pl.when(j == 0)
        def _():
            both = _dot(hh_old[...], wg_ref[...])
            gates = both[:, :LANES] + both[:, LANES:] + _dot(hl_old[...], wg_ref[:, :LANES])
            gate_ref[...] = gates
            gatet_ref[...] = gates.T[0:gatet_ref.shape[0], :]

        proj_ref[...] = _dot(hh_old[...], w_ref[...]).astype(proj_ref.dtype)
        r0 = pl.multiple_of(jnp.minimum(j, n_slices - 1) * rs, rs)
        hh, hl = _split_bf16(_layer_norm(x_ref[...], g_ref[...], b_ref[...]))
        hh_new[pl.ds(r0, rs), :] = hh
        hl_new[pl.ds(r0, rs), :] = hl

    @pl.when(lax.rem(i, 2) == 0)
    def _():
        step(hha_ref, hla_ref, hhb_ref, hlb_ref)

    @pl.when(lax.rem(i, 2) == 1)
    def _():
        step(hhb_ref, hlb_ref, hha_ref, hla_ref)


def _ln_proj(x, g, b, w_main, wg_both, tm=1024, tn=1024):
    n, d = x.shape
    tm = min(tm, n)
    ncol = w_main.shape[1] // tn
    n_tiles = n // tm
    n_slices = ncol - 1
    rs = tm // n_slices
    assert tm % n_slices == 0 and rs % SUBLANES == 0
    prev_row = lambda i, j: jnp.maximum(i - 1, 0)
    return pl.pallas_call(
        functools.partial(_ln_proj_kernel, n_slices=n_slices),
        grid=(n_tiles + 1, ncol),
        in_specs=[
            pl.BlockSpec((rs, d), lambda i, j: (jnp.minimum(i, n_tiles - 1) * n_slices
                                                + jnp.minimum(j, n_slices - 1), 0)),
            pl.BlockSpec((1, d), lambda i, j: (0, 0)),
            pl.BlockSpec((1, d), lambda i, j: (0, 0)),
            pl.BlockSpec((d, tn), lambda i, j: (0, j)),
            pl.BlockSpec((d, 2 * LANES), lambda i, j: (0, 0)),
        ],
        out_specs=[
            pl.BlockSpec((tm, tn), lambda i, j: (prev_row(i, j), jnp.where(i > 0, j, 0))),
            pl.BlockSpec((tm, LANES), lambda i, j: (prev_row(i, j), 0)),
            pl.BlockSpec((N_GATE_COLS, tm), lambda i, j: (0, prev_row(i, j))),
        ],
        out_shape=[
            jax.ShapeDtypeStruct((n, ncol * tn), F32),
            jax.ShapeDtypeStruct((n, LANES), F32),
            jax.ShapeDtypeStruct((N_GATE_COLS, n), F32),
        ],
        scratch_shapes=[pltpu.VMEM((tm, d), BF16)] * 4,
        compiler_params=_cparams(("arbitrary", "arbitrary")),
        name="ln_proj",
    )(x, g, b, w_main, wg_both)


class _MlStream:
    pass


def _mlstm_chunks(streams):
    c = streams[0].qb.shape[0]
    tt = lax.broadcasted_iota(I32, (c, c), 0)
    ss = lax.broadcasted_iota(I32, (c, c), 1)

    for s in streams:
        causal = (ss >= tt) if s.rev else (ss <= tt)
        b_col = jnp.sum(jnp.where(causal, s.f_row, 0.0), axis=1, keepdims=True)
        anti = (tt >= ss) if s.rev else (tt <= ss)
        b_row = jnp.sum(jnp.where(anti, s.f_col, 0.0), axis=0, keepdims=True)
        total = jnp.sum(s.f_row, axis=1, keepdims=True)

        dmat = jnp.where(causal, b_col - b_row + s.i_row, -jnp.inf)
        a_inter = b_col + s.m_state
        s.m_t = jnp.maximum(a_inter, jnp.max(dmat, axis=1, keepdims=True))
        s.w_inter = jnp.exp(a_inter - s.m_t)
        s.p = jnp.exp(dmat - s.m_t)
        s.qk = _dot_nt(s.qb, s.kb)
        s.qc = _dot(s.qb, s.c_state.astype(BF16))

        g_col = total - b_col + s.i_col
        g_row = total - b_row + s.i_row
        s.m_new = jnp.maximum(total + s.m_state, jnp.max(g_row, axis=1, keepdims=True))
        s.decay = jnp.exp(total + s.m_state - s.m_new)
        ks = s.kb.astype(F32) * jnp.exp(g_col - s.m_new)
        s.kv = _dot_tn(ks.astype(BF16), s.vb)
        s.n_new = s.decay * s.n_state + jnp.sum(ks, axis=0, keepdims=True)

    for s in streams:
        s.s = s.qk * s.p
        s.sv = _dot(s.s.astype(BF16), s.vb)

    for s in streams:
        num = s.w_inter * s.qc + s.sv
        qn = jnp.sum(s.qb.astype(F32) * s.n_state, axis=1, keepdims=True)
        den = s.w_inter * qn + jnp.sum(s.s, axis=1, keepdims=True)
        s.h = num / jnp.maximum(jnp.abs(den), jnp.exp(-s.m_t))
        s.c_new = s.decay * s.c_state + s.kv


def _mlstm_kernel(bias_ref, q_ref, k_ref, v_ref, o_ref, cwq_ref, cwk_ref, gt_ref, gr_ref, nw_ref,
                  out_ref, qc_ref, kc_ref, vc_ref, hf_ref, hb_ref, cs_ref):
    L, d = q_ref.shape

    lane_shift = lax.rem(LANES - 4 * pl.program_id(1), LANES)
    c = ML_CHUNK
    nc = L // c
    hd = pl.program_id(1)

    row = lax.broadcasted_iota(I32, (L, 1), 0)

    def conv_silu(x, w):
        xp = jnp.where(row == 0, 0.0, pltpu.roll(x, 1, 0))
        xn = jnp.where(row == L - 1, 0.0, pltpu.roll(x, L - 1, 0))
        return _silu(w[0:1, :] * xp + w[1:2, :] * x + w[2:3, :] * xn)

    qc_ref[...] = conv_silu(q_ref[...], cwq_ref[...]).astype(BF16)
    kc_ref[...] = (conv_silu(k_ref[...], cwk_ref[...]) * (ML_DH ** -0.5)).astype(BF16)
    vc_ref[...] = v_ref[...].astype(BF16)

    bi_f = bias_ref[hd]
    bi_b = bias_ref[ML_HEADS + hd]
    bf_f = bias_ref[2 * ML_HEADS + hd]
    bf_b = bias_ref[3 * ML_HEADS + hd]

    cs_ref[...] = jnp.zeros_like(cs_ref)

    def gates(t0):
        gc = pltpu.roll(gt_ref[pl.ds(t0, c), :], lane_shift, 1)[:, 0:4]
        gr = gr_ref[0, 0, :, pl.ds(t0, c)]
        return gc, gr

    def stream(t0, slot, rev, bias_i, bias_f, n_state, m_state):
        s = _MlStream()
        gc, gr = gates(t0)
        gi, gf = (1, 3) if rev else (0, 2)
        s.qb = qc_ref[pl.ds(t0, c), :]
        s.kb = kc_ref[pl.ds(t0, c), :]
        s.vb = vc_ref[pl.ds(t0, c), :]
        s.f_col = _log_sigmoid(gc[:, gf:gf + 1] + bias_f)
        s.f_row = _log_sigmoid(gr[gf:gf + 1, :] + bias_f)
        s.i_col = gc[:, gi:gi + 1] + bias_i
        s.i_row = gr[gi:gi + 1, :] + bias_i
        s.c_state = cs_ref[slot]
        s.n_state = n_state
        s.m_state = m_state
        s.rev = rev
        return s

    def body(j, carry):
        n_f, m_f, n_b, m_b = carry
        t0 = pl.multiple_of(j * c, c)
        t1 = pl.multiple_of((nc - 1 - j) * c, c)
        fwd = stream(t0, 0, False, bi_f, bf_f, n_f, m_f)
        bwd = stream(t1, 1, True, bi_b, bf_b, n_b, m_b)
        _mlstm_chunks([fwd, bwd])
        cs_ref[0] = fwd.c_new
        hf_ref[pl.ds(t0, c), :] = fwd.h
        cs_ref[1] = bwd.c_new
        hb_ref[pl.ds(t1, c), :] = bwd.h
        return fwd.n_new, fwd.m_new, bwd.n_new, bwd.m_new

    zn = jnp.zeros((1, d), F32)
    m0 = jnp.full((1, 1), -1e30, F32)
    lax.fori_loop(0, nc, body, (zn, m0, zn, m0))

    hm = hf_ref[...] + hb_ref[...]
    mu = jnp.mean(hm, axis=1, keepdims=True)
    hc = hm - mu
    var = jnp.mean(hc * hc, axis=1, keepdims=True)
    y = hc * lax.rsqrt(var + EPS) * nw_ref[...] * _sigmoid(o_ref[...])
    out_ref[...] = y.astype(out_ref.dtype)


def _mlstm(proj, conv_w, gate_tok, gate_row, bias, norm_w, B, L):
    d = ML_DH
    H = ML_HEADS
    grid_spec = pltpu.PrefetchScalarGridSpec(
        num_scalar_prefetch=1,
        grid=(B, H),
        in_specs=[
            pl.BlockSpec((L, d), lambda b, h, s: (b, h)),
            pl.BlockSpec((L, d), lambda b, h, s: (b, H + h)),
            pl.BlockSpec((L, d), lambda b, h, s: (b, 2 * H + h)),
            pl.BlockSpec((L, d), lambda b, h, s: (b, 3 * H + h)),
            pl.BlockSpec((3, d), lambda b, h, s: (0, h)),
            pl.BlockSpec((3, d), lambda b, h, s: (0, H + h)),
            pl.BlockSpec((L, LANES), lambda b, h, s: (b, 0)),
            pl.BlockSpec((1, 1, 4, L), lambda b, h, s: (b, h, 0, 0)),
            pl.BlockSpec((1, d), lambda b, h, s: (0, h)),
        ],
        out_specs=pl.BlockSpec((L, d), lambda b, h, s: (b, h)),
        scratch_shapes=[
            pltpu.VMEM((L, d), BF16), pltpu.VMEM((L, d), BF16), pltpu.VMEM((L, d), BF16),
            pltpu.VMEM((L, d), F32), pltpu.VMEM((L, d), F32),
            pltpu.VMEM((2, d, d), F32),
        ],
    )
    return pl.pallas_call(
        _mlstm_kernel,
        grid_spec=grid_spec,
        out_shape=jax.ShapeDtypeStruct((B * L, ML_WIDTH), BF16),
        compiler_params=_cparams(("arbitrary", "arbitrary")),
        name="mlstm",
    )(bias, proj, proj, proj, proj, conv_w, conv_w, gate_tok, gate_row, norm_w)


def _chunk_cumsum(x, rev):
    c, n = x.shape
    r = lax.broadcasted_iota(I32, (c, c), 0)
    s = lax.broadcasted_iota(I32, (c, c), 1)
    tri = jnp.where((s >= r) if rev else (s <= r), 1.0, 0.0).astype(BF16)
    hi = x.astype(BF16)
    r1 = x - hi.astype(F32)
    mid = r1.astype(BF16)
    lo = (r1 - mid.astype(F32)).astype(BF16)
    parts = _dot(tri, jnp.concatenate([hi, mid, lo], axis=1))
    return parts[:, :n] + parts[:, n:2 * n] + parts[:, 2 * n:]


class _HgStream:
    pass


def _hg_cumsum_stage(streams):
    for s in streams:
        s.b = _chunk_cumsum(s.logf, s.rev)
        s.vb = s.v.astype(BF16)
        s.o = jnp.sum(s.q * s.k, axis=1, keepdims=True) * s.v


def _hg_levels_stage(streams):
    c, dk = streams[0].q.shape
    row = lax.broadcasted_iota(I32, (c, dk), 0)
    tt = lax.broadcasted_iota(I32, (c, c), 0)
    ss = lax.broadcasted_iota(I32, (c, c), 1)

    for s in streams:
        strict = (ss > tt) if s.rev else (ss < tt)
        f_prev = pltpu.roll(s.f, 1, 0)
        f_next = pltpu.roll(s.f, c - 1, 0)
        amat = None
        m = c // 2
        while m >= 1:
            blk = 2 * m
            u = row & (blk - 1)
            tgt = (u < m) if s.rev else (u >= m)
            if m == 1:
                e = jnp.where(tgt, s.f, 1.0)
            elif m == 2:
                if s.rev:
                    e = jnp.where(u == 0, s.f * f_next, jnp.where(u == 1, s.f, jnp.where(u == 2, 1.0, f_prev)))
                else:
                    e = jnp.where(u == 0, f_next, jnp.where(u == 1, 1.0, jnp.where(u == 2, s.f, s.f * f_prev)))
            else:
                bb = s.b.reshape(c // blk, blk, dk)
                ref = bb[:, m:m + 1, :] if s.rev else bb[:, m - 1:m, :]
                e = jnp.exp(-jnp.abs(bb - ref)).reshape(c, dk)
            qk = (jnp.where(tgt, s.q, s.k) * e).astype(BF16)
            a = _dot_nt(qk, qk).astype(BF16)
            amat = a if amat is None else jnp.where((tt // blk) == (ss // blk), a, amat)
            m = m // 2
        s.amat = jnp.where(strict, amat, jnp.zeros_like(amat))
        s.edge = s.b[0:1, :] if s.rev else s.b[c - 1:c, :]
        s.qi = (s.q * jnp.exp(s.b)).astype(BF16)
        s.kl = (s.k * jnp.exp(s.edge - s.b)).astype(BF16)


def _hg_intra_stage(streams):
    for s in streams:
        s.o = s.o + _dot(s.amat, s.vb)


def _hg_state_stage(streams):
    for s in streams:
        st = s.get_state()
        s.o = s.o + _dot_nt(s.qi, st.astype(BF16))
        s.put_state(jnp.exp(s.edge) * st + _dot_tn(s.vb, s.kl))


def _hgrn2_kernel(q_ref, ff_ref, fb_ref, v_ref, g_ref, lb_ref, nw_ref, out_ref, of_ref, ob_ref, st_ref, qs_ref,
                  pb_ref, po_ref, pe_ref):
    L, dk = q_ref.shape
    c = HG_CHUNK
    nc = L // c

    lbp = lb_ref[...]
    mx = jnp.max(lbp, axis=0, keepdims=True)
    ex = jnp.exp(lbp - mx)
    lb = ex[0:1, :] / jnp.sum(ex, axis=0, keepdims=True)

    st_ref[...] = jnp.zeros_like(st_ref)
    qs_ref[...] = _silu(q_ref[...])

    per_step = HG_STREAM_CHUNKS
    n_steps = nc // per_step

    def shells(j):
        out = []
        for i in range(per_step):
            for rev in (False, True):
                s = _HgStream()
                chunk = (nc - 1 - j * per_step - i) if rev else (j * per_step + i)
                s.t0 = chunk * c if isinstance(chunk, int) else pl.multiple_of(chunk * c, c)
                s.rev = rev
                s.slot = 1 if rev else 0
                s.get_state = functools.partial(lambda slot: st_ref[slot], s.slot)
                s.put_state = functools.partial(st_ref.__setitem__, s.slot)
                out.append(s)
        return out

    def front(j):
        streams = shells(j)
        for s in streams:
            sig = _sigmoid((fb_ref if s.rev else ff_ref)[pl.ds(s.t0, c), :])
            s.f = lb + (1.0 - lb) * sig
            s.k = (1.0 - lb) * (1.0 - sig)
            s.logf = jnp.log(s.f)
            s.q = qs_ref[pl.ds(s.t0, c), :]
            s.v = v_ref[pl.ds(s.t0, c), :]
        return streams

    def save(streams):
        for i, s in enumerate(streams):
            pb_ref[i, 0] = s.amat
            pb_ref[i, 1] = s.vb
            pb_ref[i, 2] = s.qi
            pb_ref[i, 3] = s.kl
            po_ref[i] = s.o
            pe_ref[i] = jnp.broadcast_to(s.edge, (SUBLANES, dk))

    def load(j):
        streams = shells(j)
        for i, s in enumerate(streams):
            s.amat, s.vb, s.qi, s.kl = pb_ref[i, 0], pb_ref[i, 1], pb_ref[i, 2], pb_ref[i, 3]
            s.o = po_ref[i]
            s.edge = pe_ref[i, 0:1, :]
        return streams

    def emit(streams):
        for s in streams:
            (ob_ref if s.rev else of_ref)[pl.ds(s.t0, c), :] = s.o

    first = front(0)
    _hg_cumsum_stage(first)
    _hg_levels_stage(first)
    save(first)

    def body(j, carry):
        old = load(j - 1)
        new = front(j)
        _hg_intra_stage(old)
        _hg_state_stage(old[:2])
        _hg_cumsum_stage(new)
        _hg_state_stage(old[2:])
        _hg_levels_stage(new)
        emit(old)
        save(new)
        return carry

    lax.fori_loop(1, n_steps, body, 0)

    last = load(n_steps - 1)
    _hg_intra_stage(last)
    _hg_state_stage(last)
    emit(last)

    o = of_ref[...] + ob_ref[...]
    o = o * lax.rsqrt(jnp.mean(o * o, axis=1, keepdims=True) + EPS)
    y = o * nw_ref[...] * _silu(g_ref[...])
    out_ref[...] = y.astype(out_ref.dtype)


def _hgrn2(proj, hg_lb, norm_w, B, L):
    dk = HG_D
    H = HG_HEADS
    base = 4 * ML_WIDTH // dk
    return pl.pallas_call(
        _hgrn2_kernel,
        grid=(B, H),
        in_specs=[
            pl.BlockSpec((L, dk), lambda b, h: (b, base + h)),
            pl.BlockSpec((L, dk), lambda b, h: (b, base + H + h)),
            pl.BlockSpec((L, dk), lambda b, h: (b, base + 2 * H + h)),
            pl.BlockSpec((L, dk), lambda b, h: (b, base + 3 * H + h)),
            pl.BlockSpec((L, dk), lambda b, h: (b, base + 4 * H + h)),
            pl.BlockSpec((2, dk), lambda b, h: (0, h)),
            pl.BlockSpec((1, dk), lambda b, h: (0, h)),
        ],
        out_specs=pl.BlockSpec((L, dk), lambda b, h: (b, h)),
        out_shape=jax.ShapeDtypeStruct((B * L, HG_WIDTH), BF16),
        scratch_shapes=[
            pltpu.VMEM((L, dk), F32), pltpu.VMEM((L, dk), F32),
            pltpu.VMEM((2, dk, dk), F32),
            pltpu.VMEM((L, dk), F32),
            pltpu.VMEM((2 * HG_STREAM_CHUNKS, 4, HG_CHUNK, dk), BF16),
            pltpu.VMEM((2 * HG_STREAM_CHUNKS, HG_CHUNK, dk), F32),
            pltpu.VMEM((2 * HG_STREAM_CHUNKS, SUBLANES, dk), F32),
        ],
        compiler_params=_cparams(("arbitrary", "arbitrary")),
        name="hgrn2",
    )(proj, proj, proj, proj, proj, hg_lb, norm_w)


def _outproj_kernel(x_ref, ml_ref, hg_ref, eg_ref, eb_ref, wo_ref, g1_ref, b1_ref, wrh_ref, wrl_ref,
                    x1_ref, lgt_ref, ya_ref, yb_ref):
    i = pl.program_id(0)
    half = ml_ref.shape[1]
    n_exp = lgt_ref.shape[0]

    @pl.when(i == 0)
    def _():
        yb_ref[...] = jnp.zeros_like(yb_ref)

    def step(y_new, y_old):
        mix = _dot(ml_ref[...], wo_ref[0:half, :]) + _dot(hg_ref[...], wo_ref[half:, :])
        y_new[...] = ALPHA * _layer_norm(x_ref[...], eg_ref[...], eb_ref[...]) + mix
        x1 = _layer_norm(y_old[...], g1_ref[...], b1_ref[...])
        x1_ref[...] = x1
        xh, xl = _split_bf16(x1)
        lgt = _dot_nt(wrh_ref[...], xh) + _dot_nt(wrh_ref[...], xl) + _dot_nt(wrl_ref[...], xh)
        lgt_ref[...] = lgt[0:n_exp, :]

    @pl.when(lax.rem(i, 2) == 0)
    def _():
        step(ya_ref, yb_ref)

    @pl.when(lax.rem(i, 2) == 1)
    def _():
        step(yb_ref, ya_ref)


def _outproj(x, ml, hg, eg, eb, wo, g1, b1, wrt_hi, wrt_lo, n_exp, tm=256):
    n, d = x.shape
    n_tiles = n // tm
    row = lambda i: (jnp.minimum(i, n_tiles - 1), 0)
    prev_row = lambda i: (jnp.maximum(i - 1, 0), 0)
    fixed = lambda i: (0, 0)
    return pl.pallas_call(
        _outproj_kernel,
        grid=(n_tiles + 1,),
        in_specs=[
            pl.BlockSpec((tm, d), row),
            pl.BlockSpec((tm, ML_WIDTH), row),
            pl.BlockSpec((tm, HG_WIDTH), row),
            pl.BlockSpec((1, d), fixed),
            pl.BlockSpec((1, d), fixed),
            pl.BlockSpec((d, d), fixed),
            pl.BlockSpec((1, d), fixed),
            pl.BlockSpec((1, d), fixed),
            pl.BlockSpec((LANES, d), fixed),
            pl.BlockSpec((LANES, d), fixed),
        ],
        out_specs=[pl.BlockSpec((tm, d), prev_row),
                   pl.BlockSpec((n_exp, tm), lambda i: (0, jnp.maximum(i - 1, 0)))],
        out_shape=[jax.ShapeDtypeStruct((n, d), F32), jax.ShapeDtypeStruct((n_exp, n), F32)],
        scratch_shapes=[pltpu.VMEM((tm, d), F32), pltpu.VMEM((tm, d), F32)],
        compiler_params=_cparams(("arbitrary",)),
        name="outproj",
    )(x, ml, hg, eg, eb, wo, g1, b1, wrt_hi, wrt_lo)


def _pad_cols(w, width=LANES):
    return jnp.pad(w, ((0, 0), (0, width - w.shape[1])))


def _token_mixer_stage(x, emb_ln_g, emb_ln_b, w_in, conv_w, ml_igate_b, ml_fgate_b, ml_norm_w,
                       hg_lb, hg_norm_w, w_out, ln1_g, ln1_b, w_router, B, L):
    d = D_MODEL
    g0 = 4 * ML_WIDTH
    w_main = jnp.concatenate([w_in[:, :g0], w_in[:, g0 + 16:]], axis=1).astype(BF16)
    wg = w_in[:, g0:g0 + 16].reshape(d, 4, ML_HEADS).transpose(0, 2, 1).reshape(d, 16)
    wg_both = jnp.concatenate(_split_bf16(_pad_cols(wg)), axis=1)
    eg = emb_ln_g.reshape(1, d)
    eb = emb_ln_b.reshape(1, d)
    proj, gate_tok, gate_t = _ln_proj(x, eg, eb, w_main, wg_both)

    gate_row = gate_t.reshape(ML_HEADS, 4, B, L).transpose(2, 0, 1, 3)
    bias = jnp.concatenate([ml_igate_b[0], ml_igate_b[1], ml_fgate_b[0], ml_fgate_b[1]]).astype(F32)
    ml = _mlstm(proj, conv_w, gate_tok, gate_row, bias, ml_norm_w.reshape(1, ML_WIDTH), B, L)
    hg = _hgrn2(proj, hg_lb, hg_norm_w.reshape(1, HG_WIDTH), B, L)

    wrt_hi, wrt_lo = _split_bf16(_pad_cols(w_router).T)
    return _outproj(x, ml, hg, eg, eb, w_out.astype(BF16), ln1_g.reshape(1, d), ln1_b.reshape(1, d),
                    wrt_hi, wrt_lo, w_router.shape[1])


def _excl_token_cumsum(mask):
    E, R, ln = mask.shape
    mf = jnp.where(mask, 1.0, 0.0)
    mb = mf.astype(BF16)
    upper = jnp.where(lax.broadcasted_iota(I32, (ln, ln), 0) <= lax.broadcasted_iota(I32, (ln, ln), 1),
                      1.0, 0.0).astype(BF16)
    lower = jnp.where(lax.broadcasted_iota(I32, (R, R), 1) < lax.broadcasted_iota(I32, (R, R), 0),
                      1.0, 0.0).astype(BF16)
    ones = jnp.ones((ln, ln), BF16)
    within = _dot(mb.reshape(E * R, ln), upper).reshape(E, R, ln)
    rows = jnp.stack([_dot(_dot(lower, mb[e]).astype(BF16), ones) for e in range(E)], axis=0)
    return within - mf + rows


def _select_kernel(lg_ref, pos_ref, posm_ref, wts_ref, *, cap):
    E = lg_ref.shape[0]
    lg = lg_ref[...]
    mx = jnp.max(lg, axis=0, keepdims=True)
    ex = jnp.exp(lg - mx)
    aff = ex / jnp.sum(ex, axis=0, keepdims=True)

    def count(mask):
        ones = jnp.where(mask, 1.0, 0.0)
        return jnp.sum(jnp.sum(ones, axis=2, keepdims=True), axis=1, keepdims=True)

    def body(i, tbits):
        cand = tbits | lax.shift_left(jnp.int32(1), 30 - i)
        cnt = count(aff >= lax.bitcast_convert_type(cand, F32))
        return jnp.where(cnt >= cap, cand, tbits)

    tbits = lax.fori_loop(0, 31, body, jnp.zeros((E, 1, 1), I32))
    thr = lax.bitcast_convert_type(tbits, F32)
    nxt = lax.bitcast_convert_type(tbits + 1, F32)
    above = aff >= nxt
    band = jnp.logical_and(aff >= thr, jnp.logical_not(above))
    need = cap - count(above)
    sel = jnp.logical_or(above, jnp.logical_and(band, _excl_token_cumsum(band) < need))
    pos = _excl_token_cumsum(sel).astype(I32)
    pos_ref[...] = pos
    posm_ref[...] = jnp.where(sel, pos, -1)
    wts_ref[...] = jnp.where(sel, aff, 0.0)


def _select(lg_t, cap):
    E, R, ln = lg_t.shape
    full = pl.BlockSpec((E, R, ln), lambda i: (0, 0, 0))
    return pl.pallas_call(
        functools.partial(_select_kernel, cap=cap),
        grid=(1,),
        in_specs=[full],
        out_specs=[full, full, full],
        out_shape=[jax.ShapeDtypeStruct((E, R, ln), I32), jax.ShapeDtypeStruct((E, R, ln), I32),
                   jax.ShapeDtypeStruct((E, R, ln), F32)],
        compiler_params=_cparams(("arbitrary",)),
        name="select",
    )(lg_t)


TOK_TILE = 256
SLOT_CHUNK = 64


def _pack_bf16_pairs(x):
    h = x.shape[1] // 2
    lo = lax.shift_right_logical(lax.bitcast_convert_type(x[:, :h], U32), jnp.uint32(16))
    hi = lax.bitcast_convert_type(x[:, h:], U32) & jnp.uint32(0xFFFF0000)
    return hi | lo


def _unpack_bf16_pairs(w):
    lo = lax.bitcast_convert_type(lax.shift_left(w, jnp.uint32(16)), F32).astype(BF16)
    hi = lax.bitcast_convert_type(w & jnp.uint32(0xFFFF0000), F32).astype(BF16)
    return lo, hi


def _dispatch_kernel(off_ref, base_ref, cnt_ref, xa_ref, xb_ref, pos_ref, xe_ref,
                     x16_ref, res_ref, stage_ref, ostage_ref, carry_ref, sem, osem, *, tiles_a, cap_total, pad):
    E = N_EXPERTS
    CH = SLOT_CHUNK
    SUB = SUBLANES
    tc = xa_ref.shape[0]
    t = pl.program_id(0)
    nt = pl.num_programs(0)
    slot = lax.rem(t, 2)

    @pl.when(t < tiles_a)
    def _():
        x16_ref[...] = xa_ref[...].astype(BF16)

    @pl.when(t >= tiles_a)
    def _():
        x16_ref[...] = xb_ref[...].astype(BF16)

    xb = x16_ref[...]

    @pl.when(t == 0)
    def _():
        carry_ref[...] = jnp.zeros_like(carry_ref)

    base = [base_ref[t * E + e] for e in range(E)]
    cnt = [cnt_ref[t * E + e] for e in range(E)]
    al = [pl.multiple_of((b // SUB) * SUB, SUB) for b in base]
    first = [off_ref[t * E + e] - (base[e] - al[e]) for e in range(E)]

    def onehot(e, start, rows):
        kio = lax.broadcasted_iota(I32, (rows, tc), 0)
        rel = pos_ref[e:e + 1, :] - (first[e] + start)
        return jnp.where(rel == kio, 1.0, 0.0).astype(BF16)

    ot = jnp.concatenate([onehot(e, 0, CH) for e in range(E)], axis=0)
    res_ref[...] = _dot(ot, xb)
    for e in range(E):
        res_ref[e * CH:e * CH + SUB, :] += carry_ref[e * SUB:(e + 1) * SUB, :]
    stage_ref[slot] = _pack_bf16_pairs(res_ref[...])

    nxt = [((base[e] + cnt[e]) // SUB) * SUB - al[e] for e in range(E)]
    oc = jnp.concatenate([onehot(e, nxt[e], SUB) for e in range(E)], axis=0)
    new_carry = _dot(oc, xb)
    for e in range(E):
        keep = jnp.where(nxt[e] == 0, 1.0, 0.0)
        carry_ref[e * SUB:(e + 1) * SUB, :] = new_carry[e * SUB:(e + 1) * SUB, :] + keep * carry_ref[e * SUB:(e + 1) * SUB, :]

    def main_copy(s, e, row):
        return pltpu.make_async_copy(stage_ref.at[s, pl.ds(e * CH, CH)], xe_ref.at[pl.ds(row, CH)], sem.at[s])

    @pl.when(t > 0)
    def _():
        for e in range(E):
            main_copy(1 - slot, e, 0).wait()

    for e in range(E):
        main_copy(slot, e, al[e]).start()

    for e in range(E):
        nch = (base[e] - al[e] + cnt[e] + CH - 1) // CH

        def body(c, carry, e=e):
            ostage_ref[...] = _pack_bf16_pairs(_dot(onehot(e, c * CH, CH), xb))
            row = pl.multiple_of(al[e] + c * CH, SUB)
            cp = pltpu.make_async_copy(ostage_ref, xe_ref.at[pl.ds(row, CH)], osem)
            cp.start()
            cp.wait()
            return carry

        lax.fori_loop(1, nch, body, 0)

    @pl.when(t == nt - 1)
    def _():
        for e in range(E):
            main_copy(slot, e, 0).wait()
        ostage_ref[...] = jnp.zeros_like(ostage_ref)
        fills = [pltpu.make_async_copy(ostage_ref, xe_ref.at[pl.ds(e * (cap_total + pad) + cap_total + j * CH, CH)], osem)
                 for e in range(E) for j in range(pad // CH)]
        for cp in fills:
            cp.start()
        for cp in fills:
            cp.wait()


def _dispatch(off, base, cnt, x1_a, x1_b, pos_rows, cap_total, pad):
    d = x1_a.shape[1]
    E = N_EXPERTS
    tc = TOK_TILE
    tiles_a = x1_a.shape[0] // tc
    tiles_b = x1_b.shape[0] // tc
    grid_spec = pltpu.PrefetchScalarGridSpec(
        num_scalar_prefetch=3,
        grid=(tiles_a + tiles_b,),
        in_specs=[
            pl.BlockSpec((tc, d), lambda i, *_: (jnp.minimum(i, tiles_a - 1), 0)),
            pl.BlockSpec((tc, d), lambda i, *_: (jnp.maximum(i - tiles_a, 0), 0)),
            pl.BlockSpec((E, tc), lambda i, *_: (0, i)),
        ],
        out_specs=pl.BlockSpec(memory_space=pl.ANY),
        scratch_shapes=[
            pltpu.VMEM((tc, d), BF16),
            pltpu.VMEM((E * SLOT_CHUNK, d), F32),
            pltpu.VMEM((2, E * SLOT_CHUNK, d // 2), U32),
            pltpu.VMEM((SLOT_CHUNK, d // 2), U32),
            pltpu.VMEM((E * SUBLANES, d), F32),
            pltpu.SemaphoreType.DMA((2,)),
            pltpu.SemaphoreType.DMA(()),
        ],
    )
    return pl.pallas_call(
        functools.partial(_dispatch_kernel, tiles_a=tiles_a, cap_total=cap_total, pad=pad),
        grid_spec=grid_spec,
        out_shape=jax.ShapeDtypeStruct((E * (cap_total + pad), d // 2), U32),
        compiler_params=_cparams(("arbitrary",)),
        name="dispatch",
    )(off, base, cnt, x1_a, x1_b, pos_rows)


def _ffn_hidden_kernel(xe_ref, wg_ref, wu_ref, hid_ref, wg16_ref, wu16_ref):
    @pl.when(pl.program_id(2) == 0)
    def _():
        wg16_ref[...] = wg_ref[0].astype(BF16)
        wu16_ref[...] = wu_ref[0].astype(BF16)

    lo, hi = _unpack_bf16_pairs(xe_ref[0])
    h = lo.shape[1]
    gate = _dot(lo, wg16_ref[0:h, :]) + _dot(hi, wg16_ref[h:, :])
    up = _dot(lo, wu16_ref[0:h, :]) + _dot(hi, wu16_ref[h:, :])
    hid_ref[0] = (_silu(gate) * up).astype(hid_ref.dtype)


def _ffn_down_kernel(hid_ref, wd_ref, out_ref, wd16_ref):
    @pl.when(pl.program_id(2) == 0)
    def _():
        wd16_ref[...] = wd_ref[0].astype(BF16)

    out_ref[0] = _dot(hid_ref[0], wd16_ref[...]).astype(out_ref.dtype)


def _ffn(xe, wg, wu, wd, cap_total, tf=512, tn=1024):
    E, _, dh = xe.shape
    d = 2 * dh
    ff = wg.shape[2]
    tm = next(t for t in (1024, 512, 256, 128) if cap_total % t == 0)
    hid = pl.pallas_call(
        _ffn_hidden_kernel,
        grid=(E, ff // tf, cap_total // tm),
        in_specs=[
            pl.BlockSpec((1, tm, dh), lambda e, f, i: (e, i, 0)),
            pl.BlockSpec((1, d, tf), lambda e, f, i: (e, 0, f)),
            pl.BlockSpec((1, d, tf), lambda e, f, i: (e, 0, f)),
        ],
        out_specs=pl.BlockSpec((1, tm, tf), lambda e, f, i: (e, i, f)),
        out_shape=jax.ShapeDtypeStruct((E, cap_total, ff), BF16),
        scratch_shapes=[pltpu.VMEM((d, tf), BF16), pltpu.VMEM((d, tf), BF16)],
        compiler_params=_cparams(("arbitrary", "arbitrary", "arbitrary")),
        name="ffn_hidden",
    )(xe, wg, wu)
    return pl.pallas_call(
        _ffn_down_kernel,
        grid=(E, d // tn, cap_total // tm),
        in_specs=[
            pl.BlockSpec((1, tm, ff), lambda e, c, i: (e, i, 0)),
            pl.BlockSpec((1, ff, tn), lambda e, c, i: (e, 0, c)),
        ],
        out_specs=pl.BlockSpec((1, tm, tn), lambda e, c, i: (e, i, c)),
        out_shape=jax.ShapeDtypeStruct((E, cap_total, d), BF16),
        scratch_shapes=[pltpu.VMEM((ff, tn), BF16)],
        compiler_params=_cparams(("arbitrary", "arbitrary", "arbitrary")),
        name="ffn_down",
    )(hid, wd)


def _combine_kernel(base_ref, cnt_ref, x1_ref, grow_ref, w_ref, g2_ref, b2_ref, ye_ref, out_ref,
                    buf_ref, obuf_ref, acc_ref, sem, osem, *, rows_total):
    E = N_EXPERTS
    CH = SLOT_CHUNK
    tc = x1_ref.shape[0]
    t = pl.program_id(0)
    nt = pl.num_programs(0)
    slot = lax.rem(t, 2)
    last = rows_total - CH
    ALIGN = 2 * SUBLANES

    def window(step, e):
        al = (base_ref[step * E + e] // ALIGN) * ALIGN
        return al, pl.multiple_of(jnp.minimum(al, last), ALIGN)

    def chunk_copy(s, e, start):
        return pltpu.make_async_copy(ye_ref.at[pl.ds(start, CH)], buf_ref.at[s, pl.ds(e * CH, CH)], sem.at[s, e])

    @pl.when(t == 0)
    def _():
        for e in range(E):
            chunk_copy(0, e, window(0, e)[1]).start()

    @pl.when(t + 1 < nt)
    def _():
        for e in range(E):
            chunk_copy(1 - slot, e, window(t + 1, e)[1]).start()

    kio = lax.broadcasted_iota(I32, (CH, tc), 0)

    def weights(e, start, lo):
        grow = grow_ref[e:e + 1, :]
        hit = jnp.logical_and(grow - start == kio, grow >= lo)
        return jnp.where(hit, w_ref[e:e + 1, :], 0.0).astype(BF16)

    at = jnp.concatenate([weights(e, window(t, e)[1], 0) for e in range(E)], axis=0)
    for e in range(E):
        chunk_copy(slot, e, 0).wait()
    acc_ref[...] = ALPHA * x1_ref[...] + _dot_tn(at, buf_ref[slot])

    for e in range(E):
        al = window(t, e)[0]
        nch = (base_ref[t * E + e] - al + cnt_ref[t * E + e] + CH - 1) // CH

        def body(c, carry, e=e, al=al):
            lo = al + c * CH
            start = pl.multiple_of(jnp.minimum(lo, last), ALIGN)
            cp = pltpu.make_async_copy(ye_ref.at[pl.ds(start, CH)], obuf_ref, osem)
            cp.start()
            cp.wait()
            acc_ref[...] += _dot_tn(weights(e, start, lo), obuf_ref[...])
            return carry

        lax.fori_loop(1, nch, body, 0)

    out_ref[...] = _layer_norm(acc_ref[...], g2_ref[...], b2_ref[...])


def _combine(base, cnt, x1, grow_rows, w_rows, g2, b2, ye, rows_total):
    n, d = x1.shape
    E = N_EXPERTS
    tc = TOK_TILE
    grid_spec = pltpu.PrefetchScalarGridSpec(
        num_scalar_prefetch=2,
        grid=(n // tc,),
        in_specs=[
            pl.BlockSpec((tc, d), lambda i, *_: (i, 0)),
            pl.BlockSpec((E, tc), lambda i, *_: (0, i)),
            pl.BlockSpec((E, tc), lambda i, *_: (0, i)),
            pl.BlockSpec((1, d), lambda i, *_: (0, 0)),
            pl.BlockSpec((1, d), lambda i, *_: (0, 0)),
            pl.BlockSpec(memory_space=pl.ANY),
        ],
        out_specs=pl.BlockSpec((tc, d), lambda i, *_: (i, 0)),
        scratch_shapes=[
            pltpu.VMEM((2, E * SLOT_CHUNK, d), BF16),
            pltpu.VMEM((SLOT_CHUNK, d), BF16),
            pltpu.VMEM((tc, d), F32),
            pltpu.SemaphoreType.DMA((2, E)),
            pltpu.SemaphoreType.DMA(()),
        ],
    )
    return pl.pallas_call(
        functools.partial(_combine_kernel, rows_total=rows_total),
        grid_spec=grid_spec,
        out_shape=jax.ShapeDtypeStruct((n, d), F32),
        compiler_params=_cparams(("arbitrary",)),
        name="combine",
    )(base, cnt, x1, grow_rows, w_rows, g2, b2, ye)


def _routing_tables(pos, posm, wts, cap, slot0, cap_total, pad):
    E = N_EXPERTS
    n = pos.shape[1] * pos.shape[2]
    tiles = n // TOK_TILE
    eidx = jnp.arange(E, dtype=I32)
    off = pos.reshape(E, n)[:, ::TOK_TILE].T
    cnt = jnp.concatenate([off[1:], jnp.full((1, E), cap, I32)], axis=0) - off
    base_x = off + eidx[None, :] * (cap_total + pad) + slot0
    base_y = off + eidx[None, :] * cap_total + slot0
    pos_rows = posm.reshape(E, n)
    grow_rows = jnp.where(pos_rows >= 0, pos_rows + eidx[:, None] * cap_total + slot0, -1)
    flat = lambda a: a.reshape(tiles * E)
    return flat(off), flat(base_x), flat(base_y), flat(cnt), pos_rows, grow_rows, wts.reshape(E, n)


def kernel(x_prompt, x_sample, emb_ln_g, emb_ln_b, w_in, conv_w, ml_igate_b, ml_fgate_b, ml_norm_w, hg_lb, hg_norm_w, w_out, ln1_g, ln1_b, w_router, w_gate, w_up, w_down, ln2_g, ln2_b):
    E = N_EXPERTS
    d = D_MODEL
    groups = (x_prompt, x_sample)
    caps = [EC_FACTOR * x.shape[0] * x.shape[1] // E for x in groups]
    cap_total = sum(caps)
    pad = 2 * SLOT_CHUNK
    assert all(c % (2 * SUBLANES) == 0 for c in caps)

    staged = []
    tables = []
    slot0 = 0
    for x, cap in zip(groups, caps):
        B, L, _ = x.shape
        n = B * L
        x1, lg = _token_mixer_stage(x.reshape(n, d), emb_ln_g, emb_ln_b, w_in[0], conv_w[0], ml_igate_b[0],
                                    ml_fgate_b[0], ml_norm_w[0], hg_lb, hg_norm_w[0], w_out[0], ln1_g[0],
                                    ln1_b[0], w_router[0], B, L)
        lg_t = lg.reshape(E, n // LANES, LANES)
        pos, posm, wts = _select(lg_t, cap)
        off, base_x, base_y, cnt, pos_rows, grow_t, w_t = _routing_tables(pos, posm, wts, cap, slot0, cap_total, pad)
        tables.append((off, base_x, cnt, pos_rows))
        staged.append((x1, base_y, cnt, grow_t, w_t, (B, L)))
        slot0 += cap

    off, base_x, cnt, pos_rows = (jnp.concatenate(parts, axis=-1) for parts in zip(*tables))
    xe = _dispatch(off, base_x, cnt, staged[0][0], staged[1][0], pos_rows, cap_total, pad)

    ye = _ffn(xe.reshape(E, cap_total + pad, d // 2), w_gate[0], w_up[0], w_down[0], cap_total)
    ye = ye.reshape(E * cap_total, d)

    g2 = ln2_g[0].reshape(1, d)
    b2 = ln2_b[0].reshape(1, d)
    outs = []
    for x1, base_y, cnt, grow_t, w_t, (B, L) in staged:
        y = _combine(base_y, cnt, x1, grow_t, w_t, g2, b2, ye, E * cap_total)
        outs.append(y.reshape(B, L, d))
    return tuple(outs)
```

```python
import functools

import jax
import jax.numpy as jnp
from jax import lax
from jax.experimental import pallas as pl
from jax.experimental.pallas import tpu as pltpu

F32 = jnp.float32
BF16 = jnp.bfloat16
I32 = jnp.int32
U32 = jnp.uint32

D_MODEL = 2048
ML_WIDTH = 1024
ML_HEADS = 4
ML_DH = 256
HG_WIDTH = 1024
HG_HEADS = 8
HG_D = 128
N_EXPERTS = 16
EC_FACTOR = 2
EXPERT_FF = 2048
DEPTH = 1
ALPHA = (2.0 * DEPTH) ** 0.25
EPS = 1e-5

LANES = 128
SUBLANES = 8
N_GATE_COLS = 16
VMEM_LIMIT = 56 * 1024 * 1024

ML_CHUNK = 256
HG_CHUNK = 128
HG_STREAM_CHUNKS = 8


def _cparams(sem):
    return pltpu.CompilerParams(dimension_semantics=sem, vmem_limit_bytes=VMEM_LIMIT)


def _split_bf16(x):
    hi = x.astype(BF16)
    lo = (x - hi.astype(F32)).astype(BF16)
    return hi, lo


def _layer_norm(x, g, b):
    mu = jnp.mean(x, axis=-1, keepdims=True)
    xc = x - mu
    var = jnp.mean(xc * xc, axis=-1, keepdims=True)
    return xc * lax.rsqrt(var + EPS) * g + b


def _sigmoid(x):
    return 1.0 / (1.0 + jnp.exp(-x))


def _silu(x):
    return x * _sigmoid(x)


def _log_sigmoid(x):
    return jnp.minimum(x, 0.0) - jnp.log1p(jnp.exp(-jnp.abs(x)))


def _dot(a, b):
    return jnp.dot(a, b, preferred_element_type=F32)


def _dot_nt(a, b):
    return lax.dot_general(a, b, (((1,), (1,)), ((), ())), preferred_element_type=F32)


def _dot_tn(a, b):
    return lax.dot_general(a, b, (((0,), (0,)), ((), ())), preferred_element_type=F32)


def _ln_proj_kernel(x_ref, g_ref, b_ref, w_ref, wg_ref, proj_ref, gate_ref, gatet_ref,
                    hha_ref, hla_ref, hhb_ref, hlb_ref, *, n_slices):
    i = pl.program_id(0)
    j = pl.program_id(1)
    rs = x_ref.shape[0]

    @pl.when(jnp.logical_and(i == 0, j == 0))
    def _():
        hhb_ref[...] = jnp.zeros_like(hhb_ref)
        hlb_ref[...] = jnp.zeros_like(hlb_ref)

    def step(hh_new, hl_new, hh_old, hl_old):
        @pl.when(j == 0)
        def _():
            both = _dot(hh_old[...], wg_ref[...])
            gates = both[:, :LANES] + both[:, LANES:] + _dot(hl_old[...], wg_ref[:, :LANES])
            gate_ref[...] = gates
            gatet_ref[...] = gates.T[0:gatet_ref.shape[0], :]

        proj_ref[...] = _dot(hh_old[...], w_ref[...]).astype(proj_ref.dtype)
        r0 = pl.multiple_of(jnp.minimum(j, n_slices - 1) * rs, rs)
        hh, hl = _split_bf16(_layer_norm(x_ref[...], g_ref[...], b_ref[...]))
        hh_new[pl.ds(r0, rs), :] = hh
        hl_new[pl.ds(r0, rs), :] = hl

    @pl.when(lax.rem(i, 2) == 0)
    def _():
        step(hha_ref, hla_ref, hhb_ref, hlb_ref)

    @pl.when(lax.rem(i, 2) == 1)
    def _():
        step(hhb_ref, hlb_ref, hha_ref, hla_ref)


def _ln_proj(x, g, b, w_main, wg_both, tm=1024, tn=1024):
    n, d = x.shape
    tm = min(tm, n)
    ncol = w_main.shape[1] // tn
    n_tiles = n // tm
    n_slices = ncol - 1
    rs = tm // n_slices
    assert tm % n_slices == 0 and rs % SUBLANES == 0
    prev_row = lambda i, j: jnp.maximum(i - 1, 0)
    return pl.pallas_call(
        functools.partial(_ln_proj_kernel, n_slices=n_slices),
        grid=(n_tiles + 1, ncol),
        in_specs=[
            pl.BlockSpec((rs, d), lambda i, j: (jnp.minimum(i, n_tiles - 1) * n_slices
                                                + jnp.minimum(j, n_slices - 1), 0)),
            pl.BlockSpec((1, d), lambda i, j: (0, 0)),
            pl.BlockSpec((1, d), lambda i, j: (0, 0)),
            pl.BlockSpec((d, tn), lambda i, j: (0, j)),
            pl.BlockSpec((d, 2 * LANES), lambda i, j: (0, 0)),
        ],
        out_specs=[
            pl.BlockSpec((tm, tn), lambda i, j: (prev_row(i, j), jnp.where(i > 0, j, 0))),
            pl.BlockSpec((tm, LANES), lambda i, j: (prev_row(i, j), 0)),
            pl.BlockSpec((N_GATE_COLS, tm), lambda i, j: (0, prev_row(i, j))),
        ],
        out_shape=[
            jax.ShapeDtypeStruct((n, ncol * tn), F32),
            jax.ShapeDtypeStruct((n, LANES), F32),
            jax.ShapeDtypeStruct((N_GATE_COLS, n), F32),
        ],
        scratch_shapes=[pltpu.VMEM((tm, d), BF16)] * 4,
        compiler_params=_cparams(("arbitrary", "arbitrary")),
        name="ln_proj",
    )(x, g, b, w_main, wg_both)


class _MlStream:
    pass


def _mlstm_chunks(streams):
    c = streams[0].qb.shape[0]
    tt = lax.broadcasted_iota(I32, (c, c), 0)
    ss = lax.broadcasted_iota(I32, (c, c), 1)

    for s in streams:
        causal = (ss >= tt) if s.rev else (ss <= tt)
        b_col = jnp.sum(jnp.where(causal, s.f_row, 0.0), axis=1, keepdims=True)
        anti = (tt >= ss) if s.rev else (tt <= ss)
        b_row = jnp.sum(jnp.where(anti, s.f_col, 0.0), axis=0, keepdims=True)
        total = jnp.sum(s.f_row, axis=1, keepdims=True)

        dmat = jnp.where(causal, b_col - b_row + s.i_row, -jnp.inf)
        a_inter = b_col + s.m_state
        s.m_t = jnp.maximum(a_inter, jnp.max(dmat, axis=1, keepdims=True))
        s.w_inter = jnp.exp(a_inter - s.m_t)
        s.p = jnp.exp(dmat - s.m_t)
        s.qk = _dot_nt(s.qb, s.kb)
        s.qc = _dot(s.qb, s.c_state.astype(BF16))

        g_col = total - b_col + s.i_col
        g_row = total - b_row + s.i_row
        s.m_new = jnp.maximum(total + s.m_state, jnp.max(g_row, axis=1, keepdims=True))
        s.decay = jnp.exp(total + s.m_state - s.m_new)
        ks = s.kb.astype(F32) * jnp.exp(g_col - s.m_new)
        s.kv = _dot_tn(ks.astype(BF16), s.vb)
        s.n_new = s.decay * s.n_state + jnp.sum(ks, axis=0, keepdims=True)

    for s in streams:
        s.s = s.qk * s.p
        s.sv = _dot(s.s.astype(BF16), s.vb)

    for s in streams:
        num = s.w_inter * s.qc + s.sv
        qn = jnp.sum(s.qb.astype(F32) * s.n_state, axis=1, keepdims=True)
        den = s.w_inter * qn + jnp.sum(s.s, axis=1, keepdims=True)
        s.h = num / jnp.maximum(jnp.abs(den), jnp.exp(-s.m_t))
        s.c_new = s.decay * s.c_state + s.kv


def _mlstm_kernel(bias_ref, q_ref, k_ref, v_ref, o_ref, cwq_ref, cwk_ref, gt_ref, gr_ref, nw_ref,
                  out_ref, qc_ref, kc_ref, vc_ref, hf_ref, hb_ref, cs_ref):
    L, d = q_ref.shape

    lane_shift = lax.rem(LANES - 4 * pl.program_id(1), LANES)
    c = ML_CHUNK
    nc = L // c
    hd = pl.program_id(1)

    row = lax.broadcasted_iota(I32, (L, 1), 0)

    def conv_silu(x, w):
        xp = jnp.where(row == 0, 0.0, pltpu.roll(x, 1, 0))
        xn = jnp.where(row == L - 1, 0.0, pltpu.roll(x, L - 1, 0))
        return _silu(w[0:1, :] * xp + w[1:2, :] * x + w[2:3, :] * xn)

    qc_ref[...] = conv_silu(q_ref[...], cwq_ref[...]).astype(BF16)
    kc_ref[...] = (conv_silu(k_ref[...], cwk_ref[...]) * (ML_DH ** -0.5)).astype(BF16)
    vc_ref[...] = v_ref[...].astype(BF16)

    bi_f = bias_ref[hd]
    bi_b = bias_ref[ML_HEADS + hd]
    bf_f = bias_ref[2 * ML_HEADS + hd]
    bf_b = bias_ref[3 * ML_HEADS + hd]

    cs_ref[...] = jnp.zeros_like(cs_ref)

    def gates(t0):
        gc = pltpu.roll(gt_ref[pl.ds(t0, c), :], lane_shift, 1)[:, 0:4]
        gr = gr_ref[0, 0, :, pl.ds(t0, c)]
        return gc, gr

    def stream(t0, slot, rev, bias_i, bias_f, n_state, m_state):
        s = _MlStream()
        gc, gr = gates(t0)
        gi, gf = (1, 3) if rev else (0, 2)
        s.qb = qc_ref[pl.ds(t0, c), :]
        s.kb = kc_ref[pl.ds(t0, c), :]
        s.vb = vc_ref[pl.ds(t0, c), :]
        s.f_col = _log_sigmoid(gc[:, gf:gf + 1] + bias_f)
        s.f_row = _log_sigmoid(gr[gf:gf + 1, :] + bias_f)
        s.i_col = gc[:, gi:gi + 1] + bias_i
        s.i_row = gr[gi:gi + 1, :] + bias_i
        s.c_state = cs_ref[slot]
        s.n_state = n_state
        s.m_state = m_state
        s.rev = rev
        return s

    def body(j, carry):
        n_f, m_f, n_b, m_b = carry
        t0 = pl.multiple_of(j * c, c)
        t1 = pl.multiple_of((nc - 1 - j) * c, c)
        fwd = stream(t0, 0, False, bi_f, bf_f, n_f, m_f)
        bwd = stream(t1, 1, True, bi_b, bf_b, n_b, m_b)
        _mlstm_chunks([fwd, bwd])
        cs_ref[0] = fwd.c_new
        hf_ref[pl.ds(t0, c), :] = fwd.h
        cs_ref[1] = bwd.c_new
        hb_ref[pl.ds(t1, c), :] = bwd.h
        return fwd.n_new, fwd.m_new, bwd.n_new, bwd.m_new

    zn = jnp.zeros((1, d), F32)
    m0 = jnp.full((1, 1), -1e30, F32)
    lax.fori_loop(0, nc, body, (zn, m0, zn, m0))

    hm = hf_ref[...] + hb_ref[...]
    mu = jnp.mean(hm, axis=1, keepdims=True)
    hc = hm - mu
    var = jnp.mean(hc * hc, axis=1, keepdims=True)
    y = hc * lax.rsqrt(var + EPS) * nw_ref[...] * _sigmoid(o_ref[...])
    out_ref[...] = y.astype(out_ref.dtype)


def _mlstm(proj, conv_w, gate_tok, gate_row, bias, norm_w, B, L):
    d = ML_DH
    H = ML_HEADS
    grid_spec = pltpu.PrefetchScalarGridSpec(
        num_scalar_prefetch=1,
        grid=(B, H),
        in_specs=[
            pl.BlockSpec((L, d), lambda b, h, s: (b, h)),
            pl.BlockSpec((L, d), lambda b, h, s: (b, H + h)),
            pl.BlockSpec((L, d), lambda b, h, s: (b, 2 * H + h)),
            pl.BlockSpec((L, d), lambda b, h, s: (b, 3 * H + h)),
            pl.BlockSpec((3, d), lambda b, h, s: (0, h)),
            pl.BlockSpec((3, d), lambda b, h, s: (0, H + h)),
            pl.BlockSpec((L, LANES), lambda b, h, s: (b, 0)),
            pl.BlockSpec((1, 1, 4, L), lambda b, h, s: (b, h, 0, 0)),
            pl.BlockSpec((1, d), lambda b, h, s: (0, h)),
        ],
        out_specs=pl.BlockSpec((L, d), lambda b, h, s: (b, h)),
        scratch_shapes=[
            pltpu.VMEM((L, d), BF16), pltpu.VMEM((L, d), BF16), pltpu.VMEM((L, d), BF16),
            pltpu.VMEM((L, d), F32), pltpu.VMEM((L, d), F32),
            pltpu.VMEM((2, d, d), F32),
        ],
    )
    return pl.pallas_call(
        _mlstm_kernel,
        grid_spec=grid_spec,
        out_shape=jax.ShapeDtypeStruct((B * L, ML_WIDTH), BF16),
        compiler_params=_cparams(("arbitrary", "arbitrary")),
        name="mlstm",
    )(bias, proj, proj, proj, proj, conv_w, conv_w, gate_tok, gate_row, norm_w)


def _chunk_cumsum(x, rev):
    c, n = x.shape
    r = lax.broadcasted_iota(I32, (c, c), 0)
    s = lax.broadcasted_iota(I32, (c, c), 1)
    tri = jnp.where((s >= r) if rev else (s <= r), 1.0, 0.0).astype(BF16)
    hi = x.astype(BF16)
    r1 = x - hi.astype(F32)
    mid = r1.astype(BF16)
    lo = (r1 - mid.astype(F32)).astype(BF16)
    parts = _dot(tri, jnp.concatenate([hi, mid, lo], axis=1))
    return parts[:, :n] + parts[:, n:2 * n] + parts[:, 2 * n:]


class _HgStream:
    pass


def _hg_cumsum_stage(streams):
    for s in streams:
        s.b = _chunk_cumsum(s.logf, s.rev)
        s.vb = s.v.astype(BF16)
        s.o = jnp.sum(s.q * s.k, axis=1, keepdims=True) * s.v


def _hg_levels_stage(streams):
    c, dk = streams[0].q.shape
    row = lax.broadcasted_iota(I32, (c, dk), 0)
    tt = lax.broadcasted_iota(I32, (c, c), 0)
    ss = lax.broadcasted_iota(I32, (c, c), 1)

    for s in streams:
        strict = (ss > tt) if s.rev else (ss < tt)
        f_prev = pltpu.roll(s.f, 1, 0)
        f_next = pltpu.roll(s.f, c - 1, 0)
        amat = None
        m = c // 2
        while m >= 1:
            blk = 2 * m
            u = row & (blk - 1)
            tgt = (u < m) if s.rev else (u >= m)
            if m == 1:
                e = jnp.where(tgt, s.f, 1.0)
            elif m == 2:
                if s.rev:
                    e = jnp.where(u == 0, s.f * f_next, jnp.where(u == 1, s.f, jnp.where(u == 2, 1.0, f_prev)))
                else:
                    e = jnp.where(u == 0, f_next, jnp.where(u == 1, 1.0, jnp.where(u == 2, s.f, s.f * f_prev)))
            else:
                bb = s.b.reshape(c // blk, blk, dk)
                ref = bb[:, m:m + 1, :] if s.rev else bb[:, m - 1:m, :]
                e = jnp.exp(-jnp.abs(bb - ref)).reshape(c, dk)
            qk = (jnp.where(tgt, s.q, s.k) * e).astype(BF16)
            a = _dot_nt(qk, qk).astype(BF16)
            amat = a if amat is None else jnp.where((tt // blk) == (ss // blk), a, amat)
            m = m // 2
        s.amat = jnp.where(strict, amat, jnp.zeros_like(amat))
        s.edge = s.b[0:1, :] if s.rev else s.b[c - 1:c, :]
        s.qi = (s.q * jnp.exp(s.b)).astype(BF16)
        s.kl = (s.k * jnp.exp(s.edge - s.b)).astype(BF16)


def _hg_intra_stage(streams):
    for s in streams:
        s.o = s.o + _dot(s.amat, s.vb)


def _hg_state_stage(streams):
    for s in streams:
        st = s.get_state()
        s.o = s.o + _dot_nt(s.qi, st.astype(BF16))
        s.put_state(jnp.exp(s.edge) * st + _dot_tn(s.vb, s.kl))


def _hgrn2_kernel(q_ref, ff_ref, fb_ref, v_ref, g_ref, lb_ref, nw_ref, out_ref, of_ref, ob_ref, st_ref, qs_ref,
                  pb_ref, po_ref, pe_ref):
    L, dk = q_ref.shape
    c = HG_CHUNK
    nc = L // c

    lbp = lb_ref[...]
    mx = jnp.max(lbp, axis=0, keepdims=True)
    ex = jnp.exp(lbp - mx)
    lb = ex[0:1, :] / jnp.sum(ex, axis=0, keepdims=True)

    st_ref[...] = jnp.zeros_like(st_ref)
    qs_ref[...] = _silu(q_ref[...])

    per_step = HG_STREAM_CHUNKS
    n_steps = nc // per_step

    def shells(j):
        out = []
        for i in range(per_step):
            for rev in (False, True):
                s = _HgStream()
                chunk = (nc - 1 - j * per_step - i) if rev else (j * per_step + i)
                s.t0 = chunk * c if isinstance(chunk, int) else pl.multiple_of(chunk * c, c)
                s.rev = rev
                s.slot = 1 if rev else 0
                s.get_state = functools.partial(lambda slot: st_ref[slot], s.slot)
                s.put_state = functools.partial(st_ref.__setitem__, s.slot)
                out.append(s)
        return out

    def front(j):
        streams = shells(j)
        for s in streams:
            sig = _sigmoid((fb_ref if s.rev else ff_ref)[pl.ds(s.t0, c), :])
            s.f = lb + (1.0 - lb) * sig
            s.k = (1.0 - lb) * (1.0 - sig)
            s.logf = jnp.log(s.f)
            s.q = qs_ref[pl.ds(s.t0, c), :]
            s.v = v_ref[pl.ds(s.t0, c), :]
        return streams

    def save(streams):
        for i, s in enumerate(streams):
            pb_ref[i, 0] = s.amat
            pb_ref[i, 1] = s.vb
            pb_ref[i, 2] = s.qi
            pb_ref[i, 3] = s.kl
            po_ref[i] = s.o
            pe_ref[i] = jnp.broadcast_to(s.edge, (SUBLANES, dk))

    def load(j):
        streams = shells(j)
        for i, s in enumerate(streams):
            s.amat, s.vb, s.qi, s.kl = pb_ref[i, 0], pb_ref[i, 1], pb_ref[i, 2], pb_ref[i, 3]
            s.o = po_ref[i]
            s.edge = pe_ref[i, 0:1, :]
        return streams

    def emit(streams):
        for s in streams:
            (ob_ref if s.rev else of_ref)[pl.ds(s.t0, c), :] = s.o

    first = front(0)
    _hg_cumsum_stage(first)
    _hg_levels_stage(first)
    save(first)

    def body(j, carry):
        old = load(j - 1)
        new = front(j)
        _hg_intra_stage(old)
        _hg_state_stage(old[:2])
        _hg_cumsum_stage(new)
        _hg_state_stage(old[2:])
        _hg_levels_stage(new)
        emit(old)
        save(new)
        return carry

    lax.fori_loop(1, n_steps, body, 0)

    last = load(n_steps - 1)
    _hg_intra_stage(last)
    _hg_state_stage(last)
    emit(last)

    o = of_ref[...] + ob_ref[...]
    o = o * lax.rsqrt(jnp.mean(o * o, axis=1, keepdims=True) + EPS)
    y = o * nw_ref[...] * _silu(g_ref[...])
    out_ref[...] = y.astype(out_ref.dtype)


def _hgrn2(proj, hg_lb, norm_w, B, L):
    dk = HG_D
    H = HG_HEADS
    base = 4 * ML_WIDTH // dk
    return pl.pallas_call(
        _hgrn2_kernel,
        grid=(B, H),
        in_specs=[
            pl.BlockSpec((L, dk), lambda b, h: (b, base + h)),
            pl.BlockSpec((L, dk), lambda b, h: (b, base + H + h)),
            pl.BlockSpec((L, dk), lambda b, h: (b, base + 2 * H + h)),
            pl.BlockSpec((L, dk), lambda b, h: (b, base + 3 * H + h)),
            pl.BlockSpec((L, dk), lambda b, h: (b, base + 4 * H + h)),
            pl.BlockSpec((2, dk), lambda b, h: (0, h)),
            pl.BlockSpec((1, dk), lambda b, h: (0, h)),
        ],
        out_specs=pl.BlockSpec((L, dk), lambda b, h: (b, h)),
        out_shape=jax.ShapeDtypeStruct((B * L, HG_WIDTH), BF16),
        scratch_shapes=[
            pltpu.VMEM((L, dk), F32), pltpu.VMEM((L, dk), F32),
            pltpu.VMEM((2, dk, dk), F32),
            pltpu.VMEM((L, dk), F32),
            pltpu.VMEM((2 * HG_STREAM_CHUNKS, 4, HG_CHUNK, dk), BF16),
            pltpu.VMEM((2 * HG_STREAM_CHUNKS, HG_CHUNK, dk), F32),
            pltpu.VMEM((2 * HG_STREAM_CHUNKS, SUBLANES, dk), F32),
        ],
        compiler_params=_cparams(("arbitrary", "arbitrary")),
        name="hgrn2",
    )(proj, proj, proj, proj, proj, hg_lb, norm_w)


def _outproj_kernel(x_ref, ml_ref, hg_ref, eg_ref, eb_ref, wo_ref, g1_ref, b1_ref, wrh_ref, wrl_ref,
                    x1_ref, lgt_ref, ya_ref, yb_ref):
    i = pl.program_id(0)
    half = ml_ref.shape[1]
    n_exp = lgt_ref.shape[0]

    @pl.when(i == 0)
    def _():
        yb_ref[...] = jnp.zeros_like(yb_ref)

    def step(y_new, y_old):
        mix = _dot(ml_ref[...], wo_ref[0:half, :]) + _dot(hg_ref[...], wo_ref[half:, :])
        y_new[...] = ALPHA * _layer_norm(x_ref[...], eg_ref[...], eb_ref[...]) + mix
        x1 = _layer_norm(y_old[...], g1_ref[...], b1_ref[...])
        x1_ref[...] = x1
        xh, xl = _split_bf16(x1)
        lgt = _dot_nt(wrh_ref[...], xh) + _dot_nt(wrh_ref[...], xl) + _dot_nt(wrl_ref[...], xh)
        lgt_ref[...] = lgt[0:n_exp, :]

    @pl.when(lax.rem(i, 2) == 0)
    def _():
        step(ya_ref, yb_ref)

    @pl.when(lax.rem(i, 2) == 1)
    def _():
        step(yb_ref, ya_ref)


def _outproj(x, ml, hg, eg, eb, wo, g1, b1, wrt_hi, wrt_lo, n_exp, tm=256):
    n, d = x.shape
    n_tiles = n // tm
    row = lambda i: (jnp.minimum(i, n_tiles - 1), 0)
    prev_row = lambda i: (jnp.maximum(i - 1, 0), 0)
    fixed = lambda i: (0, 0)
    return pl.pallas_call(
        _outproj_kernel,
        grid=(n_tiles + 1,),
        in_specs=[
            pl.BlockSpec((tm, d), row),
            pl.BlockSpec((tm, ML_WIDTH), row),
            pl.BlockSpec((tm, HG_WIDTH), row),
            pl.BlockSpec((1, d), fixed),
            pl.BlockSpec((1, d), fixed),
            pl.BlockSpec((d, d), fixed),
            pl.BlockSpec((1, d), fixed),
            pl.BlockSpec((1, d), fixed),
            pl.BlockSpec((LANES, d), fixed),
            pl.BlockSpec((LANES, d), fixed),
        ],
        out_specs=[pl.BlockSpec((tm, d), prev_row),
                   pl.BlockSpec((n_exp, tm), lambda i: (0, jnp.maximum(i - 1, 0)))],
        out_shape=[jax.ShapeDtypeStruct((n, d), F32), jax.ShapeDtypeStruct((n_exp, n), F32)],
        scratch_shapes=[pltpu.VMEM((tm, d), F32), pltpu.VMEM((tm, d), F32)],
        compiler_params=_cparams(("arbitrary",)),
        name="outproj",
    )(x, ml, hg, eg, eb, wo, g1, b1, wrt_hi, wrt_lo)


def _pad_cols(w, width=LANES):
    return jnp.pad(w, ((0, 0), (0, width - w.shape[1])))


def _token_mixer_stage(x, emb_ln_g, emb_ln_b, w_in, conv_w, ml_igate_b, ml_fgate_b, ml_norm_w,
                       hg_lb, hg_norm_w, w_out, ln1_g, ln1_b, w_router, B, L):
    d = D_MODEL
    g0 = 4 * ML_WIDTH
    w_main = jnp.concatenate([w_in[:, :g0], w_in[:, g0 + 16:]], axis=1).astype(BF16)
    wg = w_in[:, g0:g0 + 16].reshape(d, 4, ML_HEADS).transpose(0, 2, 1).reshape(d, 16)
    wg_both = jnp.concatenate(_split_bf16(_pad_cols(wg)), axis=1)
    eg = emb_ln_g.reshape(1, d)
    eb = emb_ln_b.reshape(1, d)
    proj, gate_tok, gate_t = _ln_proj(x, eg, eb, w_main, wg_both)

    gate_row = gate_t.reshape(ML_HEADS, 4, B, L).transpose(2, 0, 1, 3)
    bias = jnp.concatenate([ml_igate_b[0], ml_igate_b[1], ml_fgate_b[0], ml_fgate_b[1]]).astype(F32)
    ml = _mlstm(proj, conv_w, gate_tok, gate_row, bias, ml_norm_w.reshape(1, ML_WIDTH), B, L)
    hg = _hgrn2(proj, hg_lb, hg_norm_w.reshape(1, HG_WIDTH), B, L)

    wrt_hi, wrt_lo = _split_bf16(_pad_cols(w_router).T)
    return _outproj(x, ml, hg, eg, eb, w_out.astype(BF16), ln1_g.reshape(1, d), ln1_b.reshape(1, d),
                    wrt_hi, wrt_lo, w_router.shape[1])


def _excl_token_cumsum(mask):
    E, R, ln = mask.shape
    mf = jnp.where(mask, 1.0, 0.0)
    mb = mf.astype(BF16)
    upper = jnp.where(lax.broadcasted_iota(I32, (ln, ln), 0) <= lax.broadcasted_iota(I32, (ln, ln), 1),
                      1.0, 0.0).astype(BF16)
    lower = jnp.where(lax.broadcasted_iota(I32, (R, R), 1) < lax.broadcasted_iota(I32, (R, R), 0),
                      1.0, 0.0).astype(BF16)
    ones = jnp.ones((ln, ln), BF16)
    within = _dot(mb.reshape(E * R, ln), upper).reshape(E, R, ln)
    rows = jnp.stack([_dot(_dot(lower, mb[e]).astype(BF16), ones) for e in range(E)], axis=0)
    return within - mf + rows


def _select_kernel(lg_ref, pos_ref, posm_ref, wts_ref, *, cap):
    E = lg_ref.shape[0]
    lg = lg_ref[...]
    mx = jnp.max(lg, axis=0, keepdims=True)
    ex = jnp.exp(lg - mx)
    aff = ex / jnp.sum(ex, axis=0, keepdims=True)

    def count(mask):
        ones = jnp.where(mask, 1.0, 0.0)
        return jnp.sum(jnp.sum(ones, axis=2, keepdims=True), axis=1, keepdims=True)

    def body(i, tbits):
        cand = tbits | lax.shift_left(jnp.int32(1), 30 - i)
        cnt = count(aff >= lax.bitcast_convert_type(cand, F32))
        return jnp.where(cnt >= cap, cand, tbits)

    tbits = lax.fori_loop(0, 31, body, jnp.zeros((E, 1, 1), I32))
    thr = lax.bitcast_convert_type(tbits, F32)
    nxt = lax.bitcast_convert_type(tbits + 1, F32)
    above = aff >= nxt
    band = jnp.logical_and(aff >= thr, jnp.logical_not(above))
    need = cap - count(above)
    sel = jnp.logical_or(above, jnp.logical_and(band, _excl_token_cumsum(band) < need))
    pos = _excl_token_cumsum(sel).astype(I32)
    pos_ref[...] = pos
    posm_ref[...] = jnp.where(sel, pos, -1)
    wts_ref[...] = jnp.where(sel, aff, 0.0)


def _select(lg_t, cap):
    E, R, ln = lg_t.shape
    full = pl.BlockSpec((E, R, ln), lambda i: (0, 0, 0))
    return pl.pallas_call(
        functools.partial(_select_kernel, cap=cap),
        grid=(1,),
        in_specs=[full],
        out_specs=[full, full, full],
        out_shape=[jax.ShapeDtypeStruct((E, R, ln), I32), jax.ShapeDtypeStruct((E, R, ln), I32),
                   jax.ShapeDtypeStruct((E, R, ln), F32)],
        compiler_params=_cparams(("arbitrary",)),
        name="select",
    )(lg_t)


TOK_TILE = 256
SLOT_CHUNK = 64


def _pack_bf16_pairs(x):
    h = x.shape[1] // 2
    lo = lax.shift_right_logical(lax.bitcast_convert_type(x[:, :h], U32), jnp.uint32(16))
    hi = lax.bitcast_convert_type(x[:, h:], U32) & jnp.uint32(0xFFFF0000)
    return hi | lo


def _unpack_bf16_pairs(w):
    lo = lax.bitcast_convert_type(lax.shift_left(w, jnp.uint32(16)), F32).astype(BF16)
    hi = lax.bitcast_convert_type(w & jnp.uint32(0xFFFF0000), F32).astype(BF16)
    return lo, hi


def _dispatch_kernel(off_ref, base_ref, cnt_ref, xa_ref, xb_ref, pos_ref, xe_ref,
                     x16_ref, res_ref, stage_ref, ostage_ref, carry_ref, sem, osem, *, tiles_a, cap_total, pad):
    E = N_EXPERTS
    CH = SLOT_CHUNK
    SUB = SUBLANES
    tc = xa_ref.shape[0]
    t = pl.program_id(0)
    nt = pl.num_programs(0)
    slot = lax.rem(t, 2)

    @pl.when(t < tiles_a)
    def _():
        x16_ref[...] = xa_ref[...].astype(BF16)

    @pl.when(t >= tiles_a)
    def _():
        x16_ref[...] = xb_ref[...].astype(BF16)

    xb = x16_ref[...]

    @pl.when(t == 0)
    def _():
        carry_ref[...] = jnp.zeros_like(carry_ref)

    base = [base_ref[t * E + e] for e in range(E)]
    cnt = [cnt_ref[t * E + e] for e in range(E)]
    al = [pl.multiple_of((b // SUB) * SUB, SUB) for b in base]
    first = [off_ref[t * E + e] - (base[e] - al[e]) for e in range(E)]

    def onehot(e, start, rows):
        kio = lax.broadcasted_iota(I32, (rows, tc), 0)
        rel = pos_ref[e:e + 1, :] - (first[e] + start)
        return jnp.where(rel == kio, 1.0, 0.0).astype(BF16)

    ot = jnp.concatenate([onehot(e, 0, CH) for e in range(E)], axis=0)
    res_ref[...] = _dot(ot, xb)
    for e in range(E):
        res_ref[e * CH:e * CH + SUB, :] += carry_ref[e * SUB:(e + 1) * SUB, :]
    stage_ref[slot] = _pack_bf16_pairs(res_ref[...])

    nxt = [((base[e] + cnt[e]) // SUB) * SUB - al[e] for e in range(E)]
    oc = jnp.concatenate([onehot(e, nxt[e], SUB) for e in range(E)], axis=0)
    new_carry = _dot(oc, xb)
    for e in range(E):
        keep = jnp.where(nxt[e] == 0, 1.0, 0.0)
        carry_ref[e * SUB:(e + 1) * SUB, :] = new_carry[e * SUB:(e + 1) * SUB, :] + keep * carry_ref[e * SUB:(e + 1) * SUB, :]

    def main_copy(s, e, row):
        return pltpu.make_async_copy(stage_ref.at[s, pl.ds(e * CH, CH)], xe_ref.at[pl.ds(row, CH)], sem.at[s])

    @pl.when(t > 0)
    def _():
        for e in range(E):
            main_copy(1 - slot, e, 0).wait()

    for e in range(E):
        main_copy(slot, e, al[e]).start()

    for e in range(E):
        nch = (base[e] - al[e] + cnt[e] + CH - 1) // CH

        def body(c, carry, e=e):
            ostage_ref[...] = _pack_bf16_pairs(_dot(onehot(e, c * CH, CH), xb))
            row = pl.multiple_of(al[e] + c * CH, SUB)
            cp = pltpu.make_async_copy(ostage_ref, xe_ref.at[pl.ds(row, CH)], osem)
            cp.start()
            cp.wait()
            return carry

        lax.fori_loop(1, nch, body, 0)

    @pl.when(t == nt - 1)
    def _():
        for e in range(E):
            main_copy(slot, e, 0).wait()
        ostage_ref[...] = jnp.zeros_like(ostage_ref)
        fills = [pltpu.make_async_copy(ostage_ref, xe_ref.at[pl.ds(e * (cap_total + pad) + cap_total + j * CH, CH)], osem)
                 for e in range(E) for j in range(pad // CH)]
        for cp in fills:
            cp.start()
        for cp in fills:
            cp.wait()


def _dispatch(off, base, cnt, x1_a, x1_b, pos_rows, cap_total, pad):
    d = x1_a.shape[1]
    E = N_EXPERTS
    tc = TOK_TILE
    tiles_a = x1_a.shape[0] // tc
    tiles_b = x1_b.shape[0] // tc
    grid_spec = pltpu.PrefetchScalarGridSpec(
        num_scalar_prefetch=3,
        grid=(tiles_a + tiles_b,),
        in_specs=[
            pl.BlockSpec((tc, d), lambda i, *_: (jnp.minimum(i, tiles_a - 1), 0)),
            pl.BlockSpec((tc, d), lambda i, *_: (jnp.maximum(i - tiles_a, 0), 0)),
            pl.BlockSpec((E, tc), lambda i, *_: (0, i)),
        ],
        out_specs=pl.BlockSpec(memory_space=pl.ANY),
        scratch_shapes=[
            pltpu.VMEM((tc, d), BF16),
            pltpu.VMEM((E * SLOT_CHUNK, d), F32),
            pltpu.VMEM((2, E * SLOT_CHUNK, d // 2), U32),
            pltpu.VMEM((SLOT_CHUNK, d // 2), U32),
            pltpu.VMEM((E * SUBLANES, d), F32),
            pltpu.SemaphoreType.DMA((2,)),
            pltpu.SemaphoreType.DMA(()),
        ],
    )
    return pl.pallas_call(
        functools.partial(_dispatch_kernel, tiles_a=tiles_a, cap_total=cap_total, pad=pad),
        grid_spec=grid_spec,
        out_shape=jax.ShapeDtypeStruct((E * (cap_total + pad), d // 2), U32),
        compiler_params=_cparams(("arbitrary",)),
        name="dispatch",
    )(off, base, cnt, x1_a, x1_b, pos_rows)


def _ffn_hidden_kernel(xe_ref, wg_ref, wu_ref, hid_ref, wg16_ref, wu16_ref):
    @pl.when(pl.program_id(2) == 0)
    def _():
        wg16_ref[...] = wg_ref[0].astype(BF16)
        wu16_ref[...] = wu_ref[0].astype(BF16)

    lo, hi = _unpack_bf16_pairs(xe_ref[0])
    h = lo.shape[1]
    gate = _dot(lo, wg16_ref[0:h, :]) + _dot(hi, wg16_ref[h:, :])
    up = _dot(lo, wu16_ref[0:h, :]) + _dot(hi, wu16_ref[h:, :])
    hid_ref[0] = (_silu(gate) * up).astype(hid_ref.dtype)


def _ffn_down_kernel(hid_ref, wd_ref, out_ref, wd16_ref):
    @pl.when(pl.program_id(2) == 0)
    def _():
        wd16_ref[...] = wd_ref[0].astype(BF16)

    out_ref[0] = _dot(hid_ref[0], wd16_ref[...]).astype(out_ref.dtype)


def _ffn(xe, wg, wu, wd, cap_total, tf=512, tn=1024):
    E, _, dh = xe.shape
    d = 2 * dh
    ff = wg.shape[2]
    tm = next(t for t in (1024, 512, 256, 128) if cap_total % t == 0)
    hid = pl.pallas_call(
        _ffn_hidden_kernel,
        grid=(E, ff // tf, cap_total // tm),
        in_specs=[
            pl.BlockSpec((1, tm, dh), lambda e, f, i: (e, i, 0)),
            pl.BlockSpec((1, d, tf), lambda e, f, i: (e, 0, f)),
            pl.BlockSpec((1, d, tf), lambda e, f, i: (e, 0, f)),
        ],
        out_specs=pl.BlockSpec((1, tm, tf), lambda e, f, i: (e, i, f)),
        out_shape=jax.ShapeDtypeStruct((E, cap_total, ff), BF16),
        scratch_shapes=[pltpu.VMEM((d, tf), BF16), pltpu.VMEM((d, tf), BF16)],
        compiler_params=_cparams(("arbitrary", "arbitrary", "arbitrary")),
        name="ffn_hidden",
    )(xe, wg, wu)
    return pl.pallas_call(
        _ffn_down_kernel,
        grid=(E, d // tn, cap_total // tm),
        in_specs=[
            pl.BlockSpec((1, tm, ff), lambda e, c, i: (e, i, 0)),
            pl.BlockSpec((1, ff, tn), lambda e, c, i: (e, 0, c)),
        ],
        out_specs=pl.BlockSpec((1, tm, tn), lambda e, c, i: (e, i, c)),
        out_shape=jax.ShapeDtypeStruct((E, cap_total, d), BF16),
        scratch_shapes=[pltpu.VMEM((ff, tn), BF16)],
        compiler_params=_cparams(("arbitrary", "arbitrary", "arbitrary")),
        name="ffn_down",
    )(hid, wd)


def _combine_kernel(base_ref, cnt_ref, x1_ref, grow_ref, w_ref, g2_ref, b2_ref, ye_ref, out_ref,
                    buf_ref, obuf_ref, acc_ref, sem, osem, *, rows_total):
    E = N_EXPERTS
    CH = SLOT_CHUNK
    tc = x1_ref.shape[0]
    t = pl.program_id(0)
    nt = pl.num_programs(0)
    slot = lax.rem(t, 2)
    last = rows_total - CH
    ALIGN = 2 * SUBLANES

    def window(step, e):
        al = (base_ref[step * E + e] // ALIGN) * ALIGN
        return al, pl.multiple_of(jnp.minimum(al, last), ALIGN)

    def chunk_copy(s, e, start):
        return pltpu.make_async_copy(ye_ref.at[pl.ds(start, CH)], buf_ref.at[s, pl.ds(e * CH, CH)], sem.at[s, e])

    @pl.when(t == 0)
    def _():
        for e in range(E):
            chunk_copy(0, e, window(0, e)[1]).start()

    @pl.when(t + 1 < nt)
    def _():
        for e in range(E):
            chunk_copy(1 - slot, e, window(t + 1, e)[1]).start()

    kio = lax.broadcasted_iota(I32, (CH, tc), 0)

    def weights(e, start, lo):
        grow = grow_ref[e:e + 1, :]
        hit = jnp.logical_and(grow - start == kio, grow >= lo)
        return jnp.where(hit, w_ref[e:e + 1, :], 0.0).astype(BF16)

    at = jnp.concatenate([weights(e, window(t, e)[1], 0) for e in range(E)], axis=0)
    for e in range(E):
        chunk_copy(slot, e, 0).wait()
    acc_ref[...] = ALPHA * x1_ref[...] + _dot_tn(at, buf_ref[slot])

    for e in range(E):
        al = window(t, e)[0]
        nch = (base_ref[t * E + e] - al + cnt_ref[t * E + e] + CH - 1) // CH

        def body(c, carry, e=e, al=al):
            lo = al + c * CH
            start = pl.multiple_of(jnp.minimum(lo, last), ALIGN)
            cp = pltpu.make_async_copy(ye_ref.at[pl.ds(start, CH)], obuf_ref, osem)
            cp.start()
            cp.wait()
            acc_ref[...] += _dot_tn(weights(e, start, lo), obuf_ref[...])
            return carry

        lax.fori_loop(1, nch, body, 0)

    out_ref[...] = _layer_norm(acc_ref[...], g2_ref[...], b2_ref[...])


def _combine(base, cnt, x1, grow_rows, w_rows, g2, b2, ye, rows_total):
    n, d = x1.shape
    E = N_EXPERTS
    tc = TOK_TILE
    grid_spec = pltpu.PrefetchScalarGridSpec(
        num_scalar_prefetch=2,
        grid=(n // tc,),
        in_specs=[
            pl.BlockSpec((tc, d), lambda i, *_: (i, 0)),
            pl.BlockSpec((E, tc), lambda i, *_: (0, i)),
            pl.BlockSpec((E, tc), lambda i, *_: (0, i)),
            pl.BlockSpec((1, d), lambda i, *_: (0, 0)),
            pl.BlockSpec((1, d), lambda i, *_: (0, 0)),
            pl.BlockSpec(memory_space=pl.ANY),
        ],
        out_specs=pl.BlockSpec((tc, d), lambda i, *_: (i, 0)),
        scratch_shapes=[
            pltpu.VMEM((2, E * SLOT_CHUNK, d), BF16),
            pltpu.VMEM((SLOT_CHUNK, d), BF16),
            pltpu.VMEM((tc, d), F32),
            pltpu.SemaphoreType.DMA((2, E)),
            pltpu.SemaphoreType.DMA(()),
        ],
    )
    return pl.pallas_call(
        functools.partial(_combine_kernel, rows_total=rows_total),
        grid_spec=grid_spec,
        out_shape=jax.ShapeDtypeStruct((n, d), F32),
        compiler_params=_cparams(("arbitrary",)),
        name="combine",
    )(base, cnt, x1, grow_rows, w_rows, g2, b2, ye)


def _routing_tables(pos, posm, wts, cap, slot0, cap_total, pad):
    E = N_EXPERTS
    n = pos.shape[1] * pos.shape[2]
    tiles = n // TOK_TILE
    eidx = jnp.arange(E, dtype=I32)
    off = pos.reshape(E, n)[:, ::TOK_TILE].T
    cnt = jnp.concatenate([off[1:], jnp.full((1, E), cap, I32)], axis=0) - off
    base_x = off + eidx[None, :] * (cap_total + pad) + slot0
    base_y = off + eidx[None, :] * cap_total + slot0
    pos_rows = posm.reshape(E, n)
    grow_rows = jnp.where(pos_rows >= 0, pos_rows + eidx[:, None] * cap_total + slot0, -1)
    flat = lambda a: a.reshape(tiles * E)
    return flat(off), flat(base_x), flat(base_y), flat(cnt), pos_rows, grow_rows, wts.reshape(E, n)


def kernel(x_prompt, x_sample, emb_ln_g, emb_ln_b, w_in, conv_w, ml_igate_b, ml_fgate_b, ml_norm_w, hg_lb, hg_norm_w, w_out, ln1_g, ln1_b, w_router, w_gate, w_up, w_down, ln2_g, ln2_b):
    E = N_EXPERTS
    d = D_MODEL
    groups = (x_prompt, x_sample)
    caps = [EC_FACTOR * x.shape[0] * x.shape[1] // E for x in groups]
    cap_total = sum(caps)
    pad = 2 * SLOT_CHUNK
    assert all(c % (2 * SUBLANES) == 0 for c in caps)

    staged = []
    tables = []
    slot0 = 0
    for x, cap in zip(groups, caps):
        B, L, _ = x.shape
        n = B * L
        x1, lg = _token_mixer_stage(x.reshape(n, d), emb_ln_g, emb_ln_b, w_in[0], conv_w[0], ml_igate_b[0],
                                    ml_fgate_b[0], ml_norm_w[0], hg_lb, hg_norm_w[0], w_out[0], ln1_g[0],
                                    ln1_b[0], w_router[0], B, L)
        lg_t = lg.reshape(E, n // LANES, LANES)
        pos, posm, wts = _select(lg_t, cap)
        off, base_x, base_y, cnt, pos_rows, grow_t, w_t = _routing_tables(pos, posm, wts, cap, slot0, cap_total, pad)
        tables.append((off, base_x, cnt, pos_rows))
        staged.append((x1, base_y, cnt, grow_t, w_t, (B, L)))
        slot0 += cap

    off, base_x, cnt, pos_rows = (jnp.concatenate(parts, axis=-1) for parts in zip(*tables))
    xe = _dispatch(off, base_x, cnt, staged[0][0], staged[1][0], pos_rows, cap_total, pad)

    ye = _ffn(xe.reshape(E, cap_total + pad, d // 2), w_gate[0], w_up[0], w_down[0], cap_total)
    ye = ye.reshape(E * cap_total, d)

    g2 = ln2_g[0].reshape(1, d)
    b2 = ln2_b[0].reshape(1, d)
    outs = []
    for x1, base_y, cnt, grow_t, w_t, (B, L) in staged:
        y = _combine(base_y, cnt, x1, grow_t, w_t, g2, b2, ye, E * cap_total)
        outs.append(y.reshape(B, L, d))
    return tuple(outs)
```

```python
import functools

import jax
import jax.numpy as jnp
from jax import lax
from jax.experimental import pallas as pl
from jax.experimental.pallas import tpu as pltpu

F32 = jnp.float32
BF16 = jnp.bfloat16
I32 = jnp.int32
U32 = jnp.uint32

D_MODEL = 2048
ML_WIDTH = 1024
ML_HEADS = 4
ML_DH = 256
HG_WIDTH = 1024
HG_HEADS = 8
HG_D = 128
N_EXPERTS = 16
EC_FACTOR = 2
EXPERT_FF = 2048
DEPTH = 1
ALPHA = (2.0 * DEPTH) ** 0.25
EPS = 1e-5

LANES = 128
SUBLANES = 8
N_GATE_COLS = 16
VMEM_LIMIT = 56 * 1024 * 1024

ML_CHUNK = 256
HG_CHUNK = 128
HG_STREAM_CHUNKS = 8


def _cparams(sem):
    return pltpu.CompilerParams(dimension_semantics=sem, vmem_limit_bytes=VMEM_LIMIT)


def _split_bf16(x):
    hi = x.astype(BF16)
    lo = (x - hi.astype(F32)).astype(BF16)
    return hi, lo


def _layer_norm(x, g, b):
    mu = jnp.mean(x, axis=-1, keepdims=True)
    xc = x - mu
    var = jnp.mean(xc * xc, axis=-1, keepdims=True)
    return xc * lax.rsqrt(var + EPS) * g + b


def _sigmoid(x):
    return 1.0 / (1.0 + jnp.exp(-x))


def _silu(x):
    return x * _sigmoid(x)


def _log_sigmoid(x):
    return jnp.minimum(x, 0.0) - jnp.log1p(jnp.exp(-jnp.abs(x)))


def _dot(a, b):
    return jnp.dot(a, b, preferred_element_type=F32)


def _dot_nt(a, b):
    return lax.dot_general(a, b, (((1,), (1,)), ((), ())), preferred_element_type=F32)


def _dot_tn(a, b):
    return lax.dot_general(a, b, (((0,), (0,)), ((), ())), preferred_element_type=F32)


def _ln_proj_kernel(x_ref, g_ref, b_ref, w_ref, wg_ref, proj_ref, gate_ref, gatet_ref,
                    hha_ref, hla_ref, hhb_ref, hlb_ref, *, n_slices):
    i = pl.program_id(0)
    j = pl.program_id(1)
    rs = x_ref.shape[0]

    @pl.when(jnp.logical_and(i == 0, j == 0))
    def _():
        hhb_ref[...] = jnp.zeros_like(hhb_ref)
        hlb_ref[...] = jnp.zeros_like(hlb_ref)

    def step(hh_new, hl_new, hh_old, hl_old):
        @pl.when(j == 0)
        def _():
            both = _dot(hh_old[...], wg_ref[...])
            gates = both[:, :LANES] + both[:, LANES:] + _dot(hl_old[...], wg_ref[:, :LANES])
            gate_ref[...] = gates
            gatet_ref[...] = gates.T[0:gatet_ref.shape[0], :]

        proj_ref[...] = _dot(hh_old[...], w_ref[...]).astype(proj_ref.dtype)
        r0 = pl.multiple_of(jnp.minimum(j, n_slices - 1) * rs, rs)
        hh, hl = _split_bf16(_layer_norm(x_ref[...], g_ref[...], b_ref[...]))
        hh_new[pl.ds(r0, rs), :] = hh
        hl_new[pl.ds(r0, rs), :] = hl

    @pl.when(lax.rem(i, 2) == 0)
    def _():
        step(hha_ref, hla_ref, hhb_ref, hlb_ref)

    @pl.when(lax.rem(i, 2) == 1)
    def _():
        step(hhb_ref, hlb_ref, hha_ref, hla_ref)


def _ln_proj(x, g, b, w_main, wg_both, tm=1024, tn=1024):
    n, d = x.shape
    tm = min(tm, n)
    ncol = w_main.shape[1] // tn
    n_tiles = n // tm
    n_slices = ncol - 1
    rs = tm // n_slices
    assert tm % n_slices == 0 and rs % SUBLANES == 0
    prev_row = lambda i, j: jnp.maximum(i - 1, 0)
    return pl.pallas_call(
        functools.partial(_ln_proj_kernel, n_slices=n_slices),
        grid=(n_tiles + 1, ncol),
        in_specs=[
            pl.BlockSpec((rs, d), lambda i, j: (jnp.minimum(i, n_tiles - 1) * n_slices
                                                + jnp.minimum(j, n_slices - 1), 0)),
            pl.BlockSpec((1, d), lambda i, j: (0, 0)),
            pl.BlockSpec((1, d), lambda i, j: (0, 0)),
            pl.BlockSpec((d, tn), lambda i, j: (0, j)),
            pl.BlockSpec((d, 2 * LANES), lambda i, j: (0, 0)),
        ],
        out_specs=[
            pl.BlockSpec((tm, tn), lambda i, j: (prev_row(i, j), jnp.where(i > 0, j, 0))),
            pl.BlockSpec((tm, LANES), lambda i, j: (prev_row(i, j), 0)),
            pl.BlockSpec((N_GATE_COLS, tm), lambda i, j: (0, prev_row(i, j))),
        ],
        out_shape=[
            jax.ShapeDtypeStruct((n, ncol * tn), F32),
            jax.ShapeDtypeStruct((n, LANES), F32),
            jax.ShapeDtypeStruct((N_GATE_COLS, n), F32),
        ],
        scratch_shapes=[pltpu.VMEM((tm, d), BF16)] * 4,
        compiler_params=_cparams(("arbitrary", "arbitrary")),
        name="ln_proj",
    )(x, g, b, w_main, wg_both)


class _MlStream:
    pass


def _mlstm_chunks(streams):
    c = streams[0].qb.shape[0]
    tt = lax.broadcasted_iota(I32, (c, c), 0)
    ss = lax.broadcasted_iota(I32, (c, c), 1)

    for s in streams:
        causal = (ss >= tt) if s.rev else (ss <= tt)
        b_col = jnp.sum(jnp.where(causal, s.f_row, 0.0), axis=1, keepdims=True)
        anti = (tt >= ss) if s.rev else (tt <= ss)
        b_row = jnp.sum(jnp.where(anti, s.f_col, 0.0), axis=0, keepdims=True)
        total = jnp.sum(s.f_row, axis=1, keepdims=True)

        dmat = jnp.where(causal, b_col - b_row + s.i_row, -jnp.inf)
        a_inter = b_col + s.m_state
        s.m_t = jnp.maximum(a_inter, jnp.max(dmat, axis=1, keepdims=True))
        s.w_inter = jnp.exp(a_inter - s.m_t)
        s.p = jnp.exp(dmat - s.m_t)
        s.qk = _dot_nt(s.qb, s.kb)
        s.qc = _dot(s.qb, s.c_state.astype(BF16))

        g_col = total - b_col + s.i_col
        g_row = total - b_row + s.i_row
        s.m_new = jnp.maximum(total + s.m_state, jnp.max(g_row, axis=1, keepdims=True))
        s.decay = jnp.exp(total + s.m_state - s.m_new)
        ks = s.kb.astype(F32) * jnp.exp(g_col - s.m_new)
        s.kv = _dot_tn(ks.astype(BF16), s.vb)
        s.n_new = s.decay * s.n_state + jnp.sum(ks, axis=0, keepdims=True)

    for s in streams:
        s.s = s.qk * s.p
        s.sv = _dot(s.s.astype(BF16), s.vb)

    for s in streams:
        num = s.w_inter * s.qc + s.sv
        qn = jnp.sum(s.qb.astype(F32) * s.n_state, axis=1, keepdims=True)
        den = s.w_inter * qn + jnp.sum(s.s, axis=1, keepdims=True)
        s.h = num / jnp.maximum(jnp.abs(den), jnp.exp(-s.m_t))
        s.c_new = s.decay * s.c_state + s.kv


def _mlstm_kernel(bias_ref, q_ref, k_ref, v_ref, o_ref, cwq_ref, cwk_ref, gt_ref, gr_ref, nw_ref,
                  out_ref, qc_ref, kc_ref, vc_ref, hf_ref, hb_ref, cs_ref):
    L, d = q_ref.shape

    lane_shift = lax.rem(LANES - 4 * pl.program_id(1), LANES)
    c = ML_CHUNK
    nc = L // c
    hd = pl.program_id(1)

    row = lax.broadcasted_iota(I32, (L, 1), 0)

    def conv_silu(x, w):
        xp = jnp.where(row == 0, 0.0, pltpu.roll(x, 1, 0))
        xn = jnp.where(row == L - 1, 0.0, pltpu.roll(x, L - 1, 0))
        return _silu(w[0:1, :] * xp + w[1:2, :] * x + w[2:3, :] * xn)

    qc_ref[...] = conv_silu(q_ref[...], cwq_ref[...]).astype(BF16)
    kc_ref[...] = (conv_silu(k_ref[...], cwk_ref[...]) * (ML_DH ** -0.5)).astype(BF16)
    vc_ref[...] = v_ref[...].astype(BF16)

    bi_f = bias_ref[hd]
    bi_b = bias_ref[ML_HEADS + hd]
    bf_f = bias_ref[2 * ML_HEADS + hd]
    bf_b = bias_ref[3 * ML_HEADS + hd]

    cs_ref[...] = jnp.zeros_like(cs_ref)

    def gates(t0):
        gc = pltpu.roll(gt_ref[pl.ds(t0, c), :], lane_shift, 1)[:, 0:4]
        gr = gr_ref[0, 0, :, pl.ds(t0, c)]
        return gc, gr

    def stream(t0, slot, rev, bias_i, bias_f, n_state, m_state):
        s = _MlStream()
        gc, gr = gates(t0)
        gi, gf = (1, 3) if rev else (0, 2)
        s.qb = qc_ref[pl.ds(t0, c), :]
        s.kb = kc_ref[pl.ds(t0, c), :]
        s.vb = vc_ref[pl.ds(t0, c), :]
        s.f_col = _log_sigmoid(gc[:, gf:gf + 1] + bias_f)
        s.f_row = _log_sigmoid(gr[gf:gf + 1, :] + bias_f)
        s.i_col = gc[:, gi:gi + 1] + bias_i
        s.i_row = gr[gi:gi + 1, :] + bias_i
        s.c_state = cs_ref[slot]
        s.n_state = n_state
        s.m_state = m_state
        s.rev = rev
        return s

    def body(j, carry):
        n_f, m_f, n_b, m_b = carry
        t0 = pl.multiple_of(j * c, c)
        t1 = pl.multiple_of((nc - 1 - j) * c, c)
        fwd = stream(t0, 0, False, bi_f, bf_f, n_f, m_f)
        bwd = stream(t1, 1, True, bi_b, bf_b, n_b, m_b)
        _mlstm_chunks([fwd, bwd])
        cs_ref[0] = fwd.c_new
        hf_ref[pl.ds(t0, c), :] = fwd.h
        cs_ref[1] = bwd.c_new
        hb_ref[pl.ds(t1, c), :] = bwd.h
        return fwd.n_new, fwd.m_new, bwd.n_new, bwd.m_new

    zn = jnp.zeros((1, d), F32)
    m0 = jnp.full((1, 1), -1e30, F32)
    lax.fori_loop(0, nc, body, (zn, m0, zn, m0))

    hm = hf_ref[...] + hb_ref[...]
    mu = jnp.mean(hm, axis=1, keepdims=True)
    hc = hm - mu
    var = jnp.mean(hc * hc, axis=1, keepdims=True)
    y = hc * lax.rsqrt(var + EPS) * nw_ref[...] * _sigmoid(o_ref[...])
    out_ref[...] = y.astype(out_ref.dtype)


def _mlstm(proj, conv_w, gate_tok, gate_row, bias, norm_w, B, L):
    d = ML_DH
    H = ML_HEADS
    grid_spec = pltpu.PrefetchScalarGridSpec(
        num_scalar_prefetch=1,
        grid=(B, H),
        in_specs=[
            pl.BlockSpec((L, d), lambda b, h, s: (b, h)),
            pl.BlockSpec((L, d), lambda b, h, s: (b, H + h)),
            pl.BlockSpec((L, d), lambda b, h, s: (b, 2 * H + h)),
            pl.BlockSpec((L, d), lambda b, h, s: (b, 3 * H + h)),
            pl.BlockSpec((3, d), lambda b, h, s: (0, h)),
            pl.BlockSpec((3, d), lambda b, h, s: (0, H + h)),
            pl.BlockSpec((L, LANES), lambda b, h, s: (b, 0)),
            pl.BlockSpec((1, 1, 4, L), lambda b, h, s: (b, h, 0, 0)),
            pl.BlockSpec((1, d), lambda b, h, s: (0, h)),
        ],
        out_specs=pl.BlockSpec((L, d), lambda b, h, s: (b, h)),
        scratch_shapes=[
            pltpu.VMEM((L, d), BF16), pltpu.VMEM((L, d), BF16), pltpu.VMEM((L, d), BF16),
            pltpu.VMEM((L, d), F32), pltpu.VMEM((L, d), F32),
            pltpu.VMEM((2, d, d), F32),
        ],
    )
    return pl.pallas_call(
        _mlstm_kernel,
        grid_spec=grid_spec,
        out_shape=jax.ShapeDtypeStruct((B * L, ML_WIDTH), BF16),
        compiler_params=_cparams(("arbitrary", "arbitrary")),
        name="mlstm",
    )(bias, proj, proj, proj, proj, conv_w, conv_w, gate_tok, gate_row, norm_w)


def _chunk_cumsum(x, rev):
    c, n = x.shape
    r = lax.broadcasted_iota(I32, (c, c), 0)
    s = lax.broadcasted_iota(I32, (c, c), 1)
    tri = jnp.where((s >= r) if rev else (s <= r), 1.0, 0.0).astype(BF16)
    hi = x.astype(BF16)
    r1 = x - hi.astype(F32)
    mid = r1.astype(BF16)
    lo = (r1 - mid.astype(F32)).astype(BF16)
    parts = _dot(tri, jnp.concatenate([hi, mid, lo], axis=1))
    return parts[:, :n] + parts[:, n:2 * n] + parts[:, 2 * n:]


class _HgStream:
    pass


def _hg_cumsum_stage(streams):
    for s in streams:
        s.b = _chunk_cumsum(s.logf, s.rev)
        s.vb = s.v.astype(BF16)
        s.o = jnp.sum(s.q * s.k, axis=1, keepdims=True) * s.v


def _hg_levels_stage(streams):
    c, dk = streams[0].q.shape
    row = lax.broadcasted_iota(I32, (c, dk), 0)
    tt = lax.broadcasted_iota(I32, (c, c), 0)
    ss = lax.broadcasted_iota(I32, (c, c), 1)

    for s in streams:
        strict = (ss > tt) if s.rev else (ss < tt)
        f_prev = pltpu.roll(s.f, 1, 0)
        f_next = pltpu.roll(s.f, c - 1, 0)
        amat = None
        m = c // 2
        while m >= 1:
            blk = 2 * m
            u = row & (blk - 1)
            tgt = (u < m) if s.rev else (u >= m)
            if m == 1:
                e = jnp.where(tgt, s.f, 1.0)
            elif m == 2:
                if s.rev:
                    e = jnp.where(u == 0, s.f * f_next, jnp.where(u == 1, s.f, jnp.where(u == 2, 1.0, f_prev)))
                else:
                    e = jnp.where(u == 0, f_next, jnp.where(u == 1, 1.0, jnp.where(u == 2, s.f, s.f * f_prev)))
            else:
                bb = s.b.reshape(c // blk, blk, dk)
                ref = bb[:, m:m + 1, :] if s.rev else bb[:, m - 1:m, :]
                e = jnp.exp(-jnp.abs(bb - ref)).reshape(c, dk)
            qk = (jnp.where(tgt, s.q, s.k) * e).astype(BF16)
            a = _dot_nt(qk, qk).astype(BF16)
            amat = a if amat is None else jnp.where((tt // blk) == (ss // blk), a, amat)
            m = m // 2
        s.amat = jnp.where(strict, amat, jnp.zeros_like(amat))
        s.edge = s.b[0:1, :] if s.rev else s.b[c - 1:c, :]
        s.qi = (s.q * jnp.exp(s.b)).astype(BF16)
        s.kl = (s.k * jnp.exp(s.edge - s.b)).astype(BF16)


def _hg_intra_stage(streams):
    for s in streams:
        s.o = s.o + _dot(s.amat, s.vb)


def _hg_state_stage(streams):
    for s in streams:
        st = s.get_state()
        s.o = s.o + _dot_nt(s.qi, st.astype(BF16))
        s.put_state(jnp.exp(s.edge) * st + _dot_tn(s.vb, s.kl))


def _hgrn2_kernel(q_ref, ff_ref, fb_ref, v_ref, g_ref, lb_ref, nw_ref, out_ref, of_ref, ob_ref, st_ref, qs_ref,
                  pb_ref, po_ref, pe_ref):
    L, dk = q_ref.shape
    c = HG_CHUNK
    nc = L // c

    lbp = lb_ref[...]
    mx = jnp.max(lbp, axis=0, keepdims=True)
    ex = jnp.exp(lbp - mx)
    lb = ex[0:1, :] / jnp.sum(ex, axis=0, keepdims=True)

    st_ref[...] = jnp.zeros_like(st_ref)
    qs_ref[...] = _silu(q_ref[...])

    per_step = HG_STREAM_CHUNKS
    n_steps = nc // per_step

    def shells(j):
        out = []
        for i in range(per_step):
            for rev in (False, True):
                s = _HgStream()
                chunk = (nc - 1 - j * per_step - i) if rev else (j * per_step + i)
                s.t0 = chunk * c if isinstance(chunk, int) else pl.multiple_of(chunk * c, c)
                s.rev = rev
                s.slot = 1 if rev else 0
                s.get_state = functools.partial(lambda slot: st_ref[slot], s.slot)
                s.put_state = functools.partial(st_ref.__setitem__, s.slot)
                out.append(s)
        return out

    def front(j):
        streams = shells(j)
        for s in streams:
            sig = _sigmoid((fb_ref if s.rev else ff_ref)[pl.ds(s.t0, c), :])
            s.f = lb + (1.0 - lb) * sig
            s.k = (1.0 - lb) * (1.0 - sig)
            s.logf = jnp.log(s.f)
            s.q = qs_ref[pl.ds(s.t0, c), :]
            s.v = v_ref[pl.ds(s.t0, c), :]
        return streams

    def save(streams):
        for i, s in enumerate(streams):
            pb_ref[i, 0] = s.amat
            pb_ref[i, 1] = s.vb
            pb_ref[i, 2] = s.qi
            pb_ref[i, 3] = s.kl
            po_ref[i] = s.o
            pe_ref[i] = jnp.broadcast_to(s.edge, (SUBLANES, dk))

    def load(j):
        streams = shells(j)
        for i, s in enumerate(streams):
            s.amat, s.vb, s.qi, s.kl = pb_ref[i, 0], pb_ref[i, 1], pb_ref[i, 2], pb_ref[i, 3]
            s.o = po_ref[i]
            s.edge = pe_ref[i, 0:1, :]
        return streams

    def emit(streams):
        for s in streams:
            (ob_ref if s.rev else of_ref)[pl.ds(s.t0, c), :] = s.o

    first = front(0)
    _hg_cumsum_stage(first)
    _hg_levels_stage(first)
    save(first)

    def body(j, carry):
        old = load(j - 1)
        new = front(j)
        _hg_intra_stage(old)
        _hg_state_stage(old[:2])
        _hg_cumsum_stage(new)
        _hg_state_stage(old[2:])
        _hg_levels_stage(new)
        emit(old)
        save(new)
        return carry

    lax.fori_loop(1, n_steps, body, 0)

    last = load(n_steps - 1)
    _hg_intra_stage(last)
    _hg_state_stage(last)
    emit(last)

    o = of_ref[...] + ob_ref[...]
    o = o * lax.rsqrt(jnp.mean(o * o, axis=1, keepdims=True) + EPS)
    y = o * nw_ref[...] * _silu(g_ref[...])
    out_ref[...] = y.astype(out_ref.dtype)


def _hgrn2(proj, hg_lb, norm_w, B, L):
    dk = HG_D
    H = HG_HEADS
    base = 4 * ML_WIDTH // dk
    return pl.pallas_call(
        _hgrn2_kernel,
        grid=(B, H),
        in_specs=[
            pl.BlockSpec((L, dk), lambda b, h: (b, base + h)),
            pl.BlockSpec((L, dk), lambda b, h: (b, base + H + h)),
            pl.BlockSpec((L, dk), lambda b, h: (b, base + 2 * H + h)),
            pl.BlockSpec((L, dk), lambda b, h: (b, base + 3 * H + h)),
            pl.BlockSpec((L, dk), lambda b, h: (b, base + 4 * H + h)),
            pl.BlockSpec((2, dk), lambda b, h: (0, h)),
            pl.BlockSpec((1, dk), lambda b, h: (0, h)),
        ],
        out_specs=pl.BlockSpec((L, dk), lambda b, h: (b, h)),
        out_shape=jax.ShapeDtypeStruct((B * L, HG_WIDTH), BF16),
        scratch_shapes=[
            pltpu.VMEM((L, dk), F32), pltpu.VMEM((L, dk), F32),
            pltpu.VMEM((2, dk, dk), F32),
            pltpu.VMEM((L, dk), F32),
            pltpu.VMEM((2 * HG_STREAM_CHUNKS, 4, HG_CHUNK, dk), BF16),
            pltpu.VMEM((2 * HG_STREAM_CHUNKS, HG_CHUNK, dk), F32),
            pltpu.VMEM((2 * HG_STREAM_CHUNKS, SUBLANES, dk), F32),
        ],
        compiler_params=_cparams(("arbitrary", "arbitrary")),
        name="hgrn2",
    )(proj, proj, proj, proj, proj, hg_lb, norm_w)


def _outproj_kernel(x_ref, ml_ref, hg_ref, eg_ref, eb_ref, wo_ref, g1_ref, b1_ref, wrh_ref, wrl_ref,
                    x1_ref, lgt_ref, ya_ref, yb_ref):
    i = pl.program_id(0)
    half = ml_ref.shape[1]
    n_exp = lgt_ref.shape[0]

    @pl.when(i == 0)
    def _():
        yb_ref[...] = jnp.zeros_like(yb_ref)

    def step(y_new, y_old):
        mix = _dot(ml_ref[...], wo_ref[0:half, :]) + _dot(hg_ref[...], wo_ref[half:, :])
        y_new[...] = ALPHA * _layer_norm(x_ref[...], eg_ref[...], eb_ref[...]) + mix
        x1 = _layer_norm(y_old[...], g1_ref[...], b1_ref[...])
        x1_ref[...] = x1
        xh, xl = _split_bf16(x1)
        lgt = _dot_nt(wrh_ref[...], xh) + _dot_nt(wrh_ref[...], xl) + _dot_nt(wrl_ref[...], xh)
        lgt_ref[...] = lgt[0:n_exp, :]

    @pl.when(lax.rem(i, 2) == 0)
    def _():
        step(ya_ref, yb_ref)

    @pl.when(lax.rem(i, 2) == 1)
    def _():
        step(yb_ref, ya_ref)


def _outproj(x, ml, hg, eg, eb, wo, g1, b1, wrt_hi, wrt_lo, n_exp, tm=256):
    n, d = x.shape
    n_tiles = n // tm
    row = lambda i: (jnp.minimum(i, n_tiles - 1), 0)
    prev_row = lambda i: (jnp.maximum(i - 1, 0), 0)
    fixed = lambda i: (0, 0)
    return pl.pallas_call(
        _outproj_kernel,
        grid=(n_tiles + 1,),
        in_specs=[
            pl.BlockSpec((tm, d), row),
            pl.BlockSpec((tm, ML_WIDTH), row),
            pl.BlockSpec((tm, HG_WIDTH), row),
            pl.BlockSpec((1, d), fixed),
            pl.BlockSpec((1, d), fixed),
            pl.BlockSpec((d, d), fixed),
            pl.BlockSpec((1, d), fixed),
            pl.BlockSpec((1, d), fixed),
            pl.BlockSpec((LANES, d), fixed),
            pl.BlockSpec((LANES, d), fixed),
        ],
        out_specs=[pl.BlockSpec((tm, d), prev_row),
                   pl.BlockSpec((n_exp, tm), lambda i: (0, jnp.maximum(i - 1, 0)))],
        out_shape=[jax.ShapeDtypeStruct((n, d), F32), jax.ShapeDtypeStruct((n_exp, n), F32)],
        scratch_shapes=[pltpu.VMEM((tm, d), F32), pltpu.VMEM((tm, d), F32)],
        compiler_params=_cparams(("arbitrary",)),
        name="outproj",
    )(x, ml, hg, eg, eb, wo, g1, b1, wrt_hi, wrt_lo)


def _pad_cols(w, width=LANES):
    return jnp.pad(w, ((0, 0), (0, width - w.shape[1])))


def _token_mixer_stage(x, emb_ln_g, emb_ln_b, w_in, conv_w, ml_igate_b, ml_fgate_b, ml_norm_w,
                       hg_lb, hg_norm_w, w_out, ln1_g, ln1_b, w_router, B, L):
    d = D_MODEL
    g0 = 4 * ML_WIDTH
    w_main = jnp.concatenate([w_in[:, :g0], w_in[:, g0 + 16:]], axis=1).astype(BF16)
    wg = w_in[:, g0:g0 + 16].reshape(d, 4, ML_HEADS).transpose(0, 2, 1).reshape(d, 16)
    wg_both = jnp.concatenate(_split_bf16(_pad_cols(wg)), axis=1)
    eg = emb_ln_g.reshape(1, d)
    eb = emb_ln_b.reshape(1, d)
    proj, gate_tok, gate_t = _ln_proj(x, eg, eb, w_main, wg_both)

    gate_row = gate_t.reshape(ML_HEADS, 4, B, L).transpose(2, 0, 1, 3)
    bias = jnp.concatenate([ml_igate_b[0], ml_igate_b[1], ml_fgate_b[0], ml_fgate_b[1]]).astype(F32)
    ml = _mlstm(proj, conv_w, gate_tok, gate_row, bias, ml_norm_w.reshape(1, ML_WIDTH), B, L)
    hg = _hgrn2(proj, hg_lb, hg_norm_w.reshape(1, HG_WIDTH), B, L)

    wrt_hi, wrt_lo = _split_bf16(_pad_cols(w_router).T)
    return _outproj(x, ml, hg, eg, eb, w_out.astype(BF16), ln1_g.reshape(1, d), ln1_b.reshape(1, d),
                    wrt_hi, wrt_lo, w_router.shape[1])


def _excl_token_cumsum(mask):
    E, R, ln = mask.shape
    mf = jnp.where(mask, 1.0, 0.0)
    mb = mf.astype(BF16)
    upper = jnp.where(lax.broadcasted_iota(I32, (ln, ln), 0) <= lax.broadcasted_iota(I32, (ln, ln), 1),
                      1.0, 0.0).astype(BF16)
    lower = jnp.where(lax.broadcasted_iota(I32, (R, R), 1) < lax.broadcasted_iota(I32, (R, R), 0),
                      1.0, 0.0).astype(BF16)
    ones = jnp.ones((ln, ln), BF16)
    within = _dot(mb.reshape(E * R, ln), upper).reshape(E, R, ln)
    rows = jnp.stack([_dot(_dot(lower, mb[e]).astype(BF16), ones) for e in range(E)], axis=0)
    return within - mf + rows


def _select_kernel(lg_ref, pos_ref, posm_ref, wts_ref, *, cap):
    E = lg_ref.shape[0]
    lg = lg_ref[...]
    mx = jnp.max(lg, axis=0, keepdims=True)
    ex = jnp.exp(lg - mx)
    aff = ex / jnp.sum(ex, axis=0, keepdims=True)

    def count(mask):
        ones = jnp.where(mask, 1.0, 0.0)
        return jnp.sum(jnp.sum(ones, axis=2, keepdims=True), axis=1, keepdims=True)

    def body(i, tbits):
        cand = tbits | lax.shift_left(jnp.int32(1), 30 - i)
        cnt = count(aff >= lax.bitcast_convert_type(cand, F32))
        return jnp.where(cnt >= cap, cand, tbits)

    tbits = lax.fori_loop(0, 31, body, jnp.zeros((E, 1, 1), I32))
    thr = lax.bitcast_convert_type(tbits, F32)
    nxt = lax.bitcast_convert_type(tbits + 1, F32)
    above = aff >= nxt
    band = jnp.logical_and(aff >= thr, jnp.logical_not(above))
    need = cap - count(above)
    sel = jnp.logical_or(above, jnp.logical_and(band, _excl_token_cumsum(band) < need))
    pos = _excl_token_cumsum(sel).astype(I32)
    pos_ref[...] = pos
    posm_ref[...] = jnp.where(sel, pos, -1)
    wts_ref[...] = jnp.where(sel, aff, 0.0)


def _select(lg_t, cap):
    E, R, ln = lg_t.shape
    full = pl.BlockSpec((E, R, ln), lambda i: (0, 0, 0))
    return pl.pallas_call(
        functools.partial(_select_kernel, cap=cap),
        grid=(1,),
        in_specs=[full],
        out_specs=[full, full, full],
        out_shape=[jax.ShapeDtypeStruct((E, R, ln), I32), jax.ShapeDtypeStruct((E, R, ln), I32),
                   jax.ShapeDtypeStruct((E, R, ln), F32)],
        compiler_params=_cparams(("arbitrary",)),
        name="select",
    )(lg_t)


TOK_TILE = 256
SLOT_CHUNK = 64


def _pack_bf16_pairs(x):
    h = x.shape[1] // 2
    lo = lax.shift_right_logical(lax.bitcast_convert_type(x[:, :h], U32), jnp.uint32(16))
    hi = lax.bitcast_convert_type(x[:, h:], U32) & jnp.uint32(0xFFFF0000)
    return hi | lo


def _unpack_bf16_pairs(w):
    lo = lax.bitcast_convert_type(lax.shift_left(w, jnp.uint32(16)), F32).astype(BF16)
    hi = lax.bitcast_convert_type(w & jnp.uint32(0xFFFF0000), F32).astype(BF16)
    return lo, hi


def _dispatch_kernel(off_ref, base_ref, cnt_ref, xa_ref, xb_ref, pos_ref, xe_ref,
                     x16_ref, res_ref, stage_ref, ostage_ref, carry_ref, sem, osem, *, tiles_a, cap_total, pad):
    E = N_EXPERTS
    CH = SLOT_CHUNK
    SUB = SUBLANES
    tc = xa_ref.shape[0]
    t = pl.program_id(0)
    nt = pl.num_programs(0)
    slot = lax.rem(t, 2)

    @pl.when(t < tiles_a)
    def _():
        x16_ref[...] = xa_ref[...].astype(BF16)

    @pl.when(t >= tiles_a)
    def _():
        x16_ref[...] = xb_ref[...].astype(BF16)

    xb = x16_ref[...]

    @pl.when(t == 0)
    def _():
        carry_ref[...] = jnp.zeros_like(carry_ref)

    base = [base_ref[t * E + e] for e in range(E)]
    cnt = [cnt_ref[t * E + e] for e in range(E)]
    al = [pl.multiple_of((b // SUB) * SUB, SUB) for b in base]
    first = [off_ref[t * E + e] - (base[e] - al[e]) for e in range(E)]

    def onehot(e, start, rows):
        kio = lax.broadcasted_iota(I32, (rows, tc), 0)
        rel = pos_ref[e:e + 1, :] - (first[e] + start)
        return jnp.where(rel == kio, 1.0, 0.0).astype(BF16)

    ot = jnp.concatenate([onehot(e, 0, CH) for e in range(E)], axis=0)
    res_ref[...] = _dot(ot, xb)
    for e in range(E):
        res_ref[e * CH:e * CH + SUB, :] += carry_ref[e * SUB:(e + 1) * SUB, :]
    stage_ref[slot] = _pack_bf16_pairs(res_ref[...])

    nxt = [((base[e] + cnt[e]) // SUB) * SUB - al[e] for e in range(E)]
    oc = jnp.concatenate([onehot(e, nxt[e], SUB) for e in range(E)], axis=0)
    new_carry = _dot(oc, xb)
    for e in range(E):
        keep = jnp.where(nxt[e] == 0, 1.0, 0.0)
        carry_ref[e * SUB:(e + 1) * SUB, :] = new_carry[e * SUB:(e + 1) * SUB, :] + keep * carry_ref[e * SUB:(e + 1) * SUB, :]

    def main_copy(s, e, row):
        return pltpu.make_async_copy(stage_ref.at[s, pl.ds(e * CH, CH)], xe_ref.at[pl.ds(row, CH)], sem.at[s])

    @pl.when(t > 0)
    def _():
        for e in range(E):
            main_copy(1 - slot, e, 0).wait()

    for e in range(E):
        main_copy(slot, e, al[e]).start(priority=e % 2)

    for e in range(E):
        nch = (base[e] - al[e] + cnt[e] + CH - 1) // CH

        def body(c, carry, e=e):
            ostage_ref[...] = _pack_bf16_pairs(_dot(onehot(e, c * CH, CH), xb))
            row = pl.multiple_of(al[e] + c * CH, SUB)
            cp = pltpu.make_async_copy(ostage_ref, xe_ref.at[pl.ds(row, CH)], osem)
            cp.start()
            cp.wait()
            return carry

        lax.fori_loop(1, nch, body, 0)

    @pl.when(t == nt - 1)
    def _():
        for e in range(E):
            main_copy(slot, e, 0).wait()
        ostage_ref[...] = jnp.zeros_like(ostage_ref)
        fills = [pltpu.make_async_copy(ostage_ref, xe_ref.at[pl.ds(e * (cap_total + pad) + cap_total + j * CH, CH)], osem)
                 for e in range(E) for j in range(pad // CH)]
        for cp in fills:
            cp.start()
        for cp in fills:
            cp.wait()


def _dispatch(off, base, cnt, x1_a, x1_b, pos_rows, cap_total, pad):
    d = x1_a.shape[1]
    E = N_EXPERTS
    tc = TOK_TILE
    tiles_a = x1_a.shape[0] // tc
    tiles_b = x1_b.shape[0] // tc
    grid_spec = pltpu.PrefetchScalarGridSpec(
        num_scalar_prefetch=3,
        grid=(tiles_a + tiles_b,),
        in_specs=[
            pl.BlockSpec((tc, d), lambda i, *_: (jnp.minimum(i, tiles_a - 1), 0)),
            pl.BlockSpec((tc, d), lambda i, *_: (jnp.maximum(i - tiles_a, 0), 0)),
            pl.BlockSpec((E, tc), lambda i, *_: (0, i)),
        ],
        out_specs=pl.BlockSpec(memory_space=pl.ANY),
        scratch_shapes=[
            pltpu.VMEM((tc, d), BF16),
            pltpu.VMEM((E * SLOT_CHUNK, d), F32),
            pltpu.VMEM((2, E * SLOT_CHUNK, d // 2), U32),
            pltpu.VMEM((SLOT_CHUNK, d // 2), U32),
            pltpu.VMEM((E * SUBLANES, d), F32),
            pltpu.SemaphoreType.DMA((2,)),
            pltpu.SemaphoreType.DMA(()),
        ],
    )
    return pl.pallas_call(
        functools.partial(_dispatch_kernel, tiles_a=tiles_a, cap_total=cap_total, pad=pad),
        grid_spec=grid_spec,
        out_shape=jax.ShapeDtypeStruct((E * (cap_total + pad), d // 2), U32),
        compiler_params=_cparams(("arbitrary",)),
        name="dispatch",
    )(off, base, cnt, x1_a, x1_b, pos_rows)


def _ffn_hidden_kernel(xe_ref, wg_ref, wu_ref, hid_ref, wg16_ref, wu16_ref):
    @pl.when(pl.program_id(2) == 0)
    def _():
        wg16_ref[...] = wg_ref[0].astype(BF16)
        wu16_ref[...] = wu_ref[0].astype(BF16)

    lo, hi = _unpack_bf16_pairs(xe_ref[0])
    h = lo.shape[1]
    gate = _dot(lo, wg16_ref[0:h, :]) + _dot(hi, wg16_ref[h:, :])
    up = _dot(lo, wu16_ref[0:h, :]) + _dot(hi, wu16_ref[h:, :])
    hid_ref[0] = (_silu(gate) * up).astype(hid_ref.dtype)


def _ffn_down_kernel(hid_ref, wd_ref, out_ref, wd16_ref):
    @pl.when(pl.program_id(2) == 0)
    def _():
        wd16_ref[...] = wd_ref[0].astype(BF16)

    out_ref[0] = _dot(hid_ref[0], wd16_ref[...]).astype(out_ref.dtype)


def _ffn(xe, wg, wu, wd, cap_total, tf=512, tn=1024):
    E, _, dh = xe.shape
    d = 2 * dh
    ff = wg.shape[2]
    tm = next(t for t in (1024, 512, 256, 128) if cap_total % t == 0)
    hid = pl.pallas_call(
        _ffn_hidden_kernel,
        grid=(E, ff // tf, cap_total // tm),
        in_specs=[
            pl.BlockSpec((1, tm, dh), lambda e, f, i: (e, i, 0)),
            pl.BlockSpec((1, d, tf), lambda e, f, i: (e, 0, f)),
            pl.BlockSpec((1, d, tf), lambda e, f, i: (e, 0, f)),
        ],
        out_specs=pl.BlockSpec((1, tm, tf), lambda e, f, i: (e, i, f)),
        out_shape=jax.ShapeDtypeStruct((E, cap_total, ff), BF16),
        scratch_shapes=[pltpu.VMEM((d, tf), BF16), pltpu.VMEM((d, tf), BF16)],
        compiler_params=_cparams(("arbitrary", "arbitrary", "arbitrary")),
        name="ffn_hidden",
    )(xe, wg, wu)
    return pl.pallas_call(
        _ffn_down_kernel,
        grid=(E, d // tn, cap_total // tm),
        in_specs=[
            pl.BlockSpec((1, tm, ff), lambda e, c, i: (e, i, 0)),
            pl.BlockSpec((1, ff, tn), lambda e, c, i: (e, 0, c)),
        ],
        out_specs=pl.BlockSpec((1, tm, tn), lambda e, c, i: (e, i, c)),
        out_shape=jax.ShapeDtypeStruct((E, cap_total, d), BF16),
        scratch_shapes=[pltpu.VMEM((ff, tn), BF16)],
        compiler_params=_cparams(("arbitrary", "arbitrary", "arbitrary")),
        name="ffn_down",
    )(hid, wd)


def _combine_kernel(base_ref, cnt_ref, x1_ref, grow_ref, w_ref, g2_ref, b2_ref, ye_ref, out_ref,
                    buf_ref, obuf_ref, acc_ref, sem, osem, *, rows_total):
    E = N_EXPERTS
    CH = SLOT_CHUNK
    tc = x1_ref.shape[0]
    t = pl.program_id(0)
    nt = pl.num_programs(0)
    slot = lax.rem(t, 2)
    last = rows_total - CH
    ALIGN = 2 * SUBLANES

    def window(step, e):
        al = (base_ref[step * E + e] // ALIGN) * ALIGN
        return al, pl.multiple_of(jnp.minimum(al, last), ALIGN)

    def chunk_copy(s, e, start):
        return pltpu.make_async_copy(ye_ref.at[pl.ds(start, CH)], buf_ref.at[s, pl.ds(e * CH, CH)], sem.at[s, e])

    @pl.when(t == 0)
    def _():
        for e in range(E):
            chunk_copy(0, e, window(0, e)[1]).start(priority=e % 2)

    @pl.when(t + 1 < nt)
    def _():
        for e in range(E):
            chunk_copy(1 - slot, e, window(t + 1, e)[1]).start(priority=e % 2)

    kio = lax.broadcasted_iota(I32, (CH, tc), 0)

    def weights(e, start, lo):
        grow = grow_ref[e:e + 1, :]
        hit = jnp.logical_and(grow - start == kio, grow >= lo)
        return jnp.where(hit, w_ref[e:e + 1, :], 0.0).astype(BF16)

    at = jnp.concatenate([weights(e, window(t, e)[1], 0) for e in range(E)], axis=0)
    for e in range(E):
        chunk_copy(slot, e, 0).wait()
    acc_ref[...] = ALPHA * x1_ref[...] + _dot_tn(at, buf_ref[slot])

    for e in range(E):
        al = window(t, e)[0]
        nch = (base_ref[t * E + e] - al + cnt_ref[t * E + e] + CH - 1) // CH

        def body(c, carry, e=e, al=al):
            lo = al + c * CH
            start = pl.multiple_of(jnp.minimum(lo, last), ALIGN)
            cp = pltpu.make_async_copy(ye_ref.at[pl.ds(start, CH)], obuf_ref, osem)
            cp.start()
            cp.wait()
            acc_ref[...] += _dot_tn(weights(e, start, lo), obuf_ref[...])
            return carry

        lax.fori_loop(1, nch, body, 0)

    out_ref[...] = _layer_norm(acc_ref[...], g2_ref[...], b2_ref[...])


def _combine(base, cnt, x1, grow_rows, w_rows, g2, b2, ye, rows_total):
    n, d = x1.shape
    E = N_EXPERTS
    tc = TOK_TILE
    grid_spec = pltpu.PrefetchScalarGridSpec(
        num_scalar_prefetch=2,
        grid=(n // tc,),
        in_specs=[
            pl.BlockSpec((tc, d), lambda i, *_: (i, 0)),
            pl.BlockSpec((E, tc), lambda i, *_: (0, i)),
            pl.BlockSpec((E, tc), lambda i, *_: (0, i)),
            pl.BlockSpec((1, d), lambda i, *_: (0, 0)),
            pl.BlockSpec((1, d), lambda i, *_: (0, 0)),
            pl.BlockSpec(memory_space=pl.ANY),
        ],
        out_specs=pl.BlockSpec((tc, d), lambda i, *_: (i, 0)),
        scratch_shapes=[
            pltpu.VMEM((2, E * SLOT_CHUNK, d), BF16),
            pltpu.VMEM((SLOT_CHUNK, d), BF16),
            pltpu.VMEM((tc, d), F32),
            pltpu.SemaphoreType.DMA((2, E)),
            pltpu.SemaphoreType.DMA(()),
        ],
    )
    return pl.pallas_call(
        functools.partial(_combine_kernel, rows_total=rows_total),
        grid_spec=grid_spec,
        out_shape=jax.ShapeDtypeStruct((n, d), F32),
        compiler_params=_cparams(("arbitrary",)),
        name="combine",
    )(base, cnt, x1, grow_rows, w_rows, g2, b2, ye)


def _routing_tables(pos, posm, wts, cap, slot0, cap_total, pad):
    E = N_EXPERTS
    n = pos.shape[1] * pos.shape[2]
    tiles = n // TOK_TILE
    eidx = jnp.arange(E, dtype=I32)
    off = pos.reshape(E, n)[:, ::TOK_TILE].T
    cnt = jnp.concatenate([off[1:], jnp.full((1, E), cap, I32)], axis=0) - off
    base_x = off + eidx[None, :] * (cap_total + pad) + slot0
    base_y = off + eidx[None, :] * cap_total + slot0
    pos_rows = posm.reshape(E, n)
    grow_rows = jnp.where(pos_rows >= 0, pos_rows + eidx[:, None] * cap_total + slot0, -1)
    flat = lambda a: a.reshape(tiles * E)
    return flat(off), flat(base_x), flat(base_y), flat(cnt), pos_rows, grow_rows, wts.reshape(E, n)


def kernel(x_prompt, x_sample, emb_ln_g, emb_ln_b, w_in, conv_w, ml_igate_b, ml_fgate_b, ml_norm_w, hg_lb, hg_norm_w, w_out, ln1_g, ln1_b, w_router, w_gate, w_up, w_down, ln2_g, ln2_b):
    E = N_EXPERTS
    d = D_MODEL
    groups = (x_prompt, x_sample)
    caps = [EC_FACTOR * x.shape[0] * x.shape[1] // E for x in groups]
    cap_total = sum(caps)
    pad = 2 * SLOT_CHUNK
    assert all(c % (2 * SUBLANES) == 0 for c in caps)

    staged = []
    tables = []
    slot0 = 0
    for x, cap in zip(groups, caps):
        B, L, _ = x.shape
        n = B * L
        x1, lg = _token_mixer_stage(x.reshape(n, d), emb_ln_g, emb_ln_b, w_in[0], conv_w[0], ml_igate_b[0],
                                    ml_fgate_b[0], ml_norm_w[0], hg_lb, hg_norm_w[0], w_out[0], ln1_g[0],
                                    ln1_b[0], w_router[0], B, L)
        lg_t = lg.reshape(E, n // LANES, LANES)
        pos, posm, wts = _select(lg_t, cap)
        off, base_x, base_y, cnt, pos_rows, grow_t, w_t = _routing_tables(pos, posm, wts, cap, slot0, cap_total, pad)
        tables.append((off, base_x, cnt, pos_rows))
        staged.append((x1, base_y, cnt, grow_t, w_t, (B, L)))
        slot0 += cap

    off, base_x, cnt, pos_rows = (jnp.concatenate(parts, axis=-1) for parts in zip(*tables))
    xe = _dispatch(off, base_x, cnt, staged[0][0], staged[1][0], pos_rows, cap_total, pad)

    ye = _ffn(xe.reshape(E, cap_total + pad, d // 2), w_gate[0], w_up[0], w_down[0], cap_total)
    ye = ye.reshape(E * cap_total, d)

    g2 = ln2_g[0].reshape(1, d)
    b2 = ln2_b[0].reshape(1, d)
    outs = []
    for x1, base_y, cnt, grow_t, w_t, (B, L) in staged:
        y = _combine(base_y, cnt, x1, grow_t, w_t, g2, b2, ye, E * cap_total)
        outs.append(y.reshape(B, L, d))
    return tuple(outs)
```
